```python
import math
import jax, jax.numpy as jnp
from jax import lax
import numpy as np

D_MODEL = 1024
BATCH = 1
SEQ = 16384
DEPTH = 1
DEC_BATCH = 128
DEC_SEQ = 4
PAST_LEN = 16384
PAGE_SIZE = 128

HEAD_DIM = 64
SWA_Q_HEADS = 8
SWA_KV_HEADS = 2
SWA_GROUP = SWA_Q_HEADS // SWA_KV_HEADS
WINDOW = 128
ROPE_THETA = 500000.0
ROPE_DIM = HEAD_DIM // 4
GLA_HEADS = 4
GLA_DK = 64
GLA_DV = 128
GLA_LOWRANK = 16
GLA_GATE_NORM = 16.0
GLA_CHUNK = 64
MIX_WIDTH = SWA_Q_HEADS * HEAD_DIM + GLA_HEADS * GLA_DV
IN_SIZES = (SWA_Q_HEADS * HEAD_DIM, SWA_KV_HEADS * HEAD_DIM, SWA_KV_HEADS * HEAD_DIM,
            GLA_HEADS * GLA_DK, GLA_HEADS * GLA_DK, GLA_HEADS * GLA_DV, GLA_HEADS * GLA_DV, GLA_LOWRANK)
IN_COLS = sum(IN_SIZES)
PEER_HEADS = 8
PEER_NKEYS = 128
PEER_N = PEER_NKEYS * PEER_NKEYS
PEER_DKEY = 128
PEER_TOPK = 16
PEER_BLOCK = 128
NORM_EPS = 1e-6

kernel_name = 'hybrid_swa_gla_peer_step'


def rms_norm(x, g):
    xf = x.astype(jnp.float32)
    xf = xf * lax.rsqrt(jnp.mean(xf * xf, axis=-1, keepdims=True) + NORM_EPS)
    return (xf * g.astype(jnp.float32)).astype(x.dtype)


def partial_rope(x, pos):
    half = ROPE_DIM // 2
    inv_freq = ROPE_THETA ** (-jnp.arange(half, dtype=jnp.float32) / half)
    ang = pos.astype(jnp.float32)[:, None] * inv_freq[None, :]
    cos = jnp.cos(ang)[:, None, :]
    sin = jnp.sin(ang)[:, None, :]
    xr = x[..., :ROPE_DIM].astype(jnp.float32)
    x1, x2 = xr[..., :half], xr[..., half:]
    rot = jnp.concatenate([x1 * cos - x2 * sin, x2 * cos + x1 * sin], axis=-1).astype(x.dtype)
    return jnp.concatenate([rot, x[..., ROPE_DIM:]], axis=-1)


def _mixer_inputs(x, pos, attn_norm, w_in, q_norm, k_norm, w_gate, b_gate):
    B, T, _ = x.shape
    h = rms_norm(x, attn_norm)
    offs = np.cumsum(IN_SIZES)[:-1].tolist()
    q, k, v, gq, gk, gv, gg, ga = jnp.split(h @ w_in, offs, axis=-1)
    q = partial_rope(rms_norm(q.reshape(B, T, SWA_Q_HEADS, HEAD_DIM), q_norm), pos)
    k = partial_rope(rms_norm(k.reshape(B, T, SWA_KV_HEADS, HEAD_DIM), k_norm), pos)
    v = v.reshape(B, T, SWA_KV_HEADS, HEAD_DIM)
    gq = gq.reshape(B, T, GLA_HEADS, GLA_DK) * (GLA_DK ** -0.5)
    gk = gk.reshape(B, T, GLA_HEADS, GLA_DK)
    gv = gv.reshape(B, T, GLA_HEADS, GLA_DV)
    logf = jax.nn.log_sigmoid((ga @ w_gate + b_gate).astype(jnp.float32)) / GLA_GATE_NORM
    logf = logf.reshape(B, T, GLA_HEADS, GLA_DK)
    return q, k, v, gq, gk, gv, gg, logf


def _sink_attention(q, k, v, mask, sinks):
    s = jnp.einsum('...qhgd,...khd->...hgqk', q, k).astype(jnp.float32) * (HEAD_DIM ** -0.5)
    s = jnp.where(mask, s, -jnp.inf)
    sink = sinks.astype(jnp.float32).reshape(SWA_KV_HEADS, SWA_GROUP, 1, 1)
    m = jnp.maximum(jnp.max(s, axis=-1, keepdims=True), sink)
    p = jnp.exp(s - m)
    p = p / (jnp.sum(p, axis=-1, keepdims=True) + jnp.exp(sink - m))
    return jnp.einsum('...hgqk,...khd->...qhgd', p.astype(v.dtype), v)


def _swa_prompt(q, k, v, sinks):
    B, T = q.shape[:2]
    nb = T // WINDOW
    qb = q.reshape(B, nb, WINDOW, SWA_KV_HEADS, SWA_GROUP, HEAD_DIM)

    def band(a):
        ab = a.reshape(B, nb, WINDOW, SWA_KV_HEADS, HEAD_DIM)
        prev = jnp.pad(ab[:, :-1], ((0, 0), (1, 0), (0, 0), (0, 0), (0, 0)))
        return jnp.concatenate([prev, ab], axis=2)

    qi = jnp.arange(WINDOW)[:, None] + WINDOW
    kj = jnp.arange(2 * WINDOW)[None, :]
    rel = qi - kj
    blk = jnp.arange(nb)[:, None, None]
    mask = (rel >= 0) & (rel <= WINDOW) & (blk * WINDOW + kj - WINDOW >= 0)
    o = _sink_attention(qb, band(k), band(v), mask[:, None, None], sinks)
    return o.reshape(B, T, SWA_Q_HEADS * HEAD_DIM)


def _swa_sample(q, k, v, cache_k, cache_v, sinks):
    B, T = q.shape[:2]
    kk = jnp.concatenate([cache_k, k], axis=1)
    vv = jnp.concatenate([cache_v, v], axis=1)
    rel = (jnp.arange(T)[:, None] + WINDOW) - jnp.arange(WINDOW + T)[None, :]
    mask = (rel >= 0) & (rel <= WINDOW)
    o = _sink_attention(q.reshape(B, T, SWA_KV_HEADS, SWA_GROUP, HEAD_DIM), kk, vv, mask, sinks)
    return o.reshape(B, T, SWA_Q_HEADS * HEAD_DIM), kk[:, -WINDOW:], vv[:, -WINDOW:]


def _gla_chunked(q, k, v, logf, s0):
    B, T, H, DK = q.shape
    DV = v.shape[-1]
    C = math.gcd(T, GLA_CHUNK)
    nc = T // C

    def chunks(a):
        return jnp.moveaxis(a.reshape(B, nc, C, *a.shape[2:]), 1, 0)

    causal = jnp.tril(jnp.ones((C, C), dtype=bool))[None, :, :, None, None]

    def step(S, inp):
        qc, kc, vc, fc = (t.astype(jnp.float32) for t in inp)
        b = jnp.cumsum(fc, axis=1)
        diff = jnp.where(causal, b[:, :, None] - b[:, None, :], -jnp.inf)
        attn = jnp.einsum('bijhd,bjhd->bhij', qc[:, :, None] * jnp.exp(diff), kc)
        o = (jnp.einsum('bhij,bjhv->bihv', attn, vc)
             + jnp.einsum('bihd,bhdv->bihv', qc * jnp.exp(b), S))
        b_last = b[:, -1]
        S = (jnp.exp(b_last)[..., None] * S
             + jnp.einsum('bjhd,bjhv->bhdv', kc * jnp.exp(b_last[:, None] - b), vc))
        return S, o

    S, o = lax.scan(step, s0.astype(jnp.float32), (chunks(q), chunks(k), chunks(v), chunks(logf)))
    o = jnp.moveaxis(o, 0, 1).reshape(B, T, H, DV)
    return o.astype(v.dtype), S.astype(s0.dtype)


def _gla_output(o, gg, gla_norm):
    B, T = o.shape[:2]
    o = rms_norm(o, gla_norm).reshape(B, T, GLA_HEADS * GLA_DV)
    return o * jax.nn.silu(gg)


def _peer(h, peer_wq, peer_keys, peer_u, peer_v):
    B, T, D = h.shape
    n = B * T
    blk = math.gcd(n, PEER_BLOCK)
    hb = h.reshape(n // blk, blk, D)

    def one_block(xb):
        q = (xb @ peer_wq).reshape(blk, PEER_HEADS, 2, PEER_DKEY)
        s = jnp.einsum('nhpd,hpkd->nhpk', q, peer_keys).astype(jnp.float32)
        sv, si = lax.top_k(s, PEER_TOPK)
        cand = (sv[:, :, 0, :, None] + sv[:, :, 1, None, :]).reshape(blk, PEER_HEADS, PEER_TOPK * PEER_TOPK)
        cidx = (si[:, :, 0, :, None] * PEER_NKEYS + si[:, :, 1, None, :]).reshape(blk, PEER_HEADS, PEER_TOPK * PEER_TOPK)
        fv, fi = lax.top_k(cand, PEER_TOPK)
        eidx = jnp.take_along_axis(cidx, fi, axis=-1)
        g = jax.nn.softmax(fv, axis=-1)
        u = jnp.take(peer_u, eidx, axis=0)
        act = jax.nn.gelu(jnp.einsum('nhkd,nd->nhk', u, xb).astype(jnp.float32), approximate=False)
        v = jnp.take(peer_v, eidx, axis=0)
        return jnp.einsum('nhk,nhkd->nd', (g * act).astype(xb.dtype), v)

    return lax.map(one_block, hb).reshape(B, T, D)


def setup_inputs(seed: int = 0) -> dict:
    key = jax.random.key(seed)
    ks = jax.random.split(key, 20)
    f32 = jnp.float32
    nrm = lambda k, shape, scale: jax.random.normal(k, shape, f32) * scale
    return {
        'x_prompt': nrm(ks[0], (BATCH, SEQ, D_MODEL), 1.0),
        'x_sample': nrm(ks[1], (DEC_BATCH, DEC_SEQ, D_MODEL), 1.0),
        'cache_swa_k': nrm(ks[2], (DEPTH, DEC_BATCH, WINDOW, SWA_KV_HEADS, HEAD_DIM), 1.0),
        'cache_swa_v': nrm(ks[3], (DEPTH, DEC_BATCH, WINDOW, SWA_KV_HEADS, HEAD_DIM), 1.0),
        'state_gla': nrm(ks[4], (DEPTH, DEC_BATCH, GLA_HEADS, GLA_DK, GLA_DV), 0.5),
        'attn_norm': 1.0 + nrm(ks[5], (DEPTH, D_MODEL), 0.05),
        'w_in': nrm(ks[6], (DEPTH, D_MODEL, IN_COLS), D_MODEL ** -0.5),
        'q_norm': 1.0 + nrm(ks[7], (DEPTH, HEAD_DIM), 0.05),
        'k_norm': 1.0 + nrm(ks[8], (DEPTH, HEAD_DIM), 0.05),
        'attn_sinks': nrm(ks[9], (DEPTH, SWA_Q_HEADS), 0.5),
        'w_gate': nrm(ks[10], (DEPTH, GLA_LOWRANK, GLA_HEADS * GLA_DK), GLA_LOWRANK ** -0.5),
        'b_gate': nrm(ks[11], (DEPTH, GLA_HEADS * GLA_DK), 0.1),
        'gla_norm': 1.0 + nrm(ks[12], (DEPTH, GLA_DV), 0.05),
        'w_out': nrm(ks[13], (DEPTH, MIX_WIDTH, D_MODEL), MIX_WIDTH ** -0.5),
        'ffn_norm': 1.0 + nrm(ks[14], (DEPTH, D_MODEL), 0.05),
        'peer_wq': nrm(ks[15], (DEPTH, D_MODEL, PEER_HEADS * 2 * PEER_DKEY), D_MODEL ** -0.5),
        'peer_keys': nrm(ks[16], (DEPTH, PEER_HEADS, 2, PEER_NKEYS, PEER_DKEY), PEER_DKEY ** -0.5),
        'peer_u': nrm(ks[17], (DEPTH, PEER_N, D_MODEL), D_MODEL ** -0.5),
        'peer_v': nrm(ks[18], (DEPTH, PEER_N, D_MODEL), PEER_HEADS ** -0.5),
    }


def reference(x_prompt, x_sample, cache_swa_k, cache_swa_v, state_gla, attn_norm, w_in, q_norm, k_norm,
              attn_sinks, w_gate, b_gate, gla_norm, w_out, ffn_norm, peer_wq, peer_keys, peer_u, peer_v):
    xp, xs = x_prompt, x_sample
    pos_p = jnp.arange(xp.shape[1], dtype=jnp.int32)
    pos_s = PAST_LEN + jnp.arange(xs.shape[1], dtype=jnp.int32)
    kp_l, vp_l, sp_l, ks_l, vs_l, ss_l = [], [], [], [], [], []
    for l in range(DEPTH):
        q, k, v, gq, gk, gv, gg, lf = _mixer_inputs(xp, pos_p, attn_norm[l], w_in[l], q_norm[l], k_norm[l], w_gate[l], b_gate[l])
        a_out = _swa_prompt(q, k, v, attn_sinks[l])
        s0 = jnp.zeros((xp.shape[0], GLA_HEADS, GLA_DK, GLA_DV), xp.dtype)
        g_o, g_s = _gla_chunked(gq, gk, gv, lf, s0)
        xp = xp + jnp.concatenate([a_out, _gla_output(g_o, gg, gla_norm[l])], axis=-1) @ w_out[l]
        xp = xp + _peer(rms_norm(xp, ffn_norm[l]), peer_wq[l], peer_keys[l], peer_u[l], peer_v[l])
        kp_l.append(k[:, -WINDOW:])
        vp_l.append(v[:, -WINDOW:])
        sp_l.append(g_s)
        q, k, v, gq, gk, gv, gg, lf = _mixer_inputs(xs, pos_s, attn_norm[l], w_in[l], q_norm[l], k_norm[l], w_gate[l], b_gate[l])
        a_out, nk, nv = _swa_sample(q, k, v, cache_swa_k[l], cache_swa_v[l], attn_sinks[l])
        g_o, g_s = _gla_chunked(gq, gk, gv, lf, state_gla[l])
        xs = xs + jnp.concatenate([a_out, _gla_output(g_o, gg, gla_norm[l])], axis=-1) @ w_out[l]
        xs = xs + _peer(rms_norm(xs, ffn_norm[l]), peer_wq[l], peer_keys[l], peer_u[l], peer_v[l])
        ks_l.append(nk)
        vs_l.append(nv)
        ss_l.append(g_s)
    return (xp, xs, jnp.stack(kp_l), jnp.stack(vp_l), jnp.stack(sp_l), jnp.stack(ks_l), jnp.stack(vs_l), jnp.stack(ss_l))
```

```python
import functools

import jax
import jax.numpy as jnp
from jax import lax
from jax.experimental import pallas as pl
from jax.experimental.pallas import tpu as pltpu

F32 = jnp.float32
BF16 = jnp.bfloat16
U32 = jnp.uint32

D_MODEL = 1024
HEAD_DIM = 64
SWA_Q_HEADS = 8
WINDOW = 128
ROPE_THETA = 500000.0
ROPE_DIM = 16
PAST_LEN = 16384
GLA_HEADS = 4
GLA_DK = 64
GLA_DV = 128
GLA_LOWRANK = 16
GLA_GATE_NORM = 16.0
PEER_HEADS = 8
PEER_NKEYS = 128
PEER_TOPK = 16
NORM_EPS = 1e-6

LANES = 128
QW = SWA_Q_HEADS * HEAD_DIM
GK = GLA_HEADS * GLA_DK
GV = GLA_HEADS * GLA_DV
C_Q, C_K, C_V, C_GQ, C_GK, C_GV, C_GG, C_GA, C_END = 0, 512, 640, 768, 1024, 1280, 1792, 2304, 2432
GLA_PAD = 128
W_PITCH = 72
NEG_INF = float("-inf")


def _nn(a, b, precision=None):
    return jnp.dot(a, b, preferred_element_type=F32, precision=precision)


def _nt(a, b):
    return lax.dot_general(a, b, (((1,), (1,)), ((), ())), preferred_element_type=F32)


def _rms(x, gain):
    ms = jnp.mean(x * x, axis=-1, keepdims=True)
    return x * lax.rsqrt(ms + NORM_EPS) * gain


def _head_norm_rope(x, gain, ctab, stab):
    lane = lax.broadcasted_iota(jnp.int32, x.shape, 1)
    lo = lane < HEAD_DIM
    sq = x * x
    ms_lo = jnp.sum(jnp.where(lo, sq, 0.0), axis=-1, keepdims=True)
    ms_hi = jnp.sum(jnp.where(lo, 0.0, sq), axis=-1, keepdims=True)
    ms = jnp.where(lo, ms_lo, ms_hi) * (1.0 / HEAD_DIM)
    xn = x * lax.rsqrt(ms + NORM_EPS) * gain
    first = (lane & (HEAD_DIM - 1)) < (ROPE_DIM // 2)
    partner = jnp.where(first, pltpu.roll(xn, LANES - ROPE_DIM // 2, 1), pltpu.roll(xn, ROPE_DIM // 2, 1))
    return xn * ctab + partner * stab


def _project(x, ct, st, anorm, win, qkg, wgate, bgate):
    h = _rms(x, anorm).astype(BF16)
    p = _nn(h, win)
    qk = [
        _head_norm_rope(p[:, c * LANES:(c + 1) * LANES], qkg[:, c * LANES:(c + 1) * LANES], ct, st)
        for c in range(C_V // LANES)
    ]
    q = jnp.concatenate(qk[:4], axis=1) * (HEAD_DIM ** -0.5)
    k = qk[4]
    v = p[:, C_V:C_GQ]
    gq = p[:, C_GQ:C_GK] * (GLA_DK ** -0.5)
    gk = p[:, C_GK:C_GV]
    gv = p[:, C_GV:C_GG]
    gg = p[:, C_GG:C_GA]
    z = _nn(p[:, C_GA:C_END].astype(BF16), wgate) + bgate
    logf = (jnp.minimum(z, 0.0) - jnp.log(1.0 + jnp.exp(-jnp.abs(z)))) * (1.0 / GLA_GATE_NORM)
    return q, k, v, gq, gk, gv, gg, logf


def _swa_block(q, k2, v2, mask, sink_ref):
    lane = lax.broadcasted_iota(jnp.int32, k2.shape, 1)
    lo = lane < HEAD_DIM
    krot = pltpu.roll(k2, HEAD_DIM, 1)
    vrot = pltpu.roll(v2, HEAD_DIM, 1)
    zero = jnp.zeros_like(k2)
    nk = k2.shape[0]
    olane = lax.broadcasted_iota(jnp.int32, (q.shape[0], LANES), 1) < HEAD_DIM
    outs = []
    for g in range(2):
        ka, kb = (k2, krot) if g == 0 else (krot, k2)
        va, vb = (v2, vrot) if g == 0 else (vrot, v2)
        kexp = jnp.concatenate([jnp.where(lo, ka, zero), jnp.where(lo, zero, kb)], axis=0).astype(BF16)
        vexp = jnp.concatenate([jnp.where(lo, va, zero), jnp.where(lo, zero, vb)], axis=0).astype(BF16)
        for c in (2 * g, 2 * g + 1):
            s = _nt(q[:, c * LANES:(c + 1) * LANES].astype(BF16), kexp)
            ps, rs = [], []
            for hh in range(2):
                sh = jnp.where(mask, s[:, hh * nk:(hh + 1) * nk], NEG_INF)
                sink = sink_ref[2 * c + hh:2 * c + hh + 1, 0:1]
                m = jnp.maximum(jnp.max(sh, axis=-1, keepdims=True), sink)
                pe = jnp.exp(sh - m)
                den = jnp.sum(pe, axis=-1, keepdims=True) + jnp.exp(sink - m)
                ps.append(pe)
                rs.append(1.0 / den)
            o = _nn(jnp.concatenate(ps, axis=1).astype(BF16), vexp)
            outs.append(o * jnp.where(olane, rs[0], rs[1]))
    return jnp.concatenate(outs, axis=1)


def _zpad(a, rows):
    if a.shape[0] == rows:
        return a
    return jnp.concatenate([a, jnp.zeros((rows - a.shape[0], a.shape[1]), a.dtype)], axis=0)


def _gla_chunk(q, k, v, f, s, sub):
    c = q.shape[0]
    nsub = c // sub
    shift = sub.bit_length() - 1
    fp = _zpad(f, GLA_PAD)
    r = lax.broadcasted_iota(jnp.int32, (c, GLA_PAD), 0)
    cc = lax.broadcasted_iota(jnp.int32, (c, GLA_PAD), 1)
    bstart = (r >> shift) << shift
    causal = cc <= r
    hi = lax.Precision.HIGHEST
    b = _nn(causal.astype(F32), fp, hi)
    bl = _nn((causal & (cc >= bstart)).astype(F32), fp, hi)
    blast_row = b[c - 1:c]
    blast_col = jnp.sum(jnp.transpose(fp), axis=1, keepdims=True)

    lane_k = lax.broadcasted_iota(jnp.int32, (c, GK), 1) >> 6
    lane_v = lax.broadcasted_iota(jnp.int32, (c, GV), 1) >> 7
    rows = lax.broadcasted_iota(jnp.int32, (c, GK), 0)

    def expand_k(kt):
        return jnp.concatenate(
            [_zpad(jnp.where(lane_k == h, kt, 0.0), GLA_PAD) for h in range(GLA_HEADS)], axis=0).astype(BF16)

    a = _nt((q * jnp.exp(bl)).astype(BF16), expand_k(k * jnp.exp(-bl)))
    acol = lax.broadcasted_iota(jnp.int32, (c, GLA_HEADS * GLA_PAD), 1) & (GLA_PAD - 1)
    arow = lax.broadcasted_iota(jnp.int32, (c, GLA_HEADS * GLA_PAD), 0)
    a = jnp.where((acol <= arow) & (acol >= ((arow >> shift) << shift)), a, 0.0)
    if nsub > 1:
        qs, ks = [], []
        for j in range(nsub - 1):
            e = (j + 1) * sub
            bj = b[e - 1:e]
            qs.append(jnp.where(rows >= e, q * jnp.exp(jnp.minimum(b - bj, 0.0)), 0.0).astype(BF16))
            ks.append(expand_k(jnp.where((rows >= e - sub) & (rows < e), k * jnp.exp(jnp.minimum(bj - b, 0.0)), 0.0)))
        a = a + _nt(jnp.concatenate(qs, axis=1), jnp.concatenate(ks, axis=1))
    vexp = jnp.concatenate(
        [_zpad(jnp.where(lane_v == h, v, 0.0), GLA_PAD) for h in range(GLA_HEADS)], axis=0).astype(BF16)
    o = _nn(a.astype(BF16), vexp) + _nn((q * jnp.exp(b)).astype(BF16), s.astype(BF16))

    kbar_t = jnp.transpose(_zpad(k * jnp.exp(blast_row - b), GLA_PAD)).astype(BF16)
    upd = _nn(kbar_t, _zpad(v, GLA_PAD).astype(BF16))
    srow = lax.broadcasted_iota(jnp.int32, s.shape, 0) >> 6
    scol = lax.broadcasted_iota(jnp.int32, s.shape, 1) >> 7
    s_new = s * jnp.exp(blast_col) + jnp.where(srow == scol, upd, 0.0)
    return o, s_new


def _gla_out(o, gg, gnorm):
    outs = []
    for h in range(GLA_HEADS):
        oh = o[:, h * GLA_DV:(h + 1) * GLA_DV]
        ms = jnp.mean(oh * oh, axis=-1, keepdims=True)
        outs.append(oh * lax.rsqrt(ms + NORM_EPS))
    on = jnp.concatenate(outs, axis=1) * gnorm
    return on * (gg / (1.0 + jnp.exp(-gg)))


def _mixer_prompt_kernel(x_ref, ct_ref, st_ref, anorm_ref, win_ref, qkg_ref, sink_ref, wgate_ref, bgate_ref,
                         gnorm_ref, wout_ref, y_ref, klast_ref, vlast_ref, sfin_ref,
                         kprev_ref, vprev_ref, s_ref, *, tm, chunk, sub):
    i = pl.program_id(0)

    @pl.when(i == 0)
    def _():
        kprev_ref[...] = jnp.zeros_like(kprev_ref)
        vprev_ref[...] = jnp.zeros_like(vprev_ref)
        s_ref[...] = jnp.zeros_like(s_ref)

    x = x_ref[...]
    q, k, v, gq, gk, gv, gg, logf = _project(
        x, ct_ref[...], st_ref[...], anorm_ref[...], win_ref[...], qkg_ref[...], wgate_ref[...], bgate_ref[...])

    kall = jnp.concatenate([kprev_ref[...], k], axis=0)
    vall = jnp.concatenate([vprev_ref[...], v], axis=0)
    qi = lax.broadcasted_iota(jnp.int32, (WINDOW, 2 * WINDOW), 0)
    kj = lax.broadcasted_iota(jnp.int32, (WINDOW, 2 * WINDOW), 1)
    band = (kj >= qi) & (kj <= qi + WINDOW)
    blocks = []
    for j in range(tm // WINDOW):
        mask = band & (kj >= jnp.where(i == 0, WINDOW, 0)) if j == 0 else band
        blocks.append(_swa_block(q[j * WINDOW:(j + 1) * WINDOW], kall[j * WINDOW:(j + 2) * WINDOW],
                                 vall[j * WINDOW:(j + 2) * WINDOW], mask, sink_ref))
    a_out = jnp.concatenate(blocks, axis=0)
    kprev_ref[...] = k[tm - WINDOW:]
    vprev_ref[...] = v[tm - WINDOW:]
    klast_ref[...] = k[tm - WINDOW:]
    vlast_ref[...] = v[tm - WINDOW:]

    s = s_ref[...]
    os_ = []
    for c in range(tm // chunk):
        sl = slice(c * chunk, (c + 1) * chunk)
        o, s = _gla_chunk(gq[sl], gk[sl], gv[sl], logf[sl], s, sub)
        os_.append(o)
    s_ref[...] = s
    for h in range(GLA_HEADS):
        sfin_ref[h] = s[h * GLA_DK:(h + 1) * GLA_DK, h * GLA_DV:(h + 1) * GLA_DV]
    g_out = _gla_out(jnp.concatenate(os_, axis=0), gg, gnorm_ref[...])

    mix = jnp.concatenate([a_out, g_out], axis=1).astype(BF16)
    y_ref[...] = x + _nn(mix, wout_ref[...])


def _const_spec(shape):
    return pl.BlockSpec(shape, lambda *_: (0,) * len(shape))


def _mixer_prompt(x, ct, st, wts, *, tm=256, chunk=128, sub=32):
    t = x.shape[0]
    tm = min(tm, t)
    anorm, win, qkg, sinks, wgate, bgate, gnorm, wout = wts
    row = lambda w: pl.BlockSpec((tm, w), lambda i: (i, 0))
    kern = functools.partial(_mixer_prompt_kernel, tm=tm, chunk=chunk, sub=sub)
    return pl.pallas_call(
        kern,
        grid=(t // tm,),
        in_specs=[row(D_MODEL), row(LANES), row(LANES)] + [_const_spec(w.shape) for w in wts],
        out_specs=[row(D_MODEL), _const_spec((WINDOW, LANES)), _const_spec((WINDOW, LANES)),
                   _const_spec((GLA_HEADS, GLA_DK, GLA_DV))],
        out_shape=[jax.ShapeDtypeStruct((t, D_MODEL), F32), jax.ShapeDtypeStruct((WINDOW, LANES), F32),
                   jax.ShapeDtypeStruct((WINDOW, LANES), F32),
                   jax.ShapeDtypeStruct((GLA_HEADS, GLA_DK, GLA_DV), F32)],
        scratch_shapes=[pltpu.VMEM((WINDOW, LANES), F32), pltpu.VMEM((WINDOW, LANES), F32),
                        pltpu.VMEM((GK, GV), F32)],
        compiler_params=pltpu.CompilerParams(dimension_semantics=("arbitrary",),
                                             vmem_limit_bytes=48 * 1024 * 1024),
        name="mixer_prompt",
    )(x, ct, st, *wts)


def _mixer_sample_kernel(x_ref, ct_ref, st_ref, anorm_ref, win_ref, qkg_ref, sink_ref, wgate_ref, bgate_ref,
                         gnorm_ref, wout_ref, ck_ref, cv_ref, sin_ref,
                         y_ref, knew_ref, vnew_ref, sout_ref,
                         q_s, k_s, v_s, gq_s, gk_s, gv_s, f_s, a_s, o_s, *, sb, dec):
    x = x_ref[...]
    q, k, v, gq, gk, gv, gg, logf = _project(
        x, ct_ref[...], st_ref[...], anorm_ref[...], win_ref[...], qkg_ref[...], wgate_ref[...], bgate_ref[...])
    knew_ref[...] = k
    vnew_ref[...] = v
    q_s[...] = q
    k_s[...] = k
    v_s[...] = v
    gq_s[...] = gq
    gk_s[...] = gk
    gv_s[...] = gv
    f_s[...] = logf

    per = 8 // dec
    qi = lax.broadcasted_iota(jnp.int32, (8, 2 * WINDOW), 0) & (dec - 1)
    kj = lax.broadcasted_iota(jnp.int32, (8, 2 * WINDOW), 1)
    mask = (kj >= qi) & (kj <= qi + WINDOW)
    ztail = jnp.zeros((WINDOW - dec, LANES), F32)
    zrow = jnp.zeros((8 - dec, GK), F32)
    zrow_v = jnp.zeros((8 - dec, GV), F32)

    def group(gi, carry):
        r0 = pl.multiple_of(gi * 8, 8)
        rows = pl.ds(r0, 8)
        q8, k8, v8 = q_s[rows, :], k_s[rows, :], v_s[rows, :]
        gq8, gk8, gv8, f8 = gq_s[rows, :], gk_s[rows, :], gv_s[rows, :], f_s[rows, :]
        a_parts, o_parts = [], []
        for u in range(per):
            b = gi * per + u
            sl = slice(u * dec, (u + 1) * dec)
            k2 = jnp.concatenate([ck_ref[b], k8[sl], ztail], axis=0)
            v2 = jnp.concatenate([cv_ref[b], v8[sl], ztail], axis=0)
            a_parts.append(_swa_block(q8, k2, v2, mask, sink_ref)[sl])
            st = sin_ref[b]
            srows = []
            for h in range(GLA_HEADS):
                parts = []
                if h > 0:
                    parts.append(jnp.zeros((GLA_DK, h * GLA_DV), F32))
                parts.append(st[h])
                if h < GLA_HEADS - 1:
                    parts.append(jnp.zeros((GLA_DK, (GLA_HEADS - 1 - h) * GLA_DV), F32))
                srows.append(jnp.concatenate(parts, axis=1))
            sexp = jnp.concatenate(srows, axis=0)
            o, s_new = _gla_chunk(jnp.concatenate([gq8[sl], zrow], axis=0), jnp.concatenate([gk8[sl], zrow], axis=0),
                                  jnp.concatenate([gv8[sl], zrow_v], axis=0), jnp.concatenate([f8[sl], zrow], axis=0),
                                  sexp, 8)
            o_parts.append(o[:dec])
            for h in range(GLA_HEADS):
                sout_ref[b, h] = s_new[h * GLA_DK:(h + 1) * GLA_DK, h * GLA_DV:(h + 1) * GLA_DV]
        a_s[rows, :] = jnp.concatenate(a_parts, axis=0)
        o_s[rows, :] = jnp.concatenate(o_parts, axis=0)
        return carry

    lax.fori_loop(0, sb // per, group, 0)
    g_out = _gla_out(o_s[...], gg, gnorm_ref[...])
    mix = jnp.concatenate([a_s[...], g_out], axis=1).astype(BF16)
    y_ref[...] = x + _nn(mix, wout_ref[...])


def _mixer_sample(x, ct, st, wts, ck, cv, s0, *, dec, sb=16):
    n = x.shape[0]
    nseq = n // dec
    m = sb * dec
    row = lambda w: pl.BlockSpec((m, w), lambda i: (i, 0))
    seq3 = pl.BlockSpec((sb, WINDOW, LANES), lambda i: (i, 0, 0))
    seq4 = pl.BlockSpec((sb, GLA_HEADS, GLA_DK, GLA_DV), lambda i: (i, 0, 0, 0))
    kern = functools.partial(_mixer_sample_kernel, sb=sb, dec=dec)
    return pl.pallas_call(
        kern,
        grid=(nseq // sb,),
        in_specs=[row(D_MODEL), row(LANES), row(LANES)] + [_const_spec(w.shape) for w in wts] + [seq3, seq3, seq4],
        out_specs=[row(D_MODEL), row(LANES), row(LANES), seq4],
        out_shape=[jax.ShapeDtypeStruct((n, D_MODEL), F32), jax.ShapeDtypeStruct((n, LANES), F32),
                   jax.ShapeDtypeStruct((n, LANES), F32), jax.ShapeDtypeStruct(s0.shape, F32)],
        scratch_shapes=[pltpu.VMEM((m, QW), F32), pltpu.VMEM((m, LANES), F32), pltpu.VMEM((m, LANES), F32),
                        pltpu.VMEM((m, GK), F32), pltpu.VMEM((m, GK), F32), pltpu.VMEM((m, GV), F32),
                        pltpu.VMEM((m, GK), F32), pltpu.VMEM((m, QW), F32), pltpu.VMEM((m, GV), F32)],
        compiler_params=pltpu.CompilerParams(dimension_semantics=("arbitrary",),
                                             vmem_limit_bytes=48 * 1024 * 1024),
        name="mixer_sample",
    )(x, ct, st, *wts, ck, cv, s0)


def _shift_kernel(ck_ref, cv_ref, nk_ref, nv_ref, ok_ref, ov_ref, *, new):
    keep = ck_ref.shape[1] - new
    ok_ref[:, :keep] = ck_ref[:, new:]
    ok_ref[:, keep:] = nk_ref[...]
    ov_ref[:, :keep] = cv_ref[:, new:]
    ov_ref[:, keep:] = nv_ref[...]


def _shift_cache(ck, cv, nk, nv, *, sb=32):
    nseq, width = ck.shape
    new = nk.shape[1]
    big = pl.BlockSpec((sb, width), lambda i: (i, 0))
    small = pl.BlockSpec((sb, new), lambda i: (i, 0))
    return pl.pallas_call(
        functools.partial(_shift_kernel, new=new),
        grid=(nseq // sb,),
        in_specs=[big, big, small, small],
        out_specs=[big, big],
        out_shape=[jax.ShapeDtypeStruct(ck.shape, F32), jax.ShapeDtypeStruct(cv.shape, F32)],
        compiler_params=pltpu.CompilerParams(dimension_semantics=("arbitrary",)),
        name="shift_cache",
    )(ck, cv, nk, nv)


def _top16(s):
    iota = lax.broadcasted_iota(jnp.int32, s.shape, 0).astype(F32)
    vals, idxs = [], []
    for _ in range(PEER_TOPK):
        m = jnp.max(s, axis=0, keepdims=True)
        idx = jnp.min(jnp.where(s == m, iota, float(PEER_NKEYS)), axis=0, keepdims=True)
        s = jnp.where(iota == idx, NEG_INF, s)
        vals.append(m)
        idxs.append(idx)
    return jnp.concatenate(vals, axis=0), jnp.concatenate(idxs, axis=0)


def _pair_top16(a, ia, b, ib):
    t = a.shape[1]
    sub8 = lax.broadcasted_iota(jnp.int32, (8, t), 0).astype(F32)
    sub16 = lax.broadcasted_iota(jnp.int32, (16, t), 0).astype(F32)
    cands = [a[0:1] + b]
    codes = [sub16]
    exps = [ia[0:1] * float(PEER_NKEYS) + ib]
    for x in range(1, 8):
        ny = PEER_TOPK // (x + 1)
        cands.append(jnp.where(sub8 < float(ny), a[x:x + 1] + b[0:8], NEG_INF))
        codes.append(sub8 + float(16 * x))
        exps.append(ia[x:x + 1] * float(PEER_NKEYS) + ib[0:8])
    cands.append(a[8:16] + b[0:1])
    codes.append((sub8 + 8.0) * 16.0)
    exps.append(ia[8:16] * float(PEER_NKEYS) + ib[0:1])
    cand = jnp.concatenate(cands, axis=0)
    code = jnp.concatenate(codes, axis=0)
    eid = jnp.concatenate(exps, axis=0)
    vals, sel = [], []
    for _ in range(PEER_TOPK):
        m = jnp.max(cand, axis=0, keepdims=True)
        cmin = jnp.min(jnp.where(cand == m, code, 1e9), axis=0, keepdims=True)
        hit = code == cmin
        sel.append(jnp.max(jnp.where(hit, eid, -1.0), axis=0, keepdims=True))
        cand = jnp.where(hit, NEG_INF, cand)
        vals.append(m)
    return jnp.concatenate(vals, axis=0), jnp.concatenate(sel, axis=0)


def _peer_route_kernel(x_ref, fnorm_ref, wqt_ref, keys_ref, h_ref, i_ref, j_ref, g_ref, qt_s, e_s, w_s):
    hb = _rms(x_ref[...], fnorm_ref[...]).astype(BF16)
    h_ref[...] = hb
    qt_s[...] = _nt(wqt_ref[...], hb)

    def head(hh, carry):
        r0 = pl.multiple_of(hh * 2 * PEER_NKEYS, 2 * PEER_NKEYS)
        s0 = _nn(keys_ref[2 * hh], qt_s[pl.ds(r0, PEER_NKEYS), :].astype(BF16))
        s1 = _nn(keys_ref[2 * hh + 1], qt_s[pl.ds(r0 + PEER_NKEYS, PEER_NKEYS), :].astype(BF16))
        a, ia = _top16(s0)
        b, ib = _top16(s1)
        c, e = _pair_top16(a, ia, b, ib)
        p = jnp.exp(c - c[0:1])
        rows = pl.ds(pl.multiple_of(hh * PEER_TOPK, PEER_TOPK), PEER_TOPK)
        e_s[rows, :] = e
        w_s[rows, :] = p / jnp.sum(p, axis=0, keepdims=True)
        return carry

    lax.fori_loop(0, PEER_HEADS, head, 0)
    et = jnp.transpose(e_s[...])
    it = jnp.floor(et * (1.0 / PEER_NKEYS))
    i_ref[...] = it
    j_ref[...] = et - it * float(PEER_NKEYS)
    g_ref[...] = jnp.transpose(w_s[...])


def _peer_route(x, fnorm, wqt, keys, *, tm=256):
    t = x.shape[0]
    tm = min(tm, t)
    nsel = PEER_HEADS * PEER_TOPK
    row = lambda w: pl.BlockSpec((tm, w), lambda i: (i, 0))
    return pl.pallas_call(
        _peer_route_kernel,
        grid=(t // tm,),
        in_specs=[row(D_MODEL), _const_spec(fnorm.shape), _const_spec(wqt.shape), _const_spec(keys.shape)],
        out_specs=[row(D_MODEL), row(nsel), row(nsel), row(nsel)],
        out_shape=[jax.ShapeDtypeStruct((t, D_MODEL), BF16)] + [jax.ShapeDtypeStruct((t, nsel), F32)] * 3,
        scratch_shapes=[pltpu.VMEM((wqt.shape[0], tm), F32), pltpu.VMEM((nsel, tm), F32),
                        pltpu.VMEM((nsel, tm), F32)],
        compiler_params=pltpu.CompilerParams(dimension_semantics=("arbitrary",),
                                             vmem_limit_bytes=48 * 1024 * 1024),
        name="peer_route",
    )(x, fnorm, wqt, keys)


def _peer_dense_kernel(h_ref, x_ref, i_ref, j_ref, g_ref, u_ref, v_ref, y_ref, w_s, acc_s, *, tm, te):
    e = pl.program_id(1)
    ne = pl.num_programs(1)
    half_rows = PEER_NKEYS // 2

    @pl.when(e == 0)
    def _():
        acc_s[...] = jnp.zeros_like(acc_s)
        sub = lax.broadcasted_iota(jnp.int32, (PEER_NKEYS, LANES), 0).astype(F32)

        def token(n, carry):
            ri = i_ref[pl.ds(n, 1), :]
            rj = j_ref[pl.ds(n, 1), :]
            rg = g_ref[pl.ds(n, 1), :]
            at = jnp.where(sub == ri, 1.0, 0.0).astype(BF16)
            bt = jnp.where(sub == rj, rg, 0.0).astype(BF16)
            w = _nt(at, bt)
            lo = lax.bitcast_convert_type(w[:half_rows], U32)
            hi = lax.bitcast_convert_type(w[half_rows:], U32)
            packed = ((lo + U32(0x8000)) & U32(0xFFFF0000)) | ((hi + U32(0x8000)) >> U32(16))
            w_s[pl.ds(pl.multiple_of(n * W_PITCH, 8), half_rows), :] = packed
            return carry

        lax.fori_loop(0, tm, token, 0, unroll=2)

    act = _nt(h_ref[...], u_ref[...])
    blocks_per_half = (half_rows * PEER_NKEYS) // te
    shift = jnp.where(e >= blocks_per_half, 16, 0).astype(U32)
    r_base = (e % blocks_per_half) * (te // PEER_NKEYS)
    zs = []
    for kk in range(te // PEER_NKEYS):
        wp = w_s[pl.ds(r_base + kk, tm, stride=W_PITCH), :]
        wf = lax.bitcast_convert_type((wp << shift) & U32(0xFFFF0000), F32)
        a = act[:, kk * PEER_NKEYS:(kk + 1) * PEER_NKEYS]
        gelu = 0.5 * a * (1.0 + lax.erf(a * (2.0 ** -0.5)))
        zs.append((wf * gelu).astype(BF16))
    acc_s[...] += _nn(jnp.concatenate(zs, axis=1), v_ref[...])

    @pl.when(e == ne - 1)
    def _():
        y_ref[...] = x_ref[...] + acc_s[...]


def _peer_dense(h, x, ii, jj, gg, u, v, *, tm=512, te=1024):
    t = h.shape[0]
    tm = min(tm, t)
    n_exp = u.shape[0]
    tok = lambda w: pl.BlockSpec((tm, w), lambda i, e: (i, 0))
    exp = pl.BlockSpec((te, D_MODEL), lambda i, e: (e, 0))
    kern = functools.partial(_peer_dense_kernel, tm=tm, te=te)
    return pl.pallas_call(
        kern,
        grid=(t // tm, n_exp // te),
        in_specs=[tok(D_MODEL), tok(D_MODEL), tok(LANES), tok(LANES), tok(LANES), exp, exp],
        out_specs=tok(D_MODEL),
        out_shape=jax.ShapeDtypeStruct((t, D_MODEL), F32),
        scratch_shapes=[pltpu.VMEM((tm * W_PITCH, LANES), U32), pltpu.VMEM((tm, D_MODEL), F32)],
        compiler_params=pltpu.CompilerParams(dimension_semantics=("arbitrary", "arbitrary"),
                                             vmem_limit_bytes=56 * 1024 * 1024),
        name="peer_dense",
    )(h, x, ii, jj, gg, u, v)


def _cast_kernel(x_ref, o_ref):
    o_ref[...] = x_ref[...].astype(o_ref.dtype)


def _to_bf16(x, *, rows=1024):
    spec = pl.BlockSpec((rows, x.shape[1]), lambda i: (i, 0))
    return pl.pallas_call(
        _cast_kernel, grid=(x.shape[0] // rows,), in_specs=[spec], out_specs=spec,
        out_shape=jax.ShapeDtypeStruct(x.shape, BF16),
        compiler_params=pltpu.CompilerParams(dimension_semantics=("arbitrary",)),
        name="cast_bf16",
    )(x)


def _rope_tables(pos):
    half = ROPE_DIM // 2
    inv_freq = ROPE_THETA ** (-jnp.arange(half, dtype=F32) / half)
    ang = pos.astype(F32)[:, None] * inv_freq[None, :]
    cos, sin = jnp.cos(ang), jnp.sin(ang)
    n = pos.shape[0]
    c64 = jnp.concatenate([cos, cos, jnp.ones((n, HEAD_DIM - ROPE_DIM), F32)], axis=1)
    s64 = jnp.concatenate([-sin, sin, jnp.zeros((n, HEAD_DIM - ROPE_DIM), F32)], axis=1)
    return jnp.tile(c64, (1, 2)), jnp.tile(s64, (1, 2))


def kernel(x_prompt, x_sample, cache_swa_k, cache_swa_v, state_gla, attn_norm, w_in, q_norm, k_norm, attn_sinks,
           w_gate, b_gate, gla_norm, w_out, ffn_norm, peer_wq, peer_keys, peer_u, peer_v):
    depth = w_in.shape[0]
    assert depth == 1 and x_prompt.shape[0] == 1
    seq = x_prompt.shape[1]
    nseq, dec = x_sample.shape[0], x_sample.shape[1]
    xp = x_prompt[0]
    xs = x_sample.reshape(nseq * dec, D_MODEL)
    l = 0

    win = jnp.pad(w_in[l], ((0, 0), (0, C_END - w_in.shape[2]))).astype(BF16)
    qkg = jnp.concatenate([jnp.tile(q_norm[l], SWA_Q_HEADS), jnp.tile(k_norm[l], 2)])[None, :]
    sinks = jnp.broadcast_to(attn_sinks[l][:, None], (SWA_Q_HEADS, LANES))
    wgate = jnp.pad(w_gate[l], ((0, LANES - GLA_LOWRANK), (0, 0))).astype(BF16)
    wts = (attn_norm[l][None, :], win, qkg, sinks, wgate, b_gate[l][None, :],
           jnp.tile(gla_norm[l], GLA_HEADS)[None, :], w_out[l].astype(BF16))
    fnorm = ffn_norm[l][None, :]
    wqt = jnp.transpose(peer_wq[l]).astype(BF16)
    keys = peer_keys[l].reshape(PEER_HEADS * 2, PEER_NKEYS, peer_keys.shape[-1]).astype(BF16)
    u_b = _to_bf16(peer_u[l])
    v_b = _to_bf16(peer_v[l])

    ct_p, st_p = _rope_tables(jnp.arange(seq, dtype=jnp.int32))
    ct_s, st_s = _rope_tables(PAST_LEN + jnp.arange(nseq * dec, dtype=jnp.int32) % dec)

    xp2, kp, vp, sp = _mixer_prompt(xp, ct_p, st_p, wts)
    hp, ip, jp, gp = _peer_route(xp2, fnorm, wqt, keys)
    yp = _peer_dense(hp, xp2, ip, jp, gp, u_b, v_b)

    ck = cache_swa_k[l].reshape(nseq, WINDOW, LANES)
    cv = cache_swa_v[l].reshape(nseq, WINDOW, LANES)
    xs2, nk, nv, ss = _mixer_sample(xs, ct_s, st_s, wts, ck, cv, state_gla[l], dec=dec)
    ck2, cv2 = _shift_cache(ck.reshape(nseq, WINDOW * LANES), cv.reshape(nseq, WINDOW * LANES),
                            nk.reshape(nseq, dec * LANES), nv.reshape(nseq, dec * LANES))
    hs, is_, js, gs = _peer_route(xs2, fnorm, wqt, keys)
    ys = _peer_dense(hs, xs2, is_, js, gs, u_b, v_b)

    kv_shape = (1, 1, WINDOW, 2, HEAD_DIM)
    return (yp[None], ys.reshape(nseq, dec, D_MODEL),
            kp.reshape(kv_shape), vp.reshape(kv_shape), sp[None, None],
            ck2.reshape(1, nseq, WINDOW, 2, HEAD_DIM), cv2.reshape(1, nseq, WINDOW, 2, HEAD_DIM), ss[None])
```

```python
import functools

import jax
import jax.numpy as jnp
from jax import lax
from jax.experimental import pallas as pl
from jax.experimental.pallas import tpu as pltpu

F32 = jnp.float32
BF16 = jnp.bfloat16
U32 = jnp.uint32

D_MODEL = 1024
HEAD_DIM = 64
SWA_Q_HEADS = 8
WINDOW = 128
ROPE_THETA = 500000.0
ROPE_DIM = 16
PAST_LEN = 16384
GLA_HEADS = 4
GLA_DK = 64
GLA_DV = 128
GLA_LOWRANK = 16
GLA_GATE_NORM = 16.0
PEER_HEADS = 8
PEER_NKEYS = 128
PEER_TOPK = 16
NORM_EPS = 1e-6

LANES = 128
QW = SWA_Q_HEADS * HEAD_DIM
GK = GLA_HEADS * GLA_DK
GV = GLA_HEADS * GLA_DV
C_Q, C_K, C_V, C_GQ, C_GK, C_GV, C_GG, C_GA, C_END = 0, 512, 640, 768, 1024, 1280, 1792, 2304, 2432
GLA_PAD = 128
W_PITCH = 136
NEG_INF = float("-inf")


def _nn(a, b, precision=None):
    return jnp.dot(a, b, preferred_element_type=F32, precision=precision)


def _nt(a, b):
    return lax.dot_general(a, b, (((1,), (1,)), ((), ())), preferred_element_type=F32)


def _rms(x, gain):
    ms = jnp.mean(x * x, axis=-1, keepdims=True)
    return x * lax.rsqrt(ms + NORM_EPS) * gain


def _head_norm_rope(x, gain, ctab, stab):
    lane = lax.broadcasted_iota(jnp.int32, x.shape, 1)
    lo = lane < HEAD_DIM
    sq = x * x
    ms_lo = jnp.sum(jnp.where(lo, sq, 0.0), axis=-1, keepdims=True)
    ms_hi = jnp.sum(jnp.where(lo, 0.0, sq), axis=-1, keepdims=True)
    ms = jnp.where(lo, ms_lo, ms_hi) * (1.0 / HEAD_DIM)
    xn = x * lax.rsqrt(ms + NORM_EPS) * gain
    first = (lane & (HEAD_DIM - 1)) < (ROPE_DIM // 2)
    partner = jnp.where(first, pltpu.roll(xn, LANES - ROPE_DIM // 2, 1), pltpu.roll(xn, ROPE_DIM // 2, 1))
    return xn * ctab + partner * stab


def _project(x, ct, st, anorm, win, qkg, wgate, bgate):
    h = _rms(x, anorm).astype(BF16)
    p = _nn(h, win)
    qk = [
        _head_norm_rope(p[:, c * LANES:(c + 1) * LANES], qkg[:, c * LANES:(c + 1) * LANES], ct, st)
        for c in range(C_V // LANES)
    ]
    q = jnp.concatenate(qk[:4], axis=1) * (HEAD_DIM ** -0.5)
    k = qk[4]
    v = p[:, C_V:C_GQ]
    gq = p[:, C_GQ:C_GK] * (GLA_DK ** -0.5)
    gk = p[:, C_GK:C_GV]
    gv = p[:, C_GV:C_GG]
    gg = p[:, C_GG:C_GA]
    z = _nn(p[:, C_GA:C_END].astype(BF16), wgate) + bgate
    logf = (jnp.minimum(z, 0.0) - jnp.log(1.0 + jnp.exp(-jnp.abs(z)))) * (1.0 / GLA_GATE_NORM)
    return q, k, v, gq, gk, gv, gg, logf


def _swa_block(q, k2, v2, mask, sink_ref):
    lane = lax.broadcasted_iota(jnp.int32, k2.shape, 1)
    lo = lane < HEAD_DIM
    krot = pltpu.roll(k2, HEAD_DIM, 1)
    vrot = pltpu.roll(v2, HEAD_DIM, 1)
    zero = jnp.zeros_like(k2)
    nk = k2.shape[0]
    olane = lax.broadcasted_iota(jnp.int32, (q.shape[0], LANES), 1) < HEAD_DIM
    outs = []
    for g in range(2):
        ka, kb = (k2, krot) if g == 0 else (krot, k2)
        va, vb = (v2, vrot) if g == 0 else (vrot, v2)
        kexp = jnp.concatenate([jnp.where(lo, ka, zero), jnp.where(lo, zero, kb)], axis=0).astype(BF16)
        vexp = jnp.concatenate([jnp.where(lo, va, zero), jnp.where(lo, zero, vb)], axis=0).astype(BF16)
        for c in (2 * g, 2 * g + 1):
            s = _nt(q[:, c * LANES:(c + 1) * LANES].astype(BF16), kexp)
            ps, rs = [], []
            for hh in range(2):
                sh = jnp.where(mask, s[:, hh * nk:(hh + 1) * nk], NEG_INF)
                sink = sink_ref[2 * c + hh:2 * c + hh + 1, 0:1]
                m = jnp.maximum(jnp.max(sh, axis=-1, keepdims=True), sink)
                pe = jnp.exp(sh - m)
                den = jnp.sum(pe, axis=-1, keepdims=True) + jnp.exp(sink - m)
                ps.append(pe)
                rs.append(1.0 / den)
            o = _nn(jnp.concatenate(ps, axis=1).astype(BF16), vexp)
            outs.append(o * jnp.where(olane, rs[0], rs[1]))
    return jnp.concatenate(outs, axis=1)


def _zpad(a, rows):
    if a.shape[0] == rows:
        return a
    return jnp.concatenate([a, jnp.zeros((rows - a.shape[0], a.shape[1]), a.dtype)], axis=0)


def _gla_chunk(q, k, v, f, s, sub):
    c = q.shape[0]
    nsub = c // sub
    shift = sub.bit_length() - 1
    fp = _zpad(f, GLA_PAD)
    r = lax.broadcasted_iota(jnp.int32, (c, GLA_PAD), 0)
    cc = lax.broadcasted_iota(jnp.int32, (c, GLA_PAD), 1)
    bstart = (r >> shift) << shift
    causal = cc <= r
    hi = lax.Precision.HIGHEST
    b = _nn(causal.astype(F32), fp, hi)
    bl = _nn((causal & (cc >= bstart)).astype(F32), fp, hi)
    blast_row = b[c - 1:c]
    blast_col = jnp.sum(jnp.transpose(fp), axis=1, keepdims=True)

    lane_k = lax.broadcasted_iota(jnp.int32, (c, GK), 1) >> 6
    lane_v = lax.broadcasted_iota(jnp.int32, (c, GV), 1) >> 7
    rows = lax.broadcasted_iota(jnp.int32, (c, GK), 0)

    def expand_k(kt):
        return jnp.concatenate(
            [_zpad(jnp.where(lane_k == h, kt, 0.0), GLA_PAD) for h in range(GLA_HEADS)], axis=0).astype(BF16)

    a = _nt((q * jnp.exp(bl)).astype(BF16), expand_k(k * jnp.exp(-bl)))
    acol = lax.broadcasted_iota(jnp.int32, (c, GLA_HEADS * GLA_PAD), 1) & (GLA_PAD - 1)
    arow = lax.broadcasted_iota(jnp.int32, (c, GLA_HEADS * GLA_PAD), 0)
    a = jnp.where((acol <= arow) & (acol >= ((arow >> shift) << shift)), a, 0.0)
    if nsub > 1:
        qs, ks = [], []
        for j in range(nsub - 1):
            e = (j + 1) * sub
            bj = b[e - 1:e]
            qs.append(jnp.where(rows >= e, q * jnp.exp(jnp.minimum(b - bj, 0.0)), 0.0).astype(BF16))
            ks.append(expand_k(jnp.where((rows >= e - sub) & (rows < e), k * jnp.exp(jnp.minimum(bj - b, 0.0)), 0.0)))
        a = a + _nt(jnp.concatenate(qs, axis=1), jnp.concatenate(ks, axis=1))
    vexp = jnp.concatenate(
        [_zpad(jnp.where(lane_v == h, v, 0.0), GLA_PAD) for h in range(GLA_HEADS)], axis=0).astype(BF16)
    o = _nn(a.astype(BF16), vexp) + _nn((q * jnp.exp(b)).astype(BF16), s.astype(BF16))

    kbar_t = jnp.transpose(_zpad(k * jnp.exp(blast_row - b), GLA_PAD)).astype(BF16)
    upd = _nn(kbar_t, _zpad(v, GLA_PAD).astype(BF16))
    srow = lax.broadcasted_iota(jnp.int32, s.shape, 0) >> 6
    scol = lax.broadcasted_iota(jnp.int32, s.shape, 1) >> 7
    s_new = s * jnp.exp(blast_col) + jnp.where(srow == scol, upd, 0.0)
    return o, s_new


def _gla_out(o, gg, gnorm):
    outs = []
    for h in range(GLA_HEADS):
        oh = o[:, h * GLA_DV:(h + 1) * GLA_DV]
        ms = jnp.mean(oh * oh, axis=-1, keepdims=True)
        outs.append(oh * lax.rsqrt(ms + NORM_EPS))
    on = jnp.concatenate(outs, axis=1) * gnorm
    return on * (gg / (1.0 + jnp.exp(-gg)))


def _mixer_prompt_kernel(x_ref, ct_ref, st_ref, anorm_ref, win_ref, qkg_ref, sink_ref, wgate_ref, bgate_ref,
                         gnorm_ref, wout_ref, y_ref, klast_ref, vlast_ref, sfin_ref,
                         kprev_ref, vprev_ref, s_ref, *, tm, chunk, sub):
    i = pl.program_id(0)

    @pl.when(i == 0)
    def _():
        kprev_ref[...] = jnp.zeros_like(kprev_ref)
        vprev_ref[...] = jnp.zeros_like(vprev_ref)
        s_ref[...] = jnp.zeros_like(s_ref)

    x = x_ref[...]
    q, k, v, gq, gk, gv, gg, logf = _project(
        x, ct_ref[...], st_ref[...], anorm_ref[...], win_ref[...], qkg_ref[...], wgate_ref[...], bgate_ref[...])

    kall = jnp.concatenate([kprev_ref[...], k], axis=0)
    vall = jnp.concatenate([vprev_ref[...], v], axis=0)
    qi = lax.broadcasted_iota(jnp.int32, (WINDOW, 2 * WINDOW), 0)
    kj = lax.broadcasted_iota(jnp.int32, (WINDOW, 2 * WINDOW), 1)
    band = (kj >= qi) & (kj <= qi + WINDOW)
    blocks = []
    for j in range(tm // WINDOW):
        mask = band & (kj >= jnp.where(i == 0, WINDOW, 0)) if j == 0 else band
        blocks.append(_swa_block(q[j * WINDOW:(j + 1) * WINDOW], kall[j * WINDOW:(j + 2) * WINDOW],
                                 vall[j * WINDOW:(j + 2) * WINDOW], mask, sink_ref))
    a_out = jnp.concatenate(blocks, axis=0)
    kprev_ref[...] = k[tm - WINDOW:]
    vprev_ref[...] = v[tm - WINDOW:]
    klast_ref[...] = k[tm - WINDOW:]
    vlast_ref[...] = v[tm - WINDOW:]

    s = s_ref[...]
    os_ = []
    for c in range(tm // chunk):
        sl = slice(c * chunk, (c + 1) * chunk)
        o, s = _gla_chunk(gq[sl], gk[sl], gv[sl], logf[sl], s, sub)
        os_.append(o)
    s_ref[...] = s
    for h in range(GLA_HEADS):
        sfin_ref[h] = s[h * GLA_DK:(h + 1) * GLA_DK, h * GLA_DV:(h + 1) * GLA_DV]
    g_out = _gla_out(jnp.concatenate(os_, axis=0), gg, gnorm_ref[...])

    mix = jnp.concatenate([a_out, g_out], axis=1).astype(BF16)
    y_ref[...] = x + _nn(mix, wout_ref[...])


def _const_spec(shape):
    return pl.BlockSpec(shape, lambda *_: (0,) * len(shape))


def _mixer_prompt(x, ct, st, wts, *, tm=256, chunk=128, sub=32):
    t = x.shape[0]
    tm = min(tm, t)
    anorm, win, qkg, sinks, wgate, bgate, gnorm, wout = wts
    row = lambda w: pl.BlockSpec((tm, w), lambda i: (i, 0))
    kern = functools.partial(_mixer_prompt_kernel, tm=tm, chunk=chunk, sub=sub)
    return pl.pallas_call(
        kern,
        grid=(t // tm,),
        in_specs=[row(D_MODEL), row(LANES), row(LANES)] + [_const_spec(w.shape) for w in wts],
        out_specs=[row(D_MODEL), _const_spec((WINDOW, LANES)), _const_spec((WINDOW, LANES)),
                   _const_spec((GLA_HEADS, GLA_DK, GLA_DV))],
        out_shape=[jax.ShapeDtypeStruct((t, D_MODEL), F32), jax.ShapeDtypeStruct((WINDOW, LANES), F32),
                   jax.ShapeDtypeStruct((WINDOW, LANES), F32),
                   jax.ShapeDtypeStruct((GLA_HEADS, GLA_DK, GLA_DV), F32)],
        scratch_shapes=[pltpu.VMEM((WINDOW, LANES), F32), pltpu.VMEM((WINDOW, LANES), F32),
                        pltpu.VMEM((GK, GV), F32)],
        compiler_params=pltpu.CompilerParams(dimension_semantics=("arbitrary",),
                                             vmem_limit_bytes=48 * 1024 * 1024),
        name="mixer_prompt",
    )(x, ct, st, *wts)


def _mixer_sample_kernel(x_ref, ct_ref, st_ref, anorm_ref, win_ref, qkg_ref, sink_ref, wgate_ref, bgate_ref,
                         gnorm_ref, wout_ref, ck_ref, cv_ref, sin_ref,
                         y_ref, knew_ref, vnew_ref, sout_ref,
                         q_s, k_s, v_s, gq_s, gk_s, gv_s, f_s, a_s, o_s, *, sb, dec):
    x = x_ref[...]
    q, k, v, gq, gk, gv, gg, logf = _project(
        x, ct_ref[...], st_ref[...], anorm_ref[...], win_ref[...], qkg_ref[...], wgate_ref[...], bgate_ref[...])
    knew_ref[...] = k
    vnew_ref[...] = v
    q_s[...] = q
    k_s[...] = k
    v_s[...] = v
    gq_s[...] = gq
    gk_s[...] = gk
    gv_s[...] = gv
    f_s[...] = logf

    per = 8 // dec
    qi = lax.broadcasted_iota(jnp.int32, (8, 2 * WINDOW), 0) & (dec - 1)
    kj = lax.broadcasted_iota(jnp.int32, (8, 2 * WINDOW), 1)
    mask = (kj >= qi) & (kj <= qi + WINDOW)
    ztail = jnp.zeros((WINDOW - dec, LANES), F32)
    zrow = jnp.zeros((8 - dec, GK), F32)
    zrow_v = jnp.zeros((8 - dec, GV), F32)

    def group(gi, carry):
        r0 = pl.multiple_of(gi * 8, 8)
        rows = pl.ds(r0, 8)
        q8, k8, v8 = q_s[rows, :], k_s[rows, :], v_s[rows, :]
        gq8, gk8, gv8, f8 = gq_s[rows, :], gk_s[rows, :], gv_s[rows, :], f_s[rows, :]
        a_parts, o_parts = [], []
        for u in range(per):
            b = gi * per + u
            sl = slice(u * dec, (u + 1) * dec)
            k2 = jnp.concatenate([ck_ref[b], k8[sl], ztail], axis=0)
            v2 = jnp.concatenate([cv_ref[b], v8[sl], ztail], axis=0)
            a_parts.append(_swa_block(q8, k2, v2, mask, sink_ref)[sl])
            st = sin_ref[b]
            srows = []
            for h in range(GLA_HEADS):
                parts = []
                if h > 0:
                    parts.append(jnp.zeros((GLA_DK, h * GLA_DV), F32))
                parts.append(st[h])
                if h < GLA_HEADS - 1:
                    parts.append(jnp.zeros((GLA_DK, (GLA_HEADS - 1 - h) * GLA_DV), F32))
                srows.append(jnp.concatenate(parts, axis=1))
            sexp = jnp.concatenate(srows, axis=0)
            o, s_new = _gla_chunk(jnp.concatenate([gq8[sl], zrow], axis=0), jnp.concatenate([gk8[sl], zrow], axis=0),
                                  jnp.concatenate([gv8[sl], zrow_v], axis=0), jnp.concatenate([f8[sl], zrow], axis=0),
                                  sexp, 8)
            o_parts.append(o[:dec])
            for h in range(GLA_HEADS):
                sout_ref[b, h] = s_new[h * GLA_DK:(h + 1) * GLA_DK, h * GLA_DV:(h + 1) * GLA_DV]
        a_s[rows, :] = jnp.concatenate(a_parts, axis=0)
        o_s[rows, :] = jnp.concatenate(o_parts, axis=0)
        return carry

    lax.fori_loop(0, sb // per, group, 0)
    g_out = _gla_out(o_s[...], gg, gnorm_ref[...])
    mix = jnp.concatenate([a_s[...], g_out], axis=1).astype(BF16)
    y_ref[...] = x + _nn(mix, wout_ref[...])


def _mixer_sample(x, ct, st, wts, ck, cv, s0, *, dec, sb=16):
    n = x.shape[0]
    nseq = n // dec
    m = sb * dec
    row = lambda w: pl.BlockSpec((m, w), lambda i: (i, 0))
    seq3 = pl.BlockSpec((sb, WINDOW, LANES), lambda i: (i, 0, 0))
    seq4 = pl.BlockSpec((sb, GLA_HEADS, GLA_DK, GLA_DV), lambda i: (i, 0, 0, 0))
    kern = functools.partial(_mixer_sample_kernel, sb=sb, dec=dec)
    return pl.pallas_call(
        kern,
        grid=(nseq // sb,),
        in_specs=[row(D_MODEL), row(LANES), row(LANES)] + [_const_spec(w.shape) for w in wts] + [seq3, seq3, seq4],
        out_specs=[row(D_MODEL), row(LANES), row(LANES), seq4],
        out_shape=[jax.ShapeDtypeStruct((n, D_MODEL), F32), jax.ShapeDtypeStruct((n, LANES), F32),
                   jax.ShapeDtypeStruct((n, LANES), F32), jax.ShapeDtypeStruct(s0.shape, F32)],
        scratch_shapes=[pltpu.VMEM((m, QW), F32), pltpu.VMEM((m, LANES), F32), pltpu.VMEM((m, LANES), F32),
                        pltpu.VMEM((m, GK), F32), pltpu.VMEM((m, GK), F32), pltpu.VMEM((m, GV), F32),
                        pltpu.VMEM((m, GK), F32), pltpu.VMEM((m, QW), F32), pltpu.VMEM((m, GV), F32)],
        compiler_params=pltpu.CompilerParams(dimension_semantics=("arbitrary",),
                                             vmem_limit_bytes=48 * 1024 * 1024),
        name="mixer_sample",
    )(x, ct, st, *wts, ck, cv, s0)


def _shift_kernel(ck_ref, cv_ref, nk_ref, nv_ref, ok_ref, ov_ref, *, new):
    keep = ck_ref.shape[1] - new
    ok_ref[:, :keep] = ck_ref[:, new:]
    ok_ref[:, keep:] = nk_ref[...]
    ov_ref[:, :keep] = cv_ref[:, new:]
    ov_ref[:, keep:] = nv_ref[...]


def _shift_cache(ck, cv, nk, nv, *, sb=32):
    nseq, width = ck.shape
    new = nk.shape[1]
    big = pl.BlockSpec((sb, width), lambda i: (i, 0))
    small = pl.BlockSpec((sb, new), lambda i: (i, 0))
    return pl.pallas_call(
        functools.partial(_shift_kernel, new=new),
        grid=(nseq // sb,),
        in_specs=[big, big, small, small],
        out_specs=[big, big],
        out_shape=[jax.ShapeDtypeStruct(ck.shape, F32), jax.ShapeDtypeStruct(cv.shape, F32)],
        compiler_params=pltpu.CompilerParams(dimension_semantics=("arbitrary",)),
        name="shift_cache",
    )(ck, cv, nk, nv)


def _top16(s):
    iota = lax.broadcasted_iota(jnp.int32, s.shape, 0).astype(F32)
    vals, idxs = [], []
    for _ in range(PEER_TOPK):
        m = jnp.max(s, axis=0, keepdims=True)
        idx = jnp.min(jnp.where(s == m, iota, float(PEER_NKEYS)), axis=0, keepdims=True)
        s = jnp.where(iota == idx, NEG_INF, s)
        vals.append(m)
        idxs.append(idx)
    return jnp.concatenate(vals, axis=0), jnp.concatenate(idxs, axis=0)


def _pair_top16(a, ia, b, ib):
    t = a.shape[1]
    sub8 = lax.broadcasted_iota(jnp.int32, (8, t), 0).astype(F32)
    sub16 = lax.broadcasted_iota(jnp.int32, (16, t), 0).astype(F32)
    cands = [a[0:1] + b]
    codes = [sub16]
    exps = [ia[0:1] * float(PEER_NKEYS) + ib]
    for x in range(1, 8):
        ny = PEER_TOPK // (x + 1)
        cands.append(jnp.where(sub8 < float(ny), a[x:x + 1] + b[0:8], NEG_INF))
        codes.append(sub8 + float(16 * x))
        exps.append(ia[x:x + 1] * float(PEER_NKEYS) + ib[0:8])
    cands.append(a[8:16] + b[0:1])
    codes.append((sub8 + 8.0) * 16.0)
    exps.append(ia[8:16] * float(PEER_NKEYS) + ib[0:1])
    cand = jnp.concatenate(cands, axis=0)
    code = jnp.concatenate(codes, axis=0)
    eid = jnp.concatenate(exps, axis=0)
    vals, sel = [], []
    for _ in range(PEER_TOPK):
        m = jnp.max(cand, axis=0, keepdims=True)
        cmin = jnp.min(jnp.where(cand == m, code, 1e9), axis=0, keepdims=True)
        hit = code == cmin
        sel.append(jnp.max(jnp.where(hit, eid, -1.0), axis=0, keepdims=True))
        cand = jnp.where(hit, NEG_INF, cand)
        vals.append(m)
    return jnp.concatenate(vals, axis=0), jnp.concatenate(sel, axis=0)


def _peer_route_kernel(x_ref, fnorm_ref, wqt_ref, keys_ref, h_ref, i_ref, j_ref, g_ref, qt_s, e_s, w_s):
    hb = _rms(x_ref[...], fnorm_ref[...]).astype(BF16)
    h_ref[...] = hb
    qt_s[...] = _nt(wqt_ref[...], hb)

    def head(hh, carry):
        r0 = pl.multiple_of(hh * 2 * PEER_NKEYS, 2 * PEER_NKEYS)
        s0 = _nn(keys_ref[2 * hh], qt_s[pl.ds(r0, PEER_NKEYS), :].astype(BF16))
        s1 = _nn(keys_ref[2 * hh + 1], qt_s[pl.ds(r0 + PEER_NKEYS, PEER_NKEYS), :].astype(BF16))
        a, ia = _top16(s0)
        b, ib = _top16(s1)
        c, e = _pair_top16(a, ia, b, ib)
        p = jnp.exp(c - c[0:1])
        rows = pl.ds(pl.multiple_of(hh * PEER_TOPK, PEER_TOPK), PEER_TOPK)
        e_s[rows, :] = e
        w_s[rows, :] = p / jnp.sum(p, axis=0, keepdims=True)
        return carry

    lax.fori_loop(0, PEER_HEADS, head, 0)
    et = jnp.transpose(e_s[...])
    it = jnp.floor(et * (1.0 / PEER_NKEYS))
    i_ref[...] = it
    j_ref[...] = et - it * float(PEER_NKEYS)
    g_ref[...] = jnp.transpose(w_s[...])


def _peer_route(x, fnorm, wqt, keys, *, tm=256):
    t = x.shape[0]
    tm = min(tm, t)
    nsel = PEER_HEADS * PEER_TOPK
    row = lambda w: pl.BlockSpec((tm, w), lambda i: (i, 0))
    return pl.pallas_call(
        _peer_route_kernel,
        grid=(t // tm,),
        in_specs=[row(D_MODEL), _const_spec(fnorm.shape), _const_spec(wqt.shape), _const_spec(keys.shape)],
        out_specs=[row(D_MODEL), row(nsel), row(nsel), row(nsel)],
        out_shape=[jax.ShapeDtypeStruct((t, D_MODEL), BF16)] + [jax.ShapeDtypeStruct((t, nsel), F32)] * 3,
        scratch_shapes=[pltpu.VMEM((wqt.shape[0], tm), F32), pltpu.VMEM((nsel, tm), F32),
                        pltpu.VMEM((nsel, tm), F32)],
        compiler_params=pltpu.CompilerParams(dimension_semantics=("arbitrary",),
                                             vmem_limit_bytes=48 * 1024 * 1024),
        name="peer_route",
    )(x, fnorm, wqt, keys)


def _peer_gates_kernel(i_ref, j_ref, g_ref, w_ref, t_s, *, tm, group):
    sub = lax.broadcasted_iota(jnp.int32, (PEER_NKEYS, LANES), 0).astype(F32)

    def body(gi, carry):
        n0 = pl.multiple_of(gi * group, group)
        for t in range(group):
            ri = i_ref[pl.ds(n0 + t, 1), :]
            rj = j_ref[pl.ds(n0 + t, 1), :]
            rg = g_ref[pl.ds(n0 + t, 1), :]
            at = jnp.where(sub == ri, 1.0, 0.0).astype(BF16)
            bt = jnp.where(sub == rj, rg, 0.0).astype(BF16)
            t_s[t * W_PITCH:t * W_PITCH + PEER_NKEYS, :] = _nt(at, bt)
        for q in range(group // 16):
            for i in range(PEER_NKEYS):
                a = t_s[pl.ds((16 * q) * W_PITCH + i, 8, stride=W_PITCH), :]
                b = t_s[pl.ds((16 * q + 8) * W_PITCH + i, 8, stride=W_PITCH), :]
                w_ref[pl.ds(n0 + 16 * q, 16), i * PEER_NKEYS:(i + 1) * PEER_NKEYS] = (
                    jnp.concatenate([a, b], axis=0).astype(BF16))
        return carry

    lax.fori_loop(0, tm // group, body, 0)


def _peer_gates(ii, jj, gg, *, tm=256, group=32):
    t = ii.shape[0]
    tm = min(tm, t)
    n_exp = PEER_NKEYS * PEER_NKEYS
    tok = pl.BlockSpec((tm, LANES), lambda i: (i, 0))
    return pl.pallas_call(
        functools.partial(_peer_gates_kernel, tm=tm, group=group),
        grid=(t // tm,),
        in_specs=[tok, tok, tok],
        out_specs=pl.BlockSpec((tm, n_exp), lambda i: (i, 0)),
        out_shape=jax.ShapeDtypeStruct((t, n_exp), BF16),
        scratch_shapes=[pltpu.VMEM((group * W_PITCH, LANES), F32)],
        compiler_params=pltpu.CompilerParams(dimension_semantics=("arbitrary",),
                                             vmem_limit_bytes=48 * 1024 * 1024),
        name="peer_gates",
    )(ii, jj, gg)


def _peer_dense_kernel(h_ref, x_ref, w_ref, u_ref, v_ref, y_ref):
    @pl.when(pl.program_id(1) == 0)
    def _():
        y_ref[...] = x_ref[...]

    act = _nt(h_ref[...], u_ref[...])
    gelu = 0.5 * act * (1.0 + lax.erf(act * (2.0 ** -0.5)))
    z = (w_ref[...].astype(F32) * gelu).astype(BF16)
    y_ref[...] += _nn(z, v_ref[...])


def _peer_dense(h, x, w, u, v, *, tm=1024, te=1024):
    t = h.shape[0]
    tm = min(tm, t)
    n_exp = u.shape[0]
    tok = pl.BlockSpec((tm, D_MODEL), lambda i, e: (i, 0))
    exp = pl.BlockSpec((te, D_MODEL), lambda i, e: (e, 0))
    return pl.pallas_call(
        _peer_dense_kernel,
        grid=(t // tm, n_exp // te),
        in_specs=[tok, tok, pl.BlockSpec((tm, te), lambda i, e: (i, e)), exp, exp],
        out_specs=tok,
        out_shape=jax.ShapeDtypeStruct((t, D_MODEL), F32),
        compiler_params=pltpu.CompilerParams(dimension_semantics=("arbitrary", "arbitrary"),
                                             vmem_limit_bytes=56 * 1024 * 1024),
        name="peer_dense",
    )(h, x, w, u, v)


def _cast_kernel(x_ref, o_ref):
    o_ref[...] = x_ref[...].astype(o_ref.dtype)


def _to_bf16(x, *, rows=1024):
    spec = pl.BlockSpec((rows, x.shape[1]), lambda i: (i, 0))
    return pl.pallas_call(
        _cast_kernel, grid=(x.shape[0] // rows,), in_specs=[spec], out_specs=spec,
        out_shape=jax.ShapeDtypeStruct(x.shape, BF16),
        compiler_params=pltpu.CompilerParams(dimension_semantics=("arbitrary",)),
        name="cast_bf16",
    )(x)


def _rope_tables(pos):
    half = ROPE_DIM // 2
    inv_freq = ROPE_THETA ** (-jnp.arange(half, dtype=F32) / half)
    ang = pos.astype(F32)[:, None] * inv_freq[None, :]
    cos, sin = jnp.cos(ang), jnp.sin(ang)
    n = pos.shape[0]
    c64 = jnp.concatenate([cos, cos, jnp.ones((n, HEAD_DIM - ROPE_DIM), F32)], axis=1)
    s64 = jnp.concatenate([-sin, sin, jnp.zeros((n, HEAD_DIM - ROPE_DIM), F32)], axis=1)
    return jnp.tile(c64, (1, 2)), jnp.tile(s64, (1, 2))


def kernel(x_prompt, x_sample, cache_swa_k, cache_swa_v, state_gla, attn_norm, w_in, q_norm, k_norm, attn_sinks,
           w_gate, b_gate, gla_norm, w_out, ffn_norm, peer_wq, peer_keys, peer_u, peer_v):
    depth = w_in.shape[0]
    assert depth == 1 and x_prompt.shape[0] == 1
    seq = x_prompt.shape[1]
    nseq, dec = x_sample.shape[0], x_sample.shape[1]
    xp = x_prompt[0]
    xs = x_sample.reshape(nseq * dec, D_MODEL)
    l = 0

    win = jnp.pad(w_in[l], ((0, 0), (0, C_END - w_in.shape[2]))).astype(BF16)
    qkg = jnp.concatenate([jnp.tile(q_norm[l], SWA_Q_HEADS), jnp.tile(k_norm[l], 2)])[None, :]
    sinks = jnp.broadcast_to(attn_sinks[l][:, None], (SWA_Q_HEADS, LANES))
    wgate = jnp.pad(w_gate[l], ((0, LANES - GLA_LOWRANK), (0, 0))).astype(BF16)
    wts = (attn_norm[l][None, :], win, qkg, sinks, wgate, b_gate[l][None, :],
           jnp.tile(gla_norm[l], GLA_HEADS)[None, :], w_out[l].astype(BF16))
    fnorm = ffn_norm[l][None, :]
    wqt = jnp.transpose(peer_wq[l]).astype(BF16)
    keys = peer_keys[l].reshape(PEER_HEADS * 2, PEER_NKEYS, peer_keys.shape[-1]).astype(BF16)
    u_b = _to_bf16(peer_u[l])
    v_b = _to_bf16(peer_v[l])

    ct_p, st_p = _rope_tables(jnp.arange(seq, dtype=jnp.int32))
    ct_s, st_s = _rope_tables(PAST_LEN + jnp.arange(nseq * dec, dtype=jnp.int32) % dec)

    xp2, kp, vp, sp = _mixer_prompt(xp, ct_p, st_p, wts)
    hp, ip, jp, gp = _peer_route(xp2, fnorm, wqt, keys)
    yp = _peer_dense(hp, xp2, _peer_gates(ip, jp, gp), u_b, v_b)

    ck = cache_swa_k[l].reshape(nseq, WINDOW, LANES)
    cv = cache_swa_v[l].reshape(nseq, WINDOW, LANES)
    xs2, nk, nv, ss = _mixer_sample(xs, ct_s, st_s, wts, ck, cv, state_gla[l], dec=dec)
    ck2, cv2 = _shift_cache(ck.reshape(nseq, WINDOW * LANES), cv.reshape(nseq, WINDOW * LANES),
                            nk.reshape(nseq, dec * LANES), nv.reshape(nseq, dec * LANES))
    hs, is_, js, gs = _peer_route(xs2, fnorm, wqt, keys)
    ys = _peer_dense(hs, xs2, _peer_gates(is_, js, gs), u_b, v_b)

    kv_shape = (1, 1, WINDOW, 2, HEAD_DIM)
    return (yp[None], ys.reshape(nseq, dec, D_MODEL),
            kp.reshape(kv_shape), vp.reshape(kv_shape), sp[None, None],
            ck2.reshape(1, nseq, WINDOW, 2, HEAD_DIM), cv2.reshape(1, nseq, WINDOW, 2, HEAD_DIM), ss[None])
```

```python
import functools

import jax
import jax.numpy as jnp
from jax import lax
from jax.experimental import pallas as pl
from jax.experimental.pallas import tpu as pltpu

F32 = jnp.float32
BF16 = jnp.bfloat16
U32 = jnp.uint32

D_MODEL = 1024
HEAD_DIM = 64
SWA_Q_HEADS = 8
WINDOW = 128
ROPE_THETA = 500000.0
ROPE_DIM = 16
PAST_LEN = 16384
GLA_HEADS = 4
GLA_DK = 64
GLA_DV = 128
GLA_LOWRANK = 16
GLA_GATE_NORM = 16.0
PEER_HEADS = 8
PEER_NKEYS = 128
PEER_TOPK = 16
NORM_EPS = 1e-6

LANES = 128
QW = SWA_Q_HEADS * HEAD_DIM
GK = GLA_HEADS * GLA_DK
GV = GLA_HEADS * GLA_DV
C_Q, C_K, C_V, C_GQ, C_GK, C_GV, C_GG, C_GA, C_END = 0, 512, 640, 768, 1024, 1280, 1792, 2304, 2432
GLA_PAD = 128
W_PITCH = 136
NEG_INF = float("-inf")


def _nn(a, b, precision=None):
    return jnp.dot(a, b, preferred_element_type=F32, precision=precision)


def _nt(a, b):
    return lax.dot_general(a, b, (((1,), (1,)), ((), ())), preferred_element_type=F32)


def _rms(x, gain):
    ms = jnp.mean(x * x, axis=-1, keepdims=True)
    return x * lax.rsqrt(ms + NORM_EPS) * gain


def _head_norm_rope(x, gain, ctab, stab):
    lane = lax.broadcasted_iota(jnp.int32, x.shape, 1)
    lo = lane < HEAD_DIM
    sq = x * x
    ms_lo = jnp.sum(jnp.where(lo, sq, 0.0), axis=-1, keepdims=True)
    ms_hi = jnp.sum(jnp.where(lo, 0.0, sq), axis=-1, keepdims=True)
    ms = jnp.where(lo, ms_lo, ms_hi) * (1.0 / HEAD_DIM)
    xn = x * lax.rsqrt(ms + NORM_EPS) * gain
    first = (lane & (HEAD_DIM - 1)) < (ROPE_DIM // 2)
    partner = jnp.where(first, pltpu.roll(xn, LANES - ROPE_DIM // 2, 1), pltpu.roll(xn, ROPE_DIM // 2, 1))
    return xn * ctab + partner * stab


def _project(x, ct, st, anorm, win, qkg, wgate, bgate):
    h = _rms(x, anorm).astype(BF16)
    p = _nn(h, win)
    qk = [
        _head_norm_rope(p[:, c * LANES:(c + 1) * LANES], qkg[:, c * LANES:(c + 1) * LANES], ct, st)
        for c in range(C_V // LANES)
    ]
    q = jnp.concatenate(qk[:4], axis=1) * (HEAD_DIM ** -0.5)
    k = qk[4]
    v = p[:, C_V:C_GQ]
    gq = p[:, C_GQ:C_GK] * (GLA_DK ** -0.5)
    gk = p[:, C_GK:C_GV]
    gv = p[:, C_GV:C_GG]
    gg = p[:, C_GG:C_GA]
    z = _nn(p[:, C_GA:C_END].astype(BF16), wgate) + bgate
    logf = (jnp.minimum(z, 0.0) - jnp.log(1.0 + jnp.exp(-jnp.abs(z)))) * (1.0 / GLA_GATE_NORM)
    return q, k, v, gq, gk, gv, gg, logf


def _swa_block(q, k2, v2, mask, sink_ref):
    lane = lax.broadcasted_iota(jnp.int32, k2.shape, 1)
    lo = lane < HEAD_DIM
    krot = pltpu.roll(k2, HEAD_DIM, 1)
    vrot = pltpu.roll(v2, HEAD_DIM, 1)
    zero = jnp.zeros_like(k2)
    nk = k2.shape[0]
    olane = lax.broadcasted_iota(jnp.int32, (q.shape[0], LANES), 1) < HEAD_DIM
    outs = []
    for g in range(2):
        ka, kb = (k2, krot) if g == 0 else (krot, k2)
        va, vb = (v2, vrot) if g == 0 else (vrot, v2)
        kexp = jnp.concatenate([jnp.where(lo, ka, zero), jnp.where(lo, zero, kb)], axis=0).astype(BF16)
        vexp = jnp.concatenate([jnp.where(lo, va, zero), jnp.where(lo, zero, vb)], axis=0).astype(BF16)
        for c in (2 * g, 2 * g + 1):
            s = _nt(q[:, c * LANES:(c + 1) * LANES].astype(BF16), kexp)
            ps, rs = [], []
            for hh in range(2):
                sh = jnp.where(mask, s[:, hh * nk:(hh + 1) * nk], NEG_INF)
                sink = sink_ref[2 * c + hh:2 * c + hh + 1, 0:1]
                m = jnp.maximum(jnp.max(sh, axis=-1, keepdims=True), sink)
                pe = jnp.exp(sh - m)
                den = jnp.sum(pe, axis=-1, keepdims=True) + jnp.exp(sink - m)
                ps.append(pe)
                rs.append(1.0 / den)
            o = _nn(jnp.concatenate(ps, axis=1).astype(BF16), vexp)
            outs.append(o * jnp.where(olane, rs[0], rs[1]))
    return jnp.concatenate(outs, axis=1)


def _zpad(a, rows):
    if a.shape[0] == rows:
        return a
    return jnp.concatenate([a, jnp.zeros((rows - a.shape[0], a.shape[1]), a.dtype)], axis=0)


def _gla_chunk(q, k, v, f, s, sub):
    c = q.shape[0]
    nsub = c // sub
    shift = sub.bit_length() - 1
    fp = _zpad(f, GLA_PAD)
    r = lax.broadcasted_iota(jnp.int32, (c, GLA_PAD), 0)
    cc = lax.broadcasted_iota(jnp.int32, (c, GLA_PAD), 1)
    bstart = (r >> shift) << shift
    causal = cc <= r
    hi = lax.Precision.HIGHEST
    b = _nn(causal.astype(F32), fp, hi)
    bl = _nn((causal & (cc >= bstart)).astype(F32), fp, hi)
    blast_row = b[c - 1:c]
    blast_col = jnp.sum(jnp.transpose(fp), axis=1, keepdims=True)

    lane_k = lax.broadcasted_iota(jnp.int32, (c, GK), 1) >> 6
    lane_v = lax.broadcasted_iota(jnp.int32, (c, GV), 1) >> 7
    rows = lax.broadcasted_iota(jnp.int32, (c, GK), 0)

    def expand_k(kt):
        return jnp.concatenate(
            [_zpad(jnp.where(lane_k == h, kt, 0.0), GLA_PAD) for h in range(GLA_HEADS)], axis=0).astype(BF16)

    a = _nt((q * jnp.exp(bl)).astype(BF16), expand_k(k * jnp.exp(-bl)))
    acol = lax.broadcasted_iota(jnp.int32, (c, GLA_HEADS * GLA_PAD), 1) & (GLA_PAD - 1)
    arow = lax.broadcasted_iota(jnp.int32, (c, GLA_HEADS * GLA_PAD), 0)
    a = jnp.where((acol <= arow) & (acol >= ((arow >> shift) << shift)), a, 0.0)
    if nsub > 1:
        qs, ks = [], []
        for j in range(nsub - 1):
            e = (j + 1) * sub
            bj = b[e - 1:e]
            qs.append(jnp.where(rows >= e, q * jnp.exp(jnp.minimum(b - bj, 0.0)), 0.0).astype(BF16))
            ks.append(expand_k(jnp.where((rows >= e - sub) & (rows < e), k * jnp.exp(jnp.minimum(bj - b, 0.0)), 0.0)))
        a = a + _nt(jnp.concatenate(qs, axis=1), jnp.concatenate(ks, axis=1))
    vexp = jnp.concatenate(
        [_zpad(jnp.where(lane_v == h, v, 0.0), GLA_PAD) for h in range(GLA_HEADS)], axis=0).astype(BF16)
    o = _nn(a.astype(BF16), vexp) + _nn((q * jnp.exp(b)).astype(BF16), s.astype(BF16))

    kbar_t = jnp.transpose(_zpad(k * jnp.exp(blast_row - b), GLA_PAD)).astype(BF16)
    upd = _nn(kbar_t, _zpad(v, GLA_PAD).astype(BF16))
    srow = lax.broadcasted_iota(jnp.int32, s.shape, 0) >> 6
    scol = lax.broadcasted_iota(jnp.int32, s.shape, 1) >> 7
    s_new = s * jnp.exp(blast_col) + jnp.where(srow == scol, upd, 0.0)
    return o, s_new


def _gla_out(o, gg, gnorm):
    outs = []
    for h in range(GLA_HEADS):
        oh = o[:, h * GLA_DV:(h + 1) * GLA_DV]
        ms = jnp.mean(oh * oh, axis=-1, keepdims=True)
        outs.append(oh * lax.rsqrt(ms + NORM_EPS))
    on = jnp.concatenate(outs, axis=1) * gnorm
    return on * (gg / (1.0 + jnp.exp(-gg)))


def _mixer_prompt_kernel(x_ref, ct_ref, st_ref, anorm_ref, win_ref, qkg_ref, sink_ref, wgate_ref, bgate_ref,
                         gnorm_ref, wout_ref, y_ref, klast_ref, vlast_ref, sfin_ref,
                         kprev_ref, vprev_ref, s_ref, *, tm, chunk, sub):
    i = pl.program_id(0)

    @pl.when(i == 0)
    def _():
        kprev_ref[...] = jnp.zeros_like(kprev_ref)
        vprev_ref[...] = jnp.zeros_like(vprev_ref)
        s_ref[...] = jnp.zeros_like(s_ref)

    x = x_ref[...]
    q, k, v, gq, gk, gv, gg, logf = _project(
        x, ct_ref[...], st_ref[...], anorm_ref[...], win_ref[...], qkg_ref[...], wgate_ref[...], bgate_ref[...])

    kall = jnp.concatenate([kprev_ref[...], k], axis=0)
    vall = jnp.concatenate([vprev_ref[...], v], axis=0)
    qi = lax.broadcasted_iota(jnp.int32, (WINDOW, 2 * WINDOW), 0)
    kj = lax.broadcasted_iota(jnp.int32, (WINDOW, 2 * WINDOW), 1)
    band = (kj >= qi) & (kj <= qi + WINDOW)
    blocks = []
    for j in range(tm // WINDOW):
        mask = band & (kj >= jnp.where(i == 0, WINDOW, 0)) if j == 0 else band
        blocks.append(_swa_block(q[j * WINDOW:(j + 1) * WINDOW], kall[j * WINDOW:(j + 2) * WINDOW],
                                 vall[j * WINDOW:(j + 2) * WINDOW], mask, sink_ref))
    a_out = jnp.concatenate(blocks, axis=0)
    kprev_ref[...] = k[tm - WINDOW:]
    vprev_ref[...] = v[tm - WINDOW:]
    klast_ref[...] = k[tm - WINDOW:]
    vlast_ref[...] = v[tm - WINDOW:]

    s = s_ref[...]
    os_ = []
    for c in range(tm // chunk):
        sl = slice(c * chunk, (c + 1) * chunk)
        o, s = _gla_chunk(gq[sl], gk[sl], gv[sl], logf[sl], s, sub)
        os_.append(o)
    s_ref[...] = s
    for h in range(GLA_HEADS):
        sfin_ref[h] = s[h * GLA_DK:(h + 1) * GLA_DK, h * GLA_DV:(h + 1) * GLA_DV]
    g_out = _gla_out(jnp.concatenate(os_, axis=0), gg, gnorm_ref[...])

    mix = jnp.concatenate([a_out, g_out], axis=1).astype(BF16)
    y_ref[...] = x + _nn(mix, wout_ref[...])


def _const_spec(shape):
    return pl.BlockSpec(shape, lambda *_: (0,) * len(shape))


def _mixer_prompt(x, ct, st, wts, *, tm=256, chunk=128, sub=32):
    t = x.shape[0]
    tm = min(tm, t)
    anorm, win, qkg, sinks, wgate, bgate, gnorm, wout = wts
    row = lambda w: pl.BlockSpec((tm, w), lambda i: (i, 0))
    kern = functools.partial(_mixer_prompt_kernel, tm=tm, chunk=chunk, sub=sub)
    return pl.pallas_call(
        kern,
        grid=(t // tm,),
        in_specs=[row(D_MODEL), row(LANES), row(LANES)] + [_const_spec(w.shape) for w in wts],
        out_specs=[row(D_MODEL), _const_spec((WINDOW, LANES)), _const_spec((WINDOW, LANES)),
                   _const_spec((GLA_HEADS, GLA_DK, GLA_DV))],
        out_shape=[jax.ShapeDtypeStruct((t, D_MODEL), F32), jax.ShapeDtypeStruct((WINDOW, LANES), F32),
                   jax.ShapeDtypeStruct((WINDOW, LANES), F32),
                   jax.ShapeDtypeStruct((GLA_HEADS, GLA_DK, GLA_DV), F32)],
        scratch_shapes=[pltpu.VMEM((WINDOW, LANES), F32), pltpu.VMEM((WINDOW, LANES), F32),
                        pltpu.VMEM((GK, GV), F32)],
        compiler_params=pltpu.CompilerParams(dimension_semantics=("arbitrary",),
                                             vmem_limit_bytes=48 * 1024 * 1024),
        name="mixer_prompt",
    )(x, ct, st, *wts)


def _mixer_sample_kernel(x_ref, ct_ref, st_ref, anorm_ref, win_ref, qkg_ref, sink_ref, wgate_ref, bgate_ref,
                         gnorm_ref, wout_ref, ck_ref, cv_ref, sin_ref,
                         y_ref, knew_ref, vnew_ref, sout_ref,
                         q_s, k_s, v_s, gq_s, gk_s, gv_s, f_s, a_s, o_s, *, sb, dec):
    x = x_ref[...]
    q, k, v, gq, gk, gv, gg, logf = _project(
        x, ct_ref[...], st_ref[...], anorm_ref[...], win_ref[...], qkg_ref[...], wgate_ref[...], bgate_ref[...])
    knew_ref[...] = k
    vnew_ref[...] = v
    q_s[...] = q
    k_s[...] = k
    v_s[...] = v
    gq_s[...] = gq
    gk_s[...] = gk
    gv_s[...] = gv
    f_s[...] = logf

    per = 8 // dec
    qi = lax.broadcasted_iota(jnp.int32, (8, 2 * WINDOW), 0) & (dec - 1)
    kj = lax.broadcasted_iota(jnp.int32, (8, 2 * WINDOW), 1)
    mask = (kj >= qi) & (kj <= qi + WINDOW)
    ztail = jnp.zeros((WINDOW - dec, LANES), F32)
    zrow = jnp.zeros((8 - dec, GK), F32)
    zrow_v = jnp.zeros((8 - dec, GV), F32)

    def group(gi, carry):
        r0 = pl.multiple_of(gi * 8, 8)
        rows = pl.ds(r0, 8)
        q8, k8, v8 = q_s[rows, :], k_s[rows, :], v_s[rows, :]
        gq8, gk8, gv8, f8 = gq_s[rows, :], gk_s[rows, :], gv_s[rows, :], f_s[rows, :]
        a_parts, o_parts = [], []
        for u in range(per):
            b = gi * per + u
            sl = slice(u * dec, (u + 1) * dec)
            k2 = jnp.concatenate([ck_ref[b], k8[sl], ztail], axis=0)
            v2 = jnp.concatenate([cv_ref[b], v8[sl], ztail], axis=0)
            a_parts.append(_swa_block(q8, k2, v2, mask, sink_ref)[sl])
            st = sin_ref[b]
            srows = []
            for h in range(GLA_HEADS):
                parts = []
                if h > 0:
                    parts.append(jnp.zeros((GLA_DK, h * GLA_DV), F32))
                parts.append(st[h])
                if h < GLA_HEADS - 1:
                    parts.append(jnp.zeros((GLA_DK, (GLA_HEADS - 1 - h) * GLA_DV), F32))
                srows.append(jnp.concatenate(parts, axis=1))
            sexp = jnp.concatenate(srows, axis=0)
            o, s_new = _gla_chunk(jnp.concatenate([gq8[sl], zrow], axis=0), jnp.concatenate([gk8[sl], zrow], axis=0),
                                  jnp.concatenate([gv8[sl], zrow_v], axis=0), jnp.concatenate([f8[sl], zrow], axis=0),
                                  sexp, 8)
            o_parts.append(o[:dec])
            for h in range(GLA_HEADS):
                sout_ref[b, h] = s_new[h * GLA_DK:(h + 1) * GLA_DK, h * GLA_DV:(h + 1) * GLA_DV]
        a_s[rows, :] = jnp.concatenate(a_parts, axis=0)
        o_s[rows, :] = jnp.concatenate(o_parts, axis=0)
        return carry

    lax.fori_loop(0, sb // per, group, 0)
    g_out = _gla_out(o_s[...], gg, gnorm_ref[...])
    mix = jnp.concatenate([a_s[...], g_out], axis=1).astype(BF16)
    y_ref[...] = x + _nn(mix, wout_ref[...])


def _mixer_sample(x, ct, st, wts, ck, cv, s0, *, dec, sb=16):
    n = x.shape[0]
    nseq = n // dec
    m = sb * dec
    row = lambda w: pl.BlockSpec((m, w), lambda i: (i, 0))
    seq3 = pl.BlockSpec((sb, WINDOW, LANES), lambda i: (i, 0, 0))
    seq4 = pl.BlockSpec((sb, GLA_HEADS, GLA_DK, GLA_DV), lambda i: (i, 0, 0, 0))
    kern = functools.partial(_mixer_sample_kernel, sb=sb, dec=dec)
    return pl.pallas_call(
        kern,
        grid=(nseq // sb,),
        in_specs=[row(D_MODEL), row(LANES), row(LANES)] + [_const_spec(w.shape) for w in wts] + [seq3, seq3, seq4],
        out_specs=[row(D_MODEL), row(LANES), row(LANES), seq4],
        out_shape=[jax.ShapeDtypeStruct((n, D_MODEL), F32), jax.ShapeDtypeStruct((n, LANES), F32),
                   jax.ShapeDtypeStruct((n, LANES), F32), jax.ShapeDtypeStruct(s0.shape, F32)],
        scratch_shapes=[pltpu.VMEM((m, QW), F32), pltpu.VMEM((m, LANES), F32), pltpu.VMEM((m, LANES), F32),
                        pltpu.VMEM((m, GK), F32), pltpu.VMEM((m, GK), F32), pltpu.VMEM((m, GV), F32),
                        pltpu.VMEM((m, GK), F32), pltpu.VMEM((m, QW), F32), pltpu.VMEM((m, GV), F32)],
        compiler_params=pltpu.CompilerParams(dimension_semantics=("arbitrary",),
                                             vmem_limit_bytes=48 * 1024 * 1024),
        name="mixer_sample",
    )(x, ct, st, *wts, ck, cv, s0)


def _shift_kernel(ck_ref, cv_ref, nk_ref, nv_ref, ok_ref, ov_ref, *, new):
    keep = ck_ref.shape[1] - new
    ok_ref[:, :keep] = ck_ref[:, new:]
    ok_ref[:, keep:] = nk_ref[...]
    ov_ref[:, :keep] = cv_ref[:, new:]
    ov_ref[:, keep:] = nv_ref[...]


def _shift_cache(ck, cv, nk, nv, *, sb=32):
    nseq, width = ck.shape
    new = nk.shape[1]
    big = pl.BlockSpec((sb, width), lambda i: (i, 0))
    small = pl.BlockSpec((sb, new), lambda i: (i, 0))
    return pl.pallas_call(
        functools.partial(_shift_kernel, new=new),
        grid=(nseq // sb,),
        in_specs=[big, big, small, small],
        out_specs=[big, big],
        out_shape=[jax.ShapeDtypeStruct(ck.shape, F32), jax.ShapeDtypeStruct(cv.shape, F32)],
        compiler_params=pltpu.CompilerParams(dimension_semantics=("arbitrary",)),
        name="shift_cache",
    )(ck, cv, nk, nv)


def _sort_network(n):
    pairs = []

    def merge(lo, m, r):
        step = 2 * r
        if step < m:
            merge(lo, m, step)
            merge(lo + r, m, step)
            pairs.extend((i, i + r) for i in range(lo + r, lo + m - r, step))
        else:
            pairs.append((lo, lo + r))

    def sort(lo, m):
        if m > 1:
            sort(lo, m // 2)
            sort(lo + m // 2, m // 2)
            merge(lo, m, 1)

    sort(0, n)
    return pairs


def _top16(s):
    t = s.shape[1]
    nv = PEER_NKEYS // 8
    sub8 = lax.broadcasted_iota(jnp.int32, (8, t), 0).astype(F32)
    v = [s[8 * k:8 * k + 8] for k in range(nv)]
    ix = [sub8 + float(8 * k) for k in range(nv)]
    for p, q in _sort_network(nv):
        swap = (v[q] > v[p]) | ((v[q] == v[p]) & (ix[q] < ix[p]))
        v[p], v[q] = jnp.where(swap, v[q], v[p]), jnp.where(swap, v[p], v[q])
        ix[p], ix[q] = jnp.where(swap, ix[q], ix[p]), jnp.where(swap, ix[p], ix[q])
    vals, idxs = [], []
    for n in range(PEER_TOPK):
        m = jnp.max(v[0], axis=0, keepdims=True)
        imin = jnp.min(jnp.where(v[0] == m, ix[0], float(PEER_NKEYS)), axis=0, keepdims=True)
        hit = ix[0] == imin
        vals.append(m)
        idxs.append(imin)
        for k in range(PEER_TOPK - 1 - n):
            v[k] = jnp.where(hit, v[k + 1], v[k])
            ix[k] = jnp.where(hit, ix[k + 1], ix[k])
    return jnp.concatenate(vals, axis=0), jnp.concatenate(idxs, axis=0)


def _pair_top16(a, ia, b, ib):
    t = a.shape[1]
    sub8 = lax.broadcasted_iota(jnp.int32, (8, t), 0).astype(F32)
    ia_lo = ia[0:8] * float(PEER_NKEYS)
    lv, le = [], []
    for y in range(PEER_TOPK):
        nx = PEER_TOPK // (y + 1)
        val = a[0:8] + b[y:y + 1]
        lv.append(val if nx >= 8 else jnp.where(sub8 < float(nx), val, NEG_INF))
        le.append(ia_lo + ib[y:y + 1])
    hv = a[8:16] + b[0:1]
    he = ia[8:16] * float(PEER_NKEYS) + ib[0:1]
    code_lo = sub8 * float(PEER_TOPK)
    code_hi = (sub8 + 8.0) * float(PEER_TOPK)
    vals, sel = [], []
    for n in range(PEER_TOPK):
        m = jnp.max(jnp.maximum(lv[0], hv), axis=0, keepdims=True)
        cmin = jnp.min(jnp.minimum(jnp.where(lv[0] == m, code_lo, 1e9), jnp.where(hv == m, code_hi, 1e9)),
                       axis=0, keepdims=True)
        hit_lo = code_lo == cmin
        hit_hi = code_hi == cmin
        sel.append(jnp.max(jnp.maximum(jnp.where(hit_lo, le[0], -1.0), jnp.where(hit_hi, he, -1.0)),
                           axis=0, keepdims=True))
        vals.append(m)
        for y in range(PEER_TOPK - 1 - n):
            lv[y] = jnp.where(hit_lo, lv[y + 1], lv[y])
            le[y] = jnp.where(hit_lo, le[y + 1], le[y])
        code_lo = jnp.where(hit_lo, code_lo + 1.0, code_lo)
        hv = jnp.where(hit_hi, NEG_INF, hv)
    return jnp.concatenate(vals, axis=0), jnp.concatenate(sel, axis=0)


def _peer_route_kernel(x_ref, fnorm_ref, wqt_ref, keys_ref, h_ref, i_ref, j_ref, g_ref, qt_s, e_s, w_s):
    hb = _rms(x_ref[...], fnorm_ref[...]).astype(BF16)
    h_ref[...] = hb
    qt_s[...] = _nt(wqt_ref[...], hb)

    def head(hh, carry):
        r0 = pl.multiple_of(hh * 2 * PEER_NKEYS, 2 * PEER_NKEYS)
        s0 = _nn(keys_ref[2 * hh], qt_s[pl.ds(r0, PEER_NKEYS), :].astype(BF16))
        s1 = _nn(keys_ref[2 * hh + 1], qt_s[pl.ds(r0 + PEER_NKEYS, PEER_NKEYS), :].astype(BF16))
        a, ia = _top16(s0)
        b, ib = _top16(s1)
        c, e = _pair_top16(a, ia, b, ib)
        p = jnp.exp(c - c[0:1])
        rows = pl.ds(pl.multiple_of(hh * PEER_TOPK, PEER_TOPK), PEER_TOPK)
        e_s[rows, :] = e
        w_s[rows, :] = p / jnp.sum(p, axis=0, keepdims=True)
        return carry

    lax.fori_loop(0, PEER_HEADS, head, 0)
    et = jnp.transpose(e_s[...])
    it = jnp.floor(et * (1.0 / PEER_NKEYS))
    i_ref[...] = it
    j_ref[...] = et - it * float(PEER_NKEYS)
    g_ref[...] = jnp.transpose(w_s[...])


def _peer_route(x, fnorm, wqt, keys, *, tm=512):
    t = x.shape[0]
    tm = min(tm, t)
    nsel = PEER_HEADS * PEER_TOPK
    row = lambda w: pl.BlockSpec((tm, w), lambda i: (i, 0))
    return pl.pallas_call(
        _peer_route_kernel,
        grid=(t // tm,),
        in_specs=[row(D_MODEL), _const_spec(fnorm.shape), _const_spec(wqt.shape), _const_spec(keys.shape)],
        out_specs=[row(D_MODEL), row(nsel), row(nsel), row(nsel)],
        out_shape=[jax.ShapeDtypeStruct((t, D_MODEL), BF16)] + [jax.ShapeDtypeStruct((t, nsel), F32)] * 3,
        scratch_shapes=[pltpu.VMEM((wqt.shape[0], tm), F32), pltpu.VMEM((nsel, tm), F32),
                        pltpu.VMEM((nsel, tm), F32)],
        compiler_params=pltpu.CompilerParams(dimension_semantics=("arbitrary",),
                                             vmem_limit_bytes=48 * 1024 * 1024),
        name="peer_route",
    )(x, fnorm, wqt, keys)


def _peer_gates_kernel(i_ref, j_ref, g_ref, w_ref, t_s, *, tm, group):
    sub = lax.broadcasted_iota(jnp.int32, (PEER_NKEYS, LANES), 0).astype(BF16)
    one = jnp.ones((PEER_NKEYS, LANES), BF16)
    zero = jnp.zeros((PEER_NKEYS, LANES), BF16)

    def body(g, carry):
        n0 = pl.multiple_of(g * group, group)
        for t in range(group):
            ri = i_ref[pl.ds(n0 + t, 1), :].astype(BF16)
            rj = j_ref[pl.ds(n0 + t, 1), :].astype(BF16)
            rg = jnp.broadcast_to(g_ref[pl.ds(n0 + t, 1), :].astype(BF16), sub.shape)
            at = jnp.where(sub == ri, one, zero)
            bt = jnp.where(sub == rj, rg, zero)
            t_s[t * W_PITCH:t * W_PITCH + PEER_NKEYS, :] = _nt(at, bt)
        for q in range(group // 16):
            for i in range(PEER_NKEYS):
                a = t_s[pl.ds((16 * q) * W_PITCH + i, 8, stride=W_PITCH), :]
                b = t_s[pl.ds((16 * q + 8) * W_PITCH + i, 8, stride=W_PITCH), :]
                w_ref[pl.ds(n0 + 16 * q, 16), i * PEER_NKEYS:(i + 1) * PEER_NKEYS] = (
                    jnp.concatenate([a, b], axis=0).astype(BF16))
        return carry

    lax.fori_loop(0, tm // group, body, 0)


def _peer_gates(ii, jj, gg, *, tm=256, group=32):
    t = ii.shape[0]
    tm = min(tm, t)
    n_exp = PEER_NKEYS * PEER_NKEYS
    tok = pl.BlockSpec((tm, LANES), lambda i: (i, 0))
    return pl.pallas_call(
        functools.partial(_peer_gates_kernel, tm=tm, group=group),
        grid=(t // tm,),
        in_specs=[tok, tok, tok],
        out_specs=pl.BlockSpec((tm, n_exp), lambda i: (i, 0)),
        out_shape=jax.ShapeDtypeStruct((t, n_exp), BF16),
        scratch_shapes=[pltpu.VMEM((group * W_PITCH, LANES), F32)],
        compiler_params=pltpu.CompilerParams(dimension_semantics=("arbitrary",),
                                             vmem_limit_bytes=48 * 1024 * 1024),
        name="peer_gates",
    )(ii, jj, gg)


def _peer_dense_kernel(h_ref, x_ref, w_ref, u_ref, v_ref, y_ref):
    @pl.when(pl.program_id(1) == 0)
    def _():
        y_ref[...] = x_ref[...]

    act = _nt(h_ref[...], u_ref[...])
    gelu = 0.5 * act * (1.0 + lax.erf(act * (2.0 ** -0.5)))
    z = (w_ref[...].astype(F32) * gelu).astype(BF16)
    y_ref[...] += _nn(z, v_ref[...])


def _peer_dense(h, x, w, u, v, *, tm=1024, te=1024):
    t = h.shape[0]
    tm = min(tm, t)
    n_exp = u.shape[0]
    tok = pl.BlockSpec((tm, D_MODEL), lambda i, e: (i, 0))
    exp = pl.BlockSpec((te, D_MODEL), lambda i, e: (e, 0))
    return pl.pallas_call(
        _peer_dense_kernel,
        grid=(t // tm, n_exp // te),
        in_specs=[tok, tok, pl.BlockSpec((tm, te), lambda i, e: (i, e)), exp, exp],
        out_specs=tok,
        out_shape=jax.ShapeDtypeStruct((t, D_MODEL), F32),
        compiler_params=pltpu.CompilerParams(dimension_semantics=("arbitrary", "arbitrary"),
                                             vmem_limit_bytes=56 * 1024 * 1024),
        name="peer_dense",
    )(h, x, w, u, v)


def _cast_kernel(x_ref, o_ref):
    o_ref[...] = x_ref[...].astype(o_ref.dtype)


def _to_bf16(x, *, rows=1024):
    spec = pl.BlockSpec((rows, x.shape[1]), lambda i: (i, 0))
    return pl.pallas_call(
        _cast_kernel, grid=(x.shape[0] // rows,), in_specs=[spec], out_specs=spec,
        out_shape=jax.ShapeDtypeStruct(x.shape, BF16),
        compiler_params=pltpu.CompilerParams(dimension_semantics=("arbitrary",)),
        name="cast_bf16",
    )(x)


def _rope_tables(pos):
    half = ROPE_DIM // 2
    inv_freq = ROPE_THETA ** (-jnp.arange(half, dtype=F32) / half)
    ang = pos.astype(F32)[:, None] * inv_freq[None, :]
    cos, sin = jnp.cos(ang), jnp.sin(ang)
    n = pos.shape[0]
    c64 = jnp.concatenate([cos, cos, jnp.ones((n, HEAD_DIM - ROPE_DIM), F32)], axis=1)
    s64 = jnp.concatenate([-sin, sin, jnp.zeros((n, HEAD_DIM - ROPE_DIM), F32)], axis=1)
    return jnp.tile(c64, (1, 2)), jnp.tile(s64, (1, 2))


def kernel(x_prompt, x_sample, cache_swa_k, cache_swa_v, state_gla, attn_norm, w_in, q_norm, k_norm, attn_sinks,
           w_gate, b_gate, gla_norm, w_out, ffn_norm, peer_wq, peer_keys, peer_u, peer_v):
    depth = w_in.shape[0]
    assert depth == 1 and x_prompt.shape[0] == 1
    seq = x_prompt.shape[1]
    nseq, dec = x_sample.shape[0], x_sample.shape[1]
    xp = x_prompt[0]
    xs = x_sample.reshape(nseq * dec, D_MODEL)
    l = 0

    win = jnp.pad(w_in[l], ((0, 0), (0, C_END - w_in.shape[2]))).astype(BF16)
    qkg = jnp.concatenate([jnp.tile(q_norm[l], SWA_Q_HEADS), jnp.tile(k_norm[l], 2)])[None, :]
    sinks = jnp.broadcast_to(attn_sinks[l][:, None], (SWA_Q_HEADS, LANES))
    wgate = jnp.pad(w_gate[l], ((0, LANES - GLA_LOWRANK), (0, 0))).astype(BF16)
    wts = (attn_norm[l][None, :], win, qkg, sinks, wgate, b_gate[l][None, :],
           jnp.tile(gla_norm[l], GLA_HEADS)[None, :], w_out[l].astype(BF16))
    fnorm = ffn_norm[l][None, :]
    wqt = jnp.transpose(peer_wq[l]).astype(BF16)
    keys = peer_keys[l].reshape(PEER_HEADS * 2, PEER_NKEYS, peer_keys.shape[-1]).astype(BF16)
    u_b = _to_bf16(peer_u[l])
    v_b = _to_bf16(peer_v[l])

    ct_p, st_p = _rope_tables(jnp.arange(seq, dtype=jnp.int32))
    ct_s, st_s = _rope_tables(PAST_LEN + jnp.arange(nseq * dec, dtype=jnp.int32) % dec)

    xp2, kp, vp, sp = _mixer_prompt(xp, ct_p, st_p, wts)
    hp, ip, jp, gp = _peer_route(xp2, fnorm, wqt, keys)
    yp = _peer_dense(hp, xp2, _peer_gates(ip, jp, gp), u_b, v_b)

    ck = cache_swa_k[l].reshape(nseq, WINDOW, LANES)
    cv = cache_swa_v[l].reshape(nseq, WINDOW, LANES)
    xs2, nk, nv, ss = _mixer_sample(xs, ct_s, st_s, wts, ck, cv, state_gla[l], dec=dec)
    ck2, cv2 = _shift_cache(ck.reshape(nseq, WINDOW * LANES), cv.reshape(nseq, WINDOW * LANES),
                            nk.reshape(nseq, dec * LANES), nv.reshape(nseq, dec * LANES))
    hs, is_, js, gs = _peer_route(xs2, fnorm, wqt, keys)
    ys = _peer_dense(hs, xs2, _peer_gates(is_, js, gs), u_b, v_b)

    kv_shape = (1, 1, WINDOW, 2, HEAD_DIM)
    return (yp[None], ys.reshape(nseq, dec, D_MODEL),
            kp.reshape(kv_shape), vp.reshape(kv_shape), sp[None, None],
            ck2.reshape(1, nseq, WINDOW, 2, HEAD_DIM), cv2.reshape(1, nseq, WINDOW, 2, HEAD_DIM), ss[None])
```

```python
import functools

import jax
import jax.numpy as jnp
from jax import lax
from jax.experimental import pallas as pl
from jax.experimental.pallas import tpu as pltpu

F32 = jnp.float32
BF16 = jnp.bfloat16

D_MODEL = 1024
HEAD_DIM = 64
SWA_Q_HEADS = 8
WINDOW = 128
ROPE_THETA = 500000.0
ROPE_DIM = 16
PAST_LEN = 16384
GLA_HEADS = 4
GLA_DK = 64
GLA_DV = 128
GLA_LOWRANK = 16
GLA_GATE_NORM = 16.0
PEER_HEADS = 8
PEER_NKEYS = 128
PEER_TOPK = 16
NORM_EPS = 1e-6

LANES = 128
QW = SWA_Q_HEADS * HEAD_DIM
GK = GLA_HEADS * GLA_DK
GV = GLA_HEADS * GLA_DV
C_Q, C_K, C_V, C_GQ, C_GK, C_GV, C_GG, C_GA, C_END = 0, 512, 640, 768, 1024, 1280, 1792, 2304, 2432
GLA_PAD = 128
W_PITCH = 136
NEG_INF = float("-inf")


def _nn(a, b, precision=None):
    return jnp.dot(a, b, preferred_element_type=F32, precision=precision)


def _nt(a, b):
    return lax.dot_general(a, b, (((1,), (1,)), ((), ())), preferred_element_type=F32)


def _rms(x, gain):
    ms = jnp.mean(x * x, axis=-1, keepdims=True)
    return x * lax.rsqrt(ms + NORM_EPS) * gain


def _head_norm_rope(x, gain, ctab, stab):
    lane = lax.broadcasted_iota(jnp.int32, x.shape, 1)
    lo = lane < HEAD_DIM
    sq = x * x
    ms_lo = jnp.sum(jnp.where(lo, sq, 0.0), axis=-1, keepdims=True)
    ms_hi = jnp.sum(jnp.where(lo, 0.0, sq), axis=-1, keepdims=True)
    ms = jnp.where(lo, ms_lo, ms_hi) * (1.0 / HEAD_DIM)
    xn = x * lax.rsqrt(ms + NORM_EPS) * gain
    first = (lane & (HEAD_DIM - 1)) < (ROPE_DIM // 2)
    partner = jnp.where(first, pltpu.roll(xn, LANES - ROPE_DIM // 2, 1), pltpu.roll(xn, ROPE_DIM // 2, 1))
    return xn * ctab + partner * stab


def _project(x, ct, st, anorm, win, qkg, wgate, bgate):
    h = _rms(x, anorm).astype(BF16)
    p = _nn(h, win)
    qk = [
        _head_norm_rope(p[:, c * LANES:(c + 1) * LANES], qkg[:, c * LANES:(c + 1) * LANES], ct, st)
        for c in range(C_V // LANES)
    ]
    q = jnp.concatenate(qk[:4], axis=1) * (HEAD_DIM ** -0.5)
    k = qk[4]
    v = p[:, C_V:C_GQ]
    gq = p[:, C_GQ:C_GK] * (GLA_DK ** -0.5)
    gk = p[:, C_GK:C_GV]
    gv = p[:, C_GV:C_GG]
    gg = p[:, C_GG:C_GA]
    z = _nn(p[:, C_GA:C_END].astype(BF16), wgate) + bgate
    logf = (jnp.minimum(z, 0.0) - jnp.log(1.0 + jnp.exp(-jnp.abs(z)))) * (1.0 / GLA_GATE_NORM)
    return q, k, v, gq, gk, gv, gg, logf


def _swa_multi(qs, k2s, v2s, masks, sink_ref):
    n = len(qs)
    m = qs[0].shape[0]
    nk = k2s[0].shape[0]
    lo = lax.broadcasted_iota(jnp.int32, (nk, LANES), 1) < HEAD_DIM
    olane = lax.broadcasted_iota(jnp.int32, (2 * m, LANES), 1) < HEAD_DIM
    top = lax.broadcasted_iota(jnp.int32, (2 * m, 1), 0) < m
    zero = jnp.zeros((nk, LANES), F32)
    kexp, vexp = {}, {}
    for p in range(n):
        krot = pltpu.roll(k2s[p], HEAD_DIM, 1)
        vrot = pltpu.roll(v2s[p], HEAD_DIM, 1)
        for g in range(2):
            ka, kb = (k2s[p], krot) if g == 0 else (krot, k2s[p])
            va, vb = (v2s[p], vrot) if g == 0 else (vrot, v2s[p])
            kexp[p, g] = jnp.concatenate([jnp.where(lo, ka, zero), jnp.where(lo, zero, kb)], axis=0).astype(BF16)
            vexp[p, g] = jnp.concatenate([jnp.where(lo, va, zero), jnp.where(lo, zero, vb)], axis=0).astype(BF16)
    s = {}
    for p in range(n):
        for g in range(2):
            qq = jnp.concatenate([qs[p][:, (2 * g) * LANES:(2 * g + 1) * LANES],
                                  qs[p][:, (2 * g + 1) * LANES:(2 * g + 2) * LANES]], axis=0).astype(BF16)
            s[p, g] = _nt(qq, kexp[p, g])
    probs, scale = {}, {}
    for p in range(n):
        mask2 = jnp.concatenate([masks[p], masks[p]], axis=0)
        for g in range(2):
            ps, rs = [], []
            for hh in range(2):
                sh = jnp.where(mask2, s[p, g][:, hh * nk:(hh + 1) * nk], NEG_INF)
                ha, hb = 4 * g + hh, 4 * g + 2 + hh
                sink = jnp.where(top, sink_ref[ha:ha + 1, 0:1], sink_ref[hb:hb + 1, 0:1])
                mx = jnp.maximum(jnp.max(sh, axis=-1, keepdims=True), sink)
                pe = jnp.exp(sh - mx)
                den = jnp.sum(pe, axis=-1, keepdims=True) + jnp.exp(sink - mx)
                ps.append(pe)
                rs.append(1.0 / den)
            probs[p, g] = jnp.concatenate(ps, axis=1).astype(BF16)
            scale[p, g] = jnp.where(olane, rs[0], rs[1])
    outs = []
    for p in range(n):
        o = [_nn(probs[p, g], vexp[p, g]) * scale[p, g] for g in range(2)]
        outs.append(jnp.concatenate([o[0][:m], o[0][m:], o[1][:m], o[1][m:]], axis=1))
    return outs


def _zpad(a, rows):
    if a.shape[0] == rows:
        return a
    return jnp.concatenate([a, jnp.zeros((rows - a.shape[0], a.shape[1]), a.dtype)], axis=0)


def _gla_intra_multi(qs, ks, vs, fs, sub):
    n = len(qs)
    c = qs[0].shape[0]
    nsub = c // sub
    shift = sub.bit_length() - 1
    r = lax.broadcasted_iota(jnp.int32, (c, GLA_PAD), 0)
    cc = lax.broadcasted_iota(jnp.int32, (c, GLA_PAD), 1)
    causal = cc <= r
    tri2 = jnp.concatenate([causal, causal & (cc >= ((r >> shift) << shift))], axis=0).astype(BF16)
    lane_k = lax.broadcasted_iota(jnp.int32, (c, GK), 1) >> 6
    lane_v = lax.broadcasted_iota(jnp.int32, (c, GV), 1) >> 7
    rows = lax.broadcasted_iota(jnp.int32, (c, GK), 0)
    acol = lax.broadcasted_iota(jnp.int32, (c, GLA_HEADS * GLA_PAD), 1) & (GLA_PAD - 1)
    arow = lax.broadcasted_iota(jnp.int32, (c, GLA_HEADS * GLA_PAD), 0)
    amask = (acol <= arow) & (acol >= ((arow >> shift) << shift))

    def expand_k(kt):
        return jnp.concatenate(
            [_zpad(jnp.where(lane_k == h, kt, 0.0), GLA_PAD) for h in range(GLA_HEADS)], axis=0).astype(BF16)

    b, bl, bcol = [], [], []
    for p in range(n):
        fp = _zpad(fs[p], GLA_PAD)
        f_hi = fp.astype(BF16)
        r1 = fp - f_hi.astype(F32)
        f_mid = r1.astype(BF16)
        f_lo = (r1 - f_mid.astype(F32)).astype(BF16)
        cs = _nn(tri2, jnp.concatenate([f_hi, f_mid, f_lo], axis=1))
        tot = (cs[:, :GK] + cs[:, GK:2 * GK]) + cs[:, 2 * GK:]
        b.append(tot[:c])
        bl.append(tot[c:])
        bcol.append(jnp.sum(jnp.transpose(fp), axis=1, keepdims=True))
    qd, kd, qo, ko, qg, kbt, vexp, vpad = [], [], [], [], [], [], [], []
    for p in range(n):
        q, k, v = qs[p], ks[p], vs[p]
        qd.append((q * jnp.exp(bl[p])).astype(BF16))
        kd.append(expand_k(k * jnp.exp(-bl[p])))
        if nsub > 1:
            qj, kj = [], []
            for j in range(nsub - 1):
                e = (j + 1) * sub
                bj = b[p][e - 1:e]
                qj.append(jnp.where(rows >= e, q * jnp.exp(jnp.minimum(b[p] - bj, 0.0)), 0.0).astype(BF16))
                kj.append(expand_k(jnp.where((rows >= e - sub) & (rows < e),
                                             k * jnp.exp(jnp.minimum(bj - b[p], 0.0)), 0.0)))
            qo.append(jnp.concatenate(qj, axis=1))
            ko.append(jnp.concatenate(kj, axis=1))
        qg.append((q * jnp.exp(b[p])).astype(BF16))
        kbt.append(jnp.transpose(_zpad(k * jnp.exp(b[p][c - 1:c] - b[p]), GLA_PAD)).astype(BF16))
        vexp.append(jnp.concatenate(
            [_zpad(jnp.where(lane_v == h, v, 0.0), GLA_PAD) for h in range(GLA_HEADS)], axis=0).astype(BF16))
        vpad.append(_zpad(v, GLA_PAD).astype(BF16))
    a = []
    for p in range(n):
        ap = jnp.where(amask, _nt(qd[p], kd[p]), 0.0)
        if nsub > 1:
            ap = ap + _nt(qo[p], ko[p])
        a.append(ap.astype(BF16))
    return [(_nn(a[p], vexp[p]), qg[p], kbt[p], vpad[p], bcol[p]) for p in range(n)]


def _gla_inter_multi(parts, states):
    srow = lax.broadcasted_iota(jnp.int32, (GK, GV), 0) >> 6
    scol = lax.broadcasted_iota(jnp.int32, (GK, GV), 1) >> 7
    diag = srow == scol
    outs = [o + _nn(qg, s.astype(BF16)) for (o, qg, _, _, _), s in zip(parts, states)]
    new = [s * jnp.exp(bcol) + jnp.where(diag, _nn(kbt, vp), 0.0) for (_, _, kbt, vp, bcol), s in zip(parts, states)]
    return outs, new


def _gla_out(o, gg, gnorm):
    outs = []
    for h in range(GLA_HEADS):
        oh = o[:, h * GLA_DV:(h + 1) * GLA_DV]
        ms = jnp.mean(oh * oh, axis=-1, keepdims=True)
        outs.append(oh * lax.rsqrt(ms + NORM_EPS))
    on = jnp.concatenate(outs, axis=1) * gnorm
    return on * (gg / (1.0 + jnp.exp(-gg)))


def _mixer_prompt_kernel(x_ref, ct_ref, st_ref, anorm_ref, win_ref, qkg_ref, sink_ref, wgate_ref, bgate_ref,
                         gnorm_ref, wout_ref, y_ref, klast_ref, vlast_ref, sfin_ref,
                         kprev_ref, vprev_ref, s_ref, *, tm, chunk, sub):
    i = pl.program_id(0)

    @pl.when(i == 0)
    def _():
        kprev_ref[...] = jnp.zeros_like(kprev_ref)
        vprev_ref[...] = jnp.zeros_like(vprev_ref)
        s_ref[...] = jnp.zeros_like(s_ref)

    x = x_ref[...]
    q, k, v, gq, gk, gv, gg, logf = _project(
        x, ct_ref[...], st_ref[...], anorm_ref[...], win_ref[...], qkg_ref[...], wgate_ref[...], bgate_ref[...])

    kall = jnp.concatenate([kprev_ref[...], k], axis=0)
    vall = jnp.concatenate([vprev_ref[...], v], axis=0)
    qi = lax.broadcasted_iota(jnp.int32, (WINDOW, 2 * WINDOW), 0)
    kj = lax.broadcasted_iota(jnp.int32, (WINDOW, 2 * WINDOW), 1)
    band = (kj >= qi) & (kj <= qi + WINDOW)
    nb = tm // WINDOW
    masks = [band & (kj >= jnp.where(i == 0, WINDOW, 0)) if j == 0 else band for j in range(nb)]
    blocks = _swa_multi([q[j * WINDOW:(j + 1) * WINDOW] for j in range(nb)],
                        [kall[j * WINDOW:(j + 2) * WINDOW] for j in range(nb)],
                        [vall[j * WINDOW:(j + 2) * WINDOW] for j in range(nb)], masks, sink_ref)
    a_out = jnp.concatenate(blocks, axis=0)
    kprev_ref[...] = k[tm - WINDOW:]
    vprev_ref[...] = v[tm - WINDOW:]
    klast_ref[...] = k[tm - WINDOW:]
    vlast_ref[...] = v[tm - WINDOW:]

    sls = [slice(c * chunk, (c + 1) * chunk) for c in range(tm // chunk)]
    parts = _gla_intra_multi([gq[sl] for sl in sls], [gk[sl] for sl in sls], [gv[sl] for sl in sls],
                             [logf[sl] for sl in sls], sub)
    s = s_ref[...]
    os_ = []
    for part in parts:
        (o,), (s,) = _gla_inter_multi([part], [s])
        os_.append(o)
    s_ref[...] = s
    for h in range(GLA_HEADS):
        sfin_ref[h] = s[h * GLA_DK:(h + 1) * GLA_DK, h * GLA_DV:(h + 1) * GLA_DV]
    g_out = _gla_out(jnp.concatenate(os_, axis=0), gg, gnorm_ref[...])

    mix = jnp.concatenate([a_out, g_out], axis=1).astype(BF16)
    y_ref[...] = x + _nn(mix, wout_ref[...])


def _const_spec(shape):
    return pl.BlockSpec(shape, lambda *_: (0,) * len(shape))


def _mixer_prompt(x, ct, st, wts, *, tm=256, chunk=128, sub=32):
    t = x.shape[0]
    tm = min(tm, t)
    row = lambda w: pl.BlockSpec((tm, w), lambda i: (i, 0))
    kern = functools.partial(_mixer_prompt_kernel, tm=tm, chunk=chunk, sub=sub)
    return pl.pallas_call(
        kern,
        grid=(t // tm,),
        in_specs=[row(D_MODEL), row(LANES), row(LANES)] + [_const_spec(w.shape) for w in wts],
        out_specs=[row(D_MODEL), _const_spec((WINDOW, LANES)), _const_spec((WINDOW, LANES)),
                   _const_spec((GLA_HEADS, GLA_DK, GLA_DV))],
        out_shape=[jax.ShapeDtypeStruct((t, D_MODEL), F32), jax.ShapeDtypeStruct((WINDOW, LANES), F32),
                   jax.ShapeDtypeStruct((WINDOW, LANES), F32),
                   jax.ShapeDtypeStruct((GLA_HEADS, GLA_DK, GLA_DV), F32)],
        scratch_shapes=[pltpu.VMEM((WINDOW, LANES), F32), pltpu.VMEM((WINDOW, LANES), F32),
                        pltpu.VMEM((GK, GV), F32)],
        compiler_params=pltpu.CompilerParams(dimension_semantics=("arbitrary",),
                                             vmem_limit_bytes=48 * 1024 * 1024),
        name="mixer_prompt",
    )(x, ct, st, *wts)


def _mixer_sample_kernel(x_ref, ct_ref, st_ref, anorm_ref, win_ref, qkg_ref, sink_ref, wgate_ref, bgate_ref,
                         gnorm_ref, wout_ref, ck_ref, cv_ref, sin_ref,
                         y_ref, knew_ref, vnew_ref, sout_ref, *, sb, dec, par):
    x = x_ref[...]
    q, k, v, gq, gk, gv, gg, logf = _project(
        x, ct_ref[...], st_ref[...], anorm_ref[...], win_ref[...], qkg_ref[...], wgate_ref[...], bgate_ref[...])
    knew_ref[...] = k
    vnew_ref[...] = v

    per = 8 // dec
    qi = lax.broadcasted_iota(jnp.int32, (8, 2 * WINDOW), 0) & (dec - 1)
    kj = lax.broadcasted_iota(jnp.int32, (8, 2 * WINDOW), 1)
    mask = (kj >= qi) & (kj <= qi + WINDOW)
    ztail = jnp.zeros((WINDOW - dec, LANES), F32)
    pad8 = lambda a: jnp.concatenate([a, jnp.zeros((8 - dec, a.shape[1]), F32)], axis=0)

    a_parts, o_parts = [], []
    for b0 in range(0, sb, par):
        seqs = range(b0, b0 + par)
        rows8 = [slice((b // per) * 8, (b // per) * 8 + 8) for b in seqs]
        own = [slice((b % per) * dec, (b % per) * dec + dec) for b in seqs]
        new = [slice(b * dec, (b + 1) * dec) for b in seqs]
        att = _swa_multi([q[r] for r in rows8],
                         [jnp.concatenate([ck_ref[b], k[nw], ztail], axis=0) for b, nw in zip(seqs, new)],
                         [jnp.concatenate([cv_ref[b], v[nw], ztail], axis=0) for b, nw in zip(seqs, new)],
                         [mask] * par, sink_ref)
        a_parts += [a[o] for a, o in zip(att, own)]
        states = []
        for b in seqs:
            st = sin_ref[b]
            srows = []
            for h in range(GLA_HEADS):
                parts = []
                if h > 0:
                    parts.append(jnp.zeros((GLA_DK, h * GLA_DV), F32))
                parts.append(st[h])
                if h < GLA_HEADS - 1:
                    parts.append(jnp.zeros((GLA_DK, (GLA_HEADS - 1 - h) * GLA_DV), F32))
                srows.append(jnp.concatenate(parts, axis=1))
            states.append(jnp.concatenate(srows, axis=0))
        intra = _gla_intra_multi([pad8(gq[nw]) for nw in new], [pad8(gk[nw]) for nw in new],
                                 [pad8(gv[nw]) for nw in new], [pad8(logf[nw]) for nw in new], 8)
        outs, states = _gla_inter_multi(intra, states)
        o_parts += [o[:dec] for o in outs]
        for b, s_new in zip(seqs, states):
            for h in range(GLA_HEADS):
                sout_ref[b, h] = s_new[h * GLA_DK:(h + 1) * GLA_DK, h * GLA_DV:(h + 1) * GLA_DV]

    g_out = _gla_out(jnp.concatenate(o_parts, axis=0), gg, gnorm_ref[...])
    mix = jnp.concatenate([jnp.concatenate(a_parts, axis=0), g_out], axis=1).astype(BF16)
    y_ref[...] = x + _nn(mix, wout_ref[...])


def _mixer_sample(x, ct, st, wts, ck, cv, s0, *, dec, sb=16, par=8):
    n = x.shape[0]
    nseq = n // dec
    m = sb * dec
    row = lambda w: pl.BlockSpec((m, w), lambda i: (i, 0))
    seq3 = pl.BlockSpec((sb, WINDOW, LANES), lambda i: (i, 0, 0))
    seq4 = pl.BlockSpec((sb, GLA_HEADS, GLA_DK, GLA_DV), lambda i: (i, 0, 0, 0))
    kern = functools.partial(_mixer_sample_kernel, sb=sb, dec=dec, par=par)
    return pl.pallas_call(
        kern,
        grid=(nseq // sb,),
        in_specs=[row(D_MODEL), row(LANES), row(LANES)] + [_const_spec(w.shape) for w in wts] + [seq3, seq3, seq4],
        out_specs=[row(D_MODEL), row(LANES), row(LANES), seq4],
        out_shape=[jax.ShapeDtypeStruct((n, D_MODEL), F32), jax.ShapeDtypeStruct((n, LANES), F32),
                   jax.ShapeDtypeStruct((n, LANES), F32), jax.ShapeDtypeStruct(s0.shape, F32)],
        compiler_params=pltpu.CompilerParams(dimension_semantics=("arbitrary",),
                                             vmem_limit_bytes=48 * 1024 * 1024),
        name="mixer_sample",
    )(x, ct, st, *wts, ck, cv, s0)


def _shift_kernel(ck_ref, cv_ref, nk_ref, nv_ref, ok_ref, ov_ref, *, new):
    keep = ck_ref.shape[1] - new
    ok_ref[:, :keep] = ck_ref[:, new:]
    ok_ref[:, keep:] = nk_ref[...]
    ov_ref[:, :keep] = cv_ref[:, new:]
    ov_ref[:, keep:] = nv_ref[...]


def _shift_cache(ck, cv, nk, nv, *, sb=32):
    nseq, width = ck.shape
    new = nk.shape[1]
    big = pl.BlockSpec((sb, width), lambda i: (i, 0))
    small = pl.BlockSpec((sb, new), lambda i: (i, 0))
    return pl.pallas_call(
        functools.partial(_shift_kernel, new=new),
        grid=(nseq // sb,),
        in_specs=[big, big, small, small],
        out_specs=[big, big],
        out_shape=[jax.ShapeDtypeStruct(ck.shape, F32), jax.ShapeDtypeStruct(cv.shape, F32)],
        compiler_params=pltpu.CompilerParams(dimension_semantics=("arbitrary",)),
        name="shift_cache",
    )(ck, cv, nk, nv)


def _sort_network(n):
    pairs = []

    def merge(lo, m, r):
        step = 2 * r
        if step < m:
            merge(lo, m, step)
            merge(lo + r, m, step)
            pairs.extend((i, i + r) for i in range(lo + r, lo + m - r, step))
        else:
            pairs.append((lo, lo + r))

    def sort(lo, m):
        if m > 1:
            sort(lo, m // 2)
            sort(lo + m // 2, m // 2)
            merge(lo, m, 1)

    sort(0, n)
    return pairs


def _top16(s):
    t = s.shape[1]
    nv = PEER_NKEYS // 8
    sub8 = lax.broadcasted_iota(jnp.int32, (8, t), 0).astype(F32)
    v = [s[8 * k:8 * k + 8] for k in range(nv)]
    ix = [sub8 + float(8 * k) for k in range(nv)]
    for p, q in _sort_network(nv):
        swap = (v[q] > v[p]) | ((v[q] == v[p]) & (ix[q] < ix[p]))
        v[p], v[q] = jnp.where(swap, v[q], v[p]), jnp.where(swap, v[p], v[q])
        ix[p], ix[q] = jnp.where(swap, ix[q], ix[p]), jnp.where(swap, ix[p], ix[q])
    vals, idxs = [], []
    for n in range(PEER_TOPK):
        m = jnp.max(v[0], axis=0, keepdims=True)
        imin = jnp.min(jnp.where(v[0] == m, ix[0], float(PEER_NKEYS)), axis=0, keepdims=True)
        hit = ix[0] == imin
        vals.append(m)
        idxs.append(imin)
        for k in range(PEER_TOPK - 1 - n):
            v[k] = jnp.where(hit, v[k + 1], v[k])
            ix[k] = jnp.where(hit, ix[k + 1], ix[k])
    return jnp.concatenate(vals, axis=0), jnp.concatenate(idxs, axis=0)


def _pair_top16(a, ia, b, ib):
    t = a.shape[1]
    sub8 = lax.broadcasted_iota(jnp.int32, (8, t), 0).astype(F32)
    ia_lo = ia[0:8] * float(PEER_NKEYS)
    lv, le = [], []
    for y in range(PEER_TOPK):
        nx = PEER_TOPK // (y + 1)
        val = a[0:8] + b[y:y + 1]
        lv.append(val if nx >= 8 else jnp.where(sub8 < float(nx), val, NEG_INF))
        le.append(ia_lo + ib[y:y + 1])
    hv = a[8:16] + b[0:1]
    he = ia[8:16] * float(PEER_NKEYS) + ib[0:1]
    code_lo = sub8 * float(PEER_TOPK)
    code_hi = (sub8 + 8.0) * float(PEER_TOPK)
    vals, sel = [], []
    for n in range(PEER_TOPK):
        m = jnp.max(jnp.maximum(lv[0], hv), axis=0, keepdims=True)
        cmin = jnp.min(jnp.minimum(jnp.where(lv[0] == m, code_lo, 1e9), jnp.where(hv == m, code_hi, 1e9)),
                       axis=0, keepdims=True)
        hit_lo = code_lo == cmin
        hit_hi = code_hi == cmin
        sel.append(jnp.max(jnp.maximum(jnp.where(hit_lo, le[0], -1.0), jnp.where(hit_hi, he, -1.0)),
                           axis=0, keepdims=True))
        vals.append(m)
        for y in range(PEER_TOPK - 1 - n):
            lv[y] = jnp.where(hit_lo, lv[y + 1], lv[y])
            le[y] = jnp.where(hit_lo, le[y + 1], le[y])
        code_lo = jnp.where(hit_lo, code_lo + 1.0, code_lo)
        hv = jnp.where(hit_hi, NEG_INF, hv)
    return jnp.concatenate(vals, axis=0), jnp.concatenate(sel, axis=0)


def _peer_route_kernel(x_ref, fnorm_ref, wqt_ref, keys_ref, h_ref, i_ref, j_ref, g_ref, qt_s, e_s, w_s):
    hb = _rms(x_ref[...], fnorm_ref[...]).astype(BF16)
    h_ref[...] = hb
    qt_s[...] = _nt(wqt_ref[...], hb)

    def head(hh, carry):
        r0 = pl.multiple_of(hh * 2 * PEER_NKEYS, 2 * PEER_NKEYS)
        s0 = _nn(keys_ref[2 * hh], qt_s[pl.ds(r0, PEER_NKEYS), :].astype(BF16))
        s1 = _nn(keys_ref[2 * hh + 1], qt_s[pl.ds(r0 + PEER_NKEYS, PEER_NKEYS), :].astype(BF16))
        a, ia = _top16(s0)
        b, ib = _top16(s1)
        c, e = _pair_top16(a, ia, b, ib)
        p = jnp.exp(c - c[0:1])
        rows = pl.ds(pl.multiple_of(hh * PEER_TOPK, PEER_TOPK), PEER_TOPK)
        e_s[rows, :] = e
        w_s[rows, :] = p / jnp.sum(p, axis=0, keepdims=True)
        return carry

    lax.fori_loop(0, PEER_HEADS, head, 0)
    et = jnp.transpose(e_s[...])
    it = jnp.floor(et * (1.0 / PEER_NKEYS))
    i_ref[...] = it
    j_ref[...] = et - it * float(PEER_NKEYS)
    g_ref[...] = jnp.transpose(w_s[...])


def _peer_route(x, fnorm, wqt, keys, *, tm=512):
    t = x.shape[0]
    tm = min(tm, t)
    nsel = PEER_HEADS * PEER_TOPK
    row = lambda w: pl.BlockSpec((tm, w), lambda i: (i, 0))
    return pl.pallas_call(
        _peer_route_kernel,
        grid=(t // tm,),
        in_specs=[row(D_MODEL), _const_spec(fnorm.shape), _const_spec(wqt.shape), _const_spec(keys.shape)],
        out_specs=[row(D_MODEL), row(nsel), row(nsel), row(nsel)],
        out_shape=[jax.ShapeDtypeStruct((t, D_MODEL), BF16)] + [jax.ShapeDtypeStruct((t, nsel), F32)] * 3,
        scratch_shapes=[pltpu.VMEM((wqt.shape[0], tm), F32), pltpu.VMEM((nsel, tm), F32),
                        pltpu.VMEM((nsel, tm), F32)],
        compiler_params=pltpu.CompilerParams(dimension_semantics=("arbitrary",),
                                             vmem_limit_bytes=48 * 1024 * 1024),
        name="peer_route",
    )(x, fnorm, wqt, keys)


def _peer_gates_kernel(i_ref, j_ref, g_ref, w_ref, t_s, *, tm, group):
    sub = lax.broadcasted_iota(jnp.int32, (PEER_NKEYS, LANES), 0).astype(BF16)
    one = jnp.ones((PEER_NKEYS, LANES), BF16)
    zero = jnp.zeros((PEER_NKEYS, LANES), BF16)

    def body(g, carry):
        n0 = pl.multiple_of(g * group, group)
        for t in range(group):
            ri = i_ref[pl.ds(n0 + t, 1), :].astype(BF16)
            rj = j_ref[pl.ds(n0 + t, 1), :].astype(BF16)
            rg = jnp.broadcast_to(g_ref[pl.ds(n0 + t, 1), :].astype(BF16), sub.shape)
            at = jnp.where(sub == ri, one, zero)
            bt = jnp.where(sub == rj, rg, zero)
            t_s[t * W_PITCH:t * W_PITCH + PEER_NKEYS, :] = _nt(at, bt)
        for q in range(group // 16):
            for i in range(PEER_NKEYS):
                a = t_s[pl.ds((16 * q) * W_PITCH + i, 8, stride=W_PITCH), :]
                b = t_s[pl.ds((16 * q + 8) * W_PITCH + i, 8, stride=W_PITCH), :]
                w_ref[pl.ds(n0 + 16 * q, 16), i * PEER_NKEYS:(i + 1) * PEER_NKEYS] = (
                    jnp.concatenate([a, b], axis=0).astype(BF16))
        return carry

    lax.fori_loop(0, tm // group, body, 0)


def _peer_gates(ii, jj, gg, *, tm=256, group=32):
    t = ii.shape[0]
    tm = min(tm, t)
    n_exp = PEER_NKEYS * PEER_NKEYS
    tok = pl.BlockSpec((tm, LANES), lambda i: (i, 0))
    return pl.pallas_call(
        functools.partial(_peer_gates_kernel, tm=tm, group=group),
        grid=(t // tm,),
        in_specs=[tok, tok, tok],
        out_specs=pl.BlockSpec((tm, n_exp), lambda i: (i, 0)),
        out_shape=jax.ShapeDtypeStruct((t, n_exp), BF16),
        scratch_shapes=[pltpu.VMEM((group * W_PITCH, LANES), F32)],
        compiler_params=pltpu.CompilerParams(dimension_semantics=("arbitrary",),
                                             vmem_limit_bytes=48 * 1024 * 1024),
        name="peer_gates",
    )(ii, jj, gg)


def _peer_dense_kernel(h_ref, x_ref, w_ref, u_ref, v_ref, y_ref):
    @pl.when(pl.program_id(1) == 0)
    def _():
        y_ref[...] = x_ref[...]

    act = _nt(h_ref[...], u_ref[...])
    gelu = 0.5 * act * (1.0 + lax.erf(act * (2.0 ** -0.5)))
    z = (w_ref[...].astype(F32) * gelu).astype(BF16)
    y_ref[...] += _nn(z, v_ref[...])


def _peer_dense(h, x, w, u, v, *, tm=1024, te=1024):
    t = h.shape[0]
    tm = min(tm, t)
    n_exp = u.shape[0]
    tok = pl.BlockSpec((tm, D_MODEL), lambda i, e: (i, 0))
    exp = pl.BlockSpec((te, D_MODEL), lambda i, e: (e, 0))
    return pl.pallas_call(
        _peer_dense_kernel,
        grid=(t // tm, n_exp // te),
        in_specs=[tok, tok, pl.BlockSpec((tm, te), lambda i, e: (i, e)), exp, exp],
        out_specs=tok,
        out_shape=jax.ShapeDtypeStruct((t, D_MODEL), F32),
        compiler_params=pltpu.CompilerParams(dimension_semantics=("arbitrary", "arbitrary"),
                                             vmem_limit_bytes=56 * 1024 * 1024),
        name="peer_dense",
    )(h, x, w, u, v)


def _cast_kernel(x_ref, o_ref):
    o_ref[...] = x_ref[...].astype(o_ref.dtype)


def _to_bf16(x, *, rows=1024):
    spec = pl.BlockSpec((rows, x.shape[1]), lambda i: (i, 0))
    return pl.pallas_call(
        _cast_kernel, grid=(x.shape[0] // rows,), in_specs=[spec], out_specs=spec,
        out_shape=jax.ShapeDtypeStruct(x.shape, BF16),
        compiler_params=pltpu.CompilerParams(dimension_semantics=("arbitrary",)),
        name="cast_bf16",
    )(x)


def _rope_tables(pos):
    half = ROPE_DIM // 2
    inv_freq = ROPE_THETA ** (-jnp.arange(half, dtype=F32) / half)
    ang = pos.astype(F32)[:, None] * inv_freq[None, :]
    cos, sin = jnp.cos(ang), jnp.sin(ang)
    n = pos.shape[0]
    c64 = jnp.concatenate([cos, cos, jnp.ones((n, HEAD_DIM - ROPE_DIM), F32)], axis=1)
    s64 = jnp.concatenate([-sin, sin, jnp.zeros((n, HEAD_DIM - ROPE_DIM), F32)], axis=1)
    return jnp.tile(c64, (1, 2)), jnp.tile(s64, (1, 2))


def kernel(x_prompt, x_sample, cache_swa_k, cache_swa_v, state_gla, attn_norm, w_in, q_norm, k_norm, attn_sinks,
           w_gate, b_gate, gla_norm, w_out, ffn_norm, peer_wq, peer_keys, peer_u, peer_v):
    depth = w_in.shape[0]
    assert depth == 1 and x_prompt.shape[0] == 1
    seq = x_prompt.shape[1]
    nseq, dec = x_sample.shape[0], x_sample.shape[1]
    xp = x_prompt[0]
    xs = x_sample.reshape(nseq * dec, D_MODEL)
    l = 0

    win = jnp.pad(w_in[l], ((0, 0), (0, C_END - w_in.shape[2]))).astype(BF16)
    qkg = jnp.concatenate([jnp.tile(q_norm[l], SWA_Q_HEADS), jnp.tile(k_norm[l], 2)])[None, :]
    sinks = jnp.broadcast_to(attn_sinks[l][:, None], (SWA_Q_HEADS, LANES))
    wgate = jnp.pad(w_gate[l], ((0, LANES - GLA_LOWRANK), (0, 0))).astype(BF16)
    wts = (attn_norm[l][None, :], win, qkg, sinks, wgate, b_gate[l][None, :],
           jnp.tile(gla_norm[l], GLA_HEADS)[None, :], w_out[l].astype(BF16))
    fnorm = ffn_norm[l][None, :]
    wqt = jnp.transpose(peer_wq[l]).astype(BF16)
    keys = peer_keys[l].reshape(PEER_HEADS * 2, PEER_NKEYS, peer_keys.shape[-1]).astype(BF16)
    u_b = _to_bf16(peer_u[l])
    v_b = _to_bf16(peer_v[l])

    ct_p, st_p = _rope_tables(jnp.arange(seq, dtype=jnp.int32))
    ct_s, st_s = _rope_tables(PAST_LEN + jnp.arange(nseq * dec, dtype=jnp.int32) % dec)

    xp2, kp, vp, sp = _mixer_prompt(xp, ct_p, st_p, wts)
    hp, ip, jp, gp = _peer_route(xp2, fnorm, wqt, keys)
    yp = _peer_dense(hp, xp2, _peer_gates(ip, jp, gp), u_b, v_b)

    ck = cache_swa_k[l].reshape(nseq, WINDOW, LANES)
    cv = cache_swa_v[l].reshape(nseq, WINDOW, LANES)
    xs2, nk, nv, ss = _mixer_sample(xs, ct_s, st_s, wts, ck, cv, state_gla[l], dec=dec)
    ck2, cv2 = _shift_cache(ck.reshape(nseq, WINDOW * LANES), cv.reshape(nseq, WINDOW * LANES),
                            nk.reshape(nseq, dec * LANES), nv.reshape(nseq, dec * LANES))
    hs, is_, js, gs = _peer_route(xs2, fnorm, wqt, keys)
    ys = _peer_dense(hs, xs2, _peer_gates(is_, js, gs), u_b, v_b)

    kv_shape = (1, 1, WINDOW, 2, HEAD_DIM)
    return (yp[None], ys.reshape(nseq, dec, D_MODEL),
            kp.reshape(kv_shape), vp.reshape(kv_shape), sp[None, None],
            ck2.reshape(1, nseq, WINDOW, 2, HEAD_DIM), cv2.reshape(1, nseq, WINDOW, 2, HEAD_DIM), ss[None])
```

```python
import functools

import jax
import jax.numpy as jnp
from jax import lax
from jax.experimental import pallas as pl
from jax.experimental.pallas import tpu as pltpu

F32 = jnp.float32
BF16 = jnp.bfloat16

D_MODEL = 1024
HEAD_DIM = 64
SWA_Q_HEADS = 8
WINDOW = 128
ROPE_THETA = 500000.0
ROPE_DIM = 16
PAST_LEN = 16384
GLA_HEADS = 4
GLA_DK = 64
GLA_DV = 128
GLA_LOWRANK = 16
GLA_GATE_NORM = 16.0
PEER_HEADS = 8
PEER_NKEYS = 128
PEER_TOPK = 16
NORM_EPS = 1e-6

LANES = 128
QW = SWA_Q_HEADS * HEAD_DIM
GK = GLA_HEADS * GLA_DK
GV = GLA_HEADS * GLA_DV
C_Q, C_K, C_V, C_GQ, C_GK, C_GV, C_GG, C_GA, C_END = 0, 512, 640, 768, 1024, 1280, 1792, 2304, 2432
GLA_PAD = 128
W_PITCH = 136
NEG_INF = float("-inf")


def _nn(a, b, precision=None):
    return jnp.dot(a, b, preferred_element_type=F32, precision=precision)


def _nt(a, b):
    return lax.dot_general(a, b, (((1,), (1,)), ((), ())), preferred_element_type=F32)


def _rms(x, gain):
    ms = jnp.mean(x * x, axis=-1, keepdims=True)
    return x * lax.rsqrt(ms + NORM_EPS) * gain


def _head_norm_rope(x, gain, ctab, stab):
    lane = lax.broadcasted_iota(jnp.int32, x.shape, 1)
    lo = lane < HEAD_DIM
    sq = x * x
    ms_lo = jnp.sum(jnp.where(lo, sq, 0.0), axis=-1, keepdims=True)
    ms_hi = jnp.sum(jnp.where(lo, 0.0, sq), axis=-1, keepdims=True)
    ms = jnp.where(lo, ms_lo, ms_hi) * (1.0 / HEAD_DIM)
    xn = x * lax.rsqrt(ms + NORM_EPS) * gain
    first = (lane & (HEAD_DIM - 1)) < (ROPE_DIM // 2)
    partner = jnp.where(first, pltpu.roll(xn, LANES - ROPE_DIM // 2, 1), pltpu.roll(xn, ROPE_DIM // 2, 1))
    return xn * ctab + partner * stab


def _project(x, ct, st, anorm, win, qkg, wgate, bgate):
    h = _rms(x, anorm).astype(BF16)
    p = _nn(h, win)
    qk = [
        _head_norm_rope(p[:, c * LANES:(c + 1) * LANES], qkg[:, c * LANES:(c + 1) * LANES], ct, st)
        for c in range(C_V // LANES)
    ]
    q = jnp.concatenate(qk[:4], axis=1) * (HEAD_DIM ** -0.5)
    k = qk[4]
    v = p[:, C_V:C_GQ]
    gq = p[:, C_GQ:C_GK] * (GLA_DK ** -0.5)
    gk = p[:, C_GK:C_GV]
    gv = p[:, C_GV:C_GG]
    gg = p[:, C_GG:C_GA]
    z = _nn(p[:, C_GA:C_END].astype(BF16), wgate) + bgate
    logf = (jnp.minimum(z, 0.0) - jnp.log(1.0 + jnp.exp(-jnp.abs(z)))) * (1.0 / GLA_GATE_NORM)
    return q, k, v, gq, gk, gv, gg, logf


def _swa_multi(qs, k2s, v2s, masks, sink_ref):
    n = len(qs)
    m = qs[0].shape[0]
    nk = k2s[0].shape[0]
    lo = lax.broadcasted_iota(jnp.int32, (nk, LANES), 1) < HEAD_DIM
    olane = lax.broadcasted_iota(jnp.int32, (2 * m, LANES), 1) < HEAD_DIM
    top = lax.broadcasted_iota(jnp.int32, (2 * m, 1), 0) < m
    zero = jnp.zeros((nk, LANES), F32)
    kexp, vexp = {}, {}
    for p in range(n):
        krot = pltpu.roll(k2s[p], HEAD_DIM, 1)
        vrot = pltpu.roll(v2s[p], HEAD_DIM, 1)
        for g in range(2):
            ka, kb = (k2s[p], krot) if g == 0 else (krot, k2s[p])
            va, vb = (v2s[p], vrot) if g == 0 else (vrot, v2s[p])
            kexp[p, g] = jnp.concatenate([jnp.where(lo, ka, zero), jnp.where(lo, zero, kb)], axis=0).astype(BF16)
            vexp[p, g] = jnp.concatenate([jnp.where(lo, va, zero), jnp.where(lo, zero, vb)], axis=0).astype(BF16)
    s = {}
    for p in range(n):
        for g in range(2):
            qq = jnp.concatenate([qs[p][:, (2 * g) * LANES:(2 * g + 1) * LANES],
                                  qs[p][:, (2 * g + 1) * LANES:(2 * g + 2) * LANES]], axis=0).astype(BF16)
            s[p, g] = _nt(qq, kexp[p, g])
    probs, scale = {}, {}
    for p in range(n):
        mask2 = jnp.concatenate([masks[p], masks[p]], axis=0)
        for g in range(2):
            ps, rs = [], []
            for hh in range(2):
                sh = jnp.where(mask2, s[p, g][:, hh * nk:(hh + 1) * nk], NEG_INF)
                ha, hb = 4 * g + hh, 4 * g + 2 + hh
                sink = jnp.where(top, sink_ref[ha:ha + 1, 0:1], sink_ref[hb:hb + 1, 0:1])
                mx = jnp.maximum(jnp.max(sh, axis=-1, keepdims=True), sink)
                pe = jnp.exp(sh - mx)
                den = jnp.sum(pe, axis=-1, keepdims=True) + jnp.exp(sink - mx)
                ps.append(pe)
                rs.append(1.0 / den)
            probs[p, g] = jnp.concatenate(ps, axis=1).astype(BF16)
            scale[p, g] = jnp.where(olane, rs[0], rs[1])
    outs = []
    for p in range(n):
        o = [_nn(probs[p, g], vexp[p, g]) * scale[p, g] for g in range(2)]
        outs.append(jnp.concatenate([o[0][:m], o[0][m:], o[1][:m], o[1][m:]], axis=1))
    return outs


def _zpad(a, rows):
    if a.shape[0] == rows:
        return a
    return jnp.concatenate([a, jnp.zeros((rows - a.shape[0], a.shape[1]), a.dtype)], axis=0)


def _gla_intra_multi(qs, ks, vs, fs, sub):
    n = len(qs)
    c = qs[0].shape[0]
    nsub = c // sub
    shift = sub.bit_length() - 1
    r = lax.broadcasted_iota(jnp.int32, (c, GLA_PAD), 0)
    cc = lax.broadcasted_iota(jnp.int32, (c, GLA_PAD), 1)
    causal = cc <= r
    tri2 = jnp.concatenate([causal, causal & (cc >= ((r >> shift) << shift))], axis=0).astype(BF16)
    lane_k = lax.broadcasted_iota(jnp.int32, (c, GK), 1) >> 6
    lane_v = lax.broadcasted_iota(jnp.int32, (c, GV), 1) >> 7
    rows = lax.broadcasted_iota(jnp.int32, (c, GK), 0)
    acol = lax.broadcasted_iota(jnp.int32, (c, GLA_HEADS * GLA_PAD), 1) & (GLA_PAD - 1)
    arow = lax.broadcasted_iota(jnp.int32, (c, GLA_HEADS * GLA_PAD), 0)
    amask = (acol <= arow) & (acol >= ((arow >> shift) << shift))

    def expand_k(kt):
        return jnp.concatenate(
            [_zpad(jnp.where(lane_k == h, kt, 0.0), GLA_PAD) for h in range(GLA_HEADS)], axis=0).astype(BF16)

    b, bl, bcol = [], [], []
    for p in range(n):
        fp = _zpad(fs[p], GLA_PAD)
        f_hi = fp.astype(BF16)
        r1 = fp - f_hi.astype(F32)
        f_mid = r1.astype(BF16)
        f_lo = (r1 - f_mid.astype(F32)).astype(BF16)
        cs = _nn(tri2, jnp.concatenate([f_hi, f_mid, f_lo], axis=1))
        tot = (cs[:, :GK] + cs[:, GK:2 * GK]) + cs[:, 2 * GK:]
        b.append(tot[:c])
        bl.append(tot[c:])
        bcol.append(jnp.sum(jnp.transpose(fp), axis=1, keepdims=True))
    qd, kd, qo, ko, qg, kbt, vexp, vpad = [], [], [], [], [], [], [], []
    for p in range(n):
        q, k, v = qs[p], ks[p], vs[p]
        qd.append((q * jnp.exp(bl[p])).astype(BF16))
        kd.append(expand_k(k * jnp.exp(-bl[p])))
        if nsub > 1:
            qj, kj = [], []
            for j in range(nsub - 1):
                e = (j + 1) * sub
                bj = b[p][e - 1:e]
                qj.append(jnp.where(rows >= e, q * jnp.exp(jnp.minimum(b[p] - bj, 0.0)), 0.0).astype(BF16))
                kj.append(expand_k(jnp.where((rows >= e - sub) & (rows < e),
                                             k * jnp.exp(jnp.minimum(bj - b[p], 0.0)), 0.0)))
            qo.append(jnp.concatenate(qj, axis=1))
            ko.append(jnp.concatenate(kj, axis=1))
        qg.append((q * jnp.exp(b[p])).astype(BF16))
        kbt.append(jnp.transpose(_zpad(k * jnp.exp(b[p][c - 1:c] - b[p]), GLA_PAD)).astype(BF16))
        vexp.append(jnp.concatenate(
            [_zpad(jnp.where(lane_v == h, v, 0.0), GLA_PAD) for h in range(GLA_HEADS)], axis=0).astype(BF16))
        vpad.append(_zpad(v, GLA_PAD).astype(BF16))
    a = []
    for p in range(n):
        ap = jnp.where(amask, _nt(qd[p], kd[p]), 0.0)
        if nsub > 1:
            ap = ap + _nt(qo[p], ko[p])
        a.append(ap.astype(BF16))
    return [(_nn(a[p], vexp[p]), qg[p], kbt[p], vpad[p], bcol[p]) for p in range(n)]


def _gla_inter_multi(parts, states):
    srow = lax.broadcasted_iota(jnp.int32, (GK, GV), 0) >> 6
    scol = lax.broadcasted_iota(jnp.int32, (GK, GV), 1) >> 7
    diag = srow == scol
    outs = [o + _nn(qg, s.astype(BF16)) for (o, qg, _, _, _), s in zip(parts, states)]
    new = [s * jnp.exp(bcol) + jnp.where(diag, _nn(kbt, vp), 0.0) for (_, _, kbt, vp, bcol), s in zip(parts, states)]
    return outs, new


def _gla_out(o, gg, gnorm):
    outs = []
    for h in range(GLA_HEADS):
        oh = o[:, h * GLA_DV:(h + 1) * GLA_DV]
        ms = jnp.mean(oh * oh, axis=-1, keepdims=True)
        outs.append(oh * lax.rsqrt(ms + NORM_EPS))
    on = jnp.concatenate(outs, axis=1) * gnorm
    return on * (gg / (1.0 + jnp.exp(-gg)))


def _mixer_prompt_kernel(x_ref, ct_ref, st_ref, anorm_ref, win_ref, qkg_ref, sink_ref, wgate_ref, bgate_ref,
                         gnorm_ref, wout_ref, y_ref, klast_ref, vlast_ref, sfin_ref,
                         kprev_ref, vprev_ref, s_ref, *, tm, chunk, sub):
    i = pl.program_id(0)

    @pl.when(i == 0)
    def _():
        kprev_ref[...] = jnp.zeros_like(kprev_ref)
        vprev_ref[...] = jnp.zeros_like(vprev_ref)
        s_ref[...] = jnp.zeros_like(s_ref)

    x = x_ref[...]
    q, k, v, gq, gk, gv, gg, logf = _project(
        x, ct_ref[...], st_ref[...], anorm_ref[...], win_ref[...], qkg_ref[...], wgate_ref[...], bgate_ref[...])

    kall = jnp.concatenate([kprev_ref[...], k], axis=0)
    vall = jnp.concatenate([vprev_ref[...], v], axis=0)
    qi = lax.broadcasted_iota(jnp.int32, (WINDOW, 2 * WINDOW), 0)
    kj = lax.broadcasted_iota(jnp.int32, (WINDOW, 2 * WINDOW), 1)
    band = (kj >= qi) & (kj <= qi + WINDOW)
    nb = tm // WINDOW
    masks = [band & (kj >= jnp.where(i == 0, WINDOW, 0)) if j == 0 else band for j in range(nb)]
    blocks = _swa_multi([q[j * WINDOW:(j + 1) * WINDOW] for j in range(nb)],
                        [kall[j * WINDOW:(j + 2) * WINDOW] for j in range(nb)],
                        [vall[j * WINDOW:(j + 2) * WINDOW] for j in range(nb)], masks, sink_ref)
    a_out = jnp.concatenate(blocks, axis=0)
    kprev_ref[...] = k[tm - WINDOW:]
    vprev_ref[...] = v[tm - WINDOW:]
    klast_ref[...] = k[tm - WINDOW:]
    vlast_ref[...] = v[tm - WINDOW:]

    sls = [slice(c * chunk, (c + 1) * chunk) for c in range(tm // chunk)]
    parts = _gla_intra_multi([gq[sl] for sl in sls], [gk[sl] for sl in sls], [gv[sl] for sl in sls],
                             [logf[sl] for sl in sls], sub)
    s = s_ref[...]
    os_ = []
    for part in parts:
        (o,), (s,) = _gla_inter_multi([part], [s])
        os_.append(o)
    s_ref[...] = s
    for h in range(GLA_HEADS):
        sfin_ref[h] = s[h * GLA_DK:(h + 1) * GLA_DK, h * GLA_DV:(h + 1) * GLA_DV]
    g_out = _gla_out(jnp.concatenate(os_, axis=0), gg, gnorm_ref[...])

    mix = jnp.concatenate([a_out, g_out], axis=1).astype(BF16)
    y_ref[...] = x + _nn(mix, wout_ref[...])


def _const_spec(shape):
    return pl.BlockSpec(shape, lambda *_: (0,) * len(shape))


def _mixer_prompt(x, ct, st, wts, *, tm=256, chunk=128, sub=32):
    t = x.shape[0]
    tm = min(tm, t)
    row = lambda w: pl.BlockSpec((tm, w), lambda i: (i, 0))
    kern = functools.partial(_mixer_prompt_kernel, tm=tm, chunk=chunk, sub=sub)
    return pl.pallas_call(
        kern,
        grid=(t // tm,),
        in_specs=[row(D_MODEL), row(LANES), row(LANES)] + [_const_spec(w.shape) for w in wts],
        out_specs=[row(D_MODEL), _const_spec((WINDOW, LANES)), _const_spec((WINDOW, LANES)),
                   _const_spec((GLA_HEADS, GLA_DK, GLA_DV))],
        out_shape=[jax.ShapeDtypeStruct((t, D_MODEL), F32), jax.ShapeDtypeStruct((WINDOW, LANES), F32),
                   jax.ShapeDtypeStruct((WINDOW, LANES), F32),
                   jax.ShapeDtypeStruct((GLA_HEADS, GLA_DK, GLA_DV), F32)],
        scratch_shapes=[pltpu.VMEM((WINDOW, LANES), F32), pltpu.VMEM((WINDOW, LANES), F32),
                        pltpu.VMEM((GK, GV), F32)],
        compiler_params=pltpu.CompilerParams(dimension_semantics=("arbitrary",),
                                             vmem_limit_bytes=48 * 1024 * 1024),
        name="mixer_prompt",
    )(x, ct, st, *wts)


def _mixer_sample_kernel(x_ref, ct_ref, st_ref, anorm_ref, win_ref, qkg_ref, sink_ref, wgate_ref, bgate_ref,
                         gnorm_ref, wout_ref, ck_ref, cv_ref, sin_ref,
                         y_ref, knew_ref, vnew_ref, sout_ref, *, sb, dec, par):
    x = x_ref[...]
    q, k, v, gq, gk, gv, gg, logf = _project(
        x, ct_ref[...], st_ref[...], anorm_ref[...], win_ref[...], qkg_ref[...], wgate_ref[...], bgate_ref[...])
    knew_ref[...] = k
    vnew_ref[...] = v

    per = 8 // dec
    qi = lax.broadcasted_iota(jnp.int32, (8, 2 * WINDOW), 0) & (dec - 1)
    kj = lax.broadcasted_iota(jnp.int32, (8, 2 * WINDOW), 1)
    mask = (kj >= qi) & (kj <= qi + WINDOW)
    ztail = jnp.zeros((WINDOW - dec, LANES), F32)
    pad8 = lambda a: jnp.concatenate([a, jnp.zeros((8 - dec, a.shape[1]), F32)], axis=0)

    a_parts, o_parts = [], []
    for b0 in range(0, sb, par):
        seqs = range(b0, b0 + par)
        rows8 = [slice((b // per) * 8, (b // per) * 8 + 8) for b in seqs]
        own = [slice((b % per) * dec, (b % per) * dec + dec) for b in seqs]
        new = [slice(b * dec, (b + 1) * dec) for b in seqs]
        att = _swa_multi([q[r] for r in rows8],
                         [jnp.concatenate([ck_ref[b], k[nw], ztail], axis=0) for b, nw in zip(seqs, new)],
                         [jnp.concatenate([cv_ref[b], v[nw], ztail], axis=0) for b, nw in zip(seqs, new)],
                         [mask] * par, sink_ref)
        a_parts += [a[o] for a, o in zip(att, own)]
        states = []
        for b in seqs:
            st = sin_ref[b]
            srows = []
            for h in range(GLA_HEADS):
                parts = []
                if h > 0:
                    parts.append(jnp.zeros((GLA_DK, h * GLA_DV), F32))
                parts.append(st[h])
                if h < GLA_HEADS - 1:
                    parts.append(jnp.zeros((GLA_DK, (GLA_HEADS - 1 - h) * GLA_DV), F32))
                srows.append(jnp.concatenate(parts, axis=1))
            states.append(jnp.concatenate(srows, axis=0))
        intra = _gla_intra_multi([pad8(gq[nw]) for nw in new], [pad8(gk[nw]) for nw in new],
                                 [pad8(gv[nw]) for nw in new], [pad8(logf[nw]) for nw in new], 8)
        outs, states = _gla_inter_multi(intra, states)
        o_parts += [o[:dec] for o in outs]
        for b, s_new in zip(seqs, states):
            for h in range(GLA_HEADS):
                sout_ref[b, h] = s_new[h * GLA_DK:(h + 1) * GLA_DK, h * GLA_DV:(h + 1) * GLA_DV]

    g_out = _gla_out(jnp.concatenate(o_parts, axis=0), gg, gnorm_ref[...])
    mix = jnp.concatenate([jnp.concatenate(a_parts, axis=0), g_out], axis=1).astype(BF16)
    y_ref[...] = x + _nn(mix, wout_ref[...])


def _mixer_sample(x, ct, st, wts, ck, cv, s0, *, dec, sb=16, par=8):
    n = x.shape[0]
    nseq = n // dec
    m = sb * dec
    row = lambda w: pl.BlockSpec((m, w), lambda i: (i, 0))
    seq3 = pl.BlockSpec((sb, WINDOW, LANES), lambda i: (i, 0, 0))
    seq4 = pl.BlockSpec((sb, GLA_HEADS, GLA_DK, GLA_DV), lambda i: (i, 0, 0, 0))
    kern = functools.partial(_mixer_sample_kernel, sb=sb, dec=dec, par=par)
    return pl.pallas_call(
        kern,
        grid=(nseq // sb,),
        in_specs=[row(D_MODEL), row(LANES), row(LANES)] + [_const_spec(w.shape) for w in wts] + [seq3, seq3, seq4],
        out_specs=[row(D_MODEL), row(LANES), row(LANES), seq4],
        out_shape=[jax.ShapeDtypeStruct((n, D_MODEL), F32), jax.ShapeDtypeStruct((n, LANES), F32),
                   jax.ShapeDtypeStruct((n, LANES), F32), jax.ShapeDtypeStruct(s0.shape, F32)],
        compiler_params=pltpu.CompilerParams(dimension_semantics=("arbitrary",),
                                             vmem_limit_bytes=48 * 1024 * 1024),
        name="mixer_sample",
    )(x, ct, st, *wts, ck, cv, s0)


def _shift_kernel(ck_ref, cv_ref, nk_ref, nv_ref, ok_ref, ov_ref, *, new):
    keep = ck_ref.shape[1] - new
    ok_ref[:, :keep] = ck_ref[:, new:]
    ok_ref[:, keep:] = nk_ref[...]
    ov_ref[:, :keep] = cv_ref[:, new:]
    ov_ref[:, keep:] = nv_ref[...]


def _shift_cache(ck, cv, nk, nv, *, sb=32):
    nseq, width = ck.shape
    new = nk.shape[1]
    big = pl.BlockSpec((sb, width), lambda i: (i, 0))
    small = pl.BlockSpec((sb, new), lambda i: (i, 0))
    return pl.pallas_call(
        functools.partial(_shift_kernel, new=new),
        grid=(nseq // sb,),
        in_specs=[big, big, small, small],
        out_specs=[big, big],
        out_shape=[jax.ShapeDtypeStruct(ck.shape, F32), jax.ShapeDtypeStruct(cv.shape, F32)],
        compiler_params=pltpu.CompilerParams(dimension_semantics=("arbitrary",)),
        name="shift_cache",
    )(ck, cv, nk, nv)


def _sort_network(n):
    pairs = []

    def merge(lo, m, r):
        step = 2 * r
        if step < m:
            merge(lo, m, step)
            merge(lo + r, m, step)
            pairs.extend((i, i + r) for i in range(lo + r, lo + m - r, step))
        else:
            pairs.append((lo, lo + r))

    def sort(lo, m):
        if m > 1:
            sort(lo, m // 2)
            sort(lo + m // 2, m // 2)
            merge(lo, m, 1)

    sort(0, n)
    return pairs


def _top16(s):
    t = s.shape[1]
    nv = PEER_NKEYS // 8
    sub8 = lax.broadcasted_iota(jnp.int32, (8, t), 0).astype(F32)
    v = [s[8 * k:8 * k + 8] for k in range(nv)]
    ix = [sub8 + float(8 * k) for k in range(nv)]
    for p, q in _sort_network(nv):
        swap = (v[q] > v[p]) | ((v[q] == v[p]) & (ix[q] < ix[p]))
        v[p], v[q] = jnp.where(swap, v[q], v[p]), jnp.where(swap, v[p], v[q])
        ix[p], ix[q] = jnp.where(swap, ix[q], ix[p]), jnp.where(swap, ix[p], ix[q])
    vals, idxs = [], []
    for n in range(PEER_TOPK):
        m = jnp.max(v[0], axis=0, keepdims=True)
        imin = jnp.min(jnp.where(v[0] == m, ix[0], float(PEER_NKEYS)), axis=0, keepdims=True)
        hit = ix[0] == imin
        vals.append(m)
        idxs.append(imin)
        for k in range(PEER_TOPK - 1 - n):
            v[k] = jnp.where(hit, v[k + 1], v[k])
            ix[k] = jnp.where(hit, ix[k + 1], ix[k])
    return jnp.concatenate(vals, axis=0), jnp.concatenate(idxs, axis=0)


def _pair_top16(a, ia, b, ib):
    t = a.shape[1]
    sub8 = lax.broadcasted_iota(jnp.int32, (8, t), 0).astype(F32)
    ia_lo = ia[0:8] * float(PEER_NKEYS)
    lv, le = [], []
    for y in range(PEER_TOPK):
        nx = PEER_TOPK // (y + 1)
        val = a[0:8] + b[y:y + 1]
        lv.append(val if nx >= 8 else jnp.where(sub8 < float(nx), val, NEG_INF))
        le.append(ia_lo + ib[y:y + 1])
    hv = a[8:16] + b[0:1]
    he = ia[8:16] * float(PEER_NKEYS) + ib[0:1]
    code_lo = sub8 * float(PEER_TOPK)
    code_hi = (sub8 + 8.0) * float(PEER_TOPK)
    vals, sel = [], []
    for n in range(PEER_TOPK):
        m = jnp.max(jnp.maximum(lv[0], hv), axis=0, keepdims=True)
        cmin = jnp.min(jnp.minimum(jnp.where(lv[0] == m, code_lo, 1e9), jnp.where(hv == m, code_hi, 1e9)),
                       axis=0, keepdims=True)
        hit_lo = code_lo == cmin
        hit_hi = code_hi == cmin
        sel.append(jnp.max(jnp.maximum(jnp.where(hit_lo, le[0], -1.0), jnp.where(hit_hi, he, -1.0)),
                           axis=0, keepdims=True))
        vals.append(m)
        for y in range(PEER_TOPK - 1 - n):
            lv[y] = jnp.where(hit_lo, lv[y + 1], lv[y])
            le[y] = jnp.where(hit_lo, le[y + 1], le[y])
        code_lo = jnp.where(hit_lo, code_lo + 1.0, code_lo)
        hv = jnp.where(hit_hi, NEG_INF, hv)
    return jnp.concatenate(vals, axis=0), jnp.concatenate(sel, axis=0)


def _peer_route_gates_kernel(x_ref, fnorm_ref, wqt_ref, keys_ref, h_ref, w_ref,
                             qt_s, e_s, p_s, i_s, j_s, g_s, t_s, *, tm):
    t = pl.program_id(0)
    cur = lax.rem(t, 2)
    prev = 1 - cur
    group = tm // PEER_HEADS

    @pl.when(t == 0)
    def _():
        i_s[...] = jnp.zeros_like(i_s)
        j_s[...] = jnp.zeros_like(j_s)
        g_s[...] = jnp.zeros_like(g_s)

    hb = _rms(x_ref[...], fnorm_ref[...]).astype(BF16)
    h_ref[...] = hb
    qt_s[...] = _nt(wqt_ref[...], hb)
    sub = lax.broadcasted_iota(jnp.int32, (PEER_NKEYS, LANES), 0).astype(BF16)
    one = jnp.ones((PEER_NKEYS, LANES), BF16)
    zero = jnp.zeros((PEER_NKEYS, LANES), BF16)

    def emit(n0, q):
        for i in range(PEER_NKEYS):
            a = t_s[pl.ds((16 * q) * W_PITCH + i, 8, stride=W_PITCH), :]
            b = t_s[pl.ds((16 * q + 8) * W_PITCH + i, 8, stride=W_PITCH), :]
            w_ref[pl.ds(n0 + 16 * q, 16), i * PEER_NKEYS:(i + 1) * PEER_NKEYS] = (
                jnp.concatenate([a, b], axis=0).astype(BF16))

    def head(hh, carry):
        r0 = pl.multiple_of(hh * 2 * PEER_NKEYS, 2 * PEER_NKEYS)
        s0 = _nn(keys_ref[2 * hh], qt_s[pl.ds(r0, PEER_NKEYS), :].astype(BF16))
        s1 = _nn(keys_ref[2 * hh + 1], qt_s[pl.ds(r0 + PEER_NKEYS, PEER_NKEYS), :].astype(BF16))
        n0 = pl.multiple_of(hh * group, group)
        for tok in range(group):
            ri = i_s[prev, pl.ds(n0 + tok, 1), :].astype(BF16)
            rj = j_s[prev, pl.ds(n0 + tok, 1), :].astype(BF16)
            rg = jnp.broadcast_to(g_s[prev, pl.ds(n0 + tok, 1), :].astype(BF16), sub.shape)
            at = jnp.where(sub == ri, one, zero)
            bt = jnp.where(sub == rj, rg, zero)
            t_s[tok * W_PITCH:tok * W_PITCH + PEER_NKEYS, :] = _nt(at, bt)
        a, ia = _top16(s0)
        for q in range(group // 32):
            emit(n0, 2 * q)
        b, ib = _top16(s1)
        for q in range(group // 32):
            emit(n0, 2 * q + 1)
        c, e = _pair_top16(a, ia, b, ib)
        p = jnp.exp(c - c[0:1])
        rows = pl.ds(pl.multiple_of(hh * PEER_TOPK, PEER_TOPK), PEER_TOPK)
        e_s[rows, :] = e
        p_s[rows, :] = p / jnp.sum(p, axis=0, keepdims=True)
        return carry

    lax.fori_loop(0, PEER_HEADS, head, 0)
    et = jnp.transpose(e_s[...])
    it = jnp.floor(et * (1.0 / PEER_NKEYS))
    i_s[cur] = it
    j_s[cur] = et - it * float(PEER_NKEYS)
    g_s[cur] = jnp.transpose(p_s[...])


def _peer_route_gates(x, fnorm, wqt, keys, *, tm=256):
    t = x.shape[0]
    tm = min(tm, t)
    nt = t // tm
    nsel = PEER_HEADS * PEER_TOPK
    n_exp = PEER_NKEYS * PEER_NKEYS
    group = tm // PEER_HEADS
    assert group % 32 == 0
    cur_tile = lambda i: (jnp.minimum(i, nt - 1), 0)
    return pl.pallas_call(
        functools.partial(_peer_route_gates_kernel, tm=tm),
        grid=(nt + 1,),
        in_specs=[pl.BlockSpec((tm, D_MODEL), cur_tile), _const_spec(fnorm.shape), _const_spec(wqt.shape),
                  _const_spec(keys.shape)],
        out_specs=[pl.BlockSpec((tm, D_MODEL), cur_tile),
                   pl.BlockSpec((tm, n_exp), lambda i: (jnp.maximum(i - 1, 0), 0))],
        out_shape=[jax.ShapeDtypeStruct((t, D_MODEL), BF16), jax.ShapeDtypeStruct((t, n_exp), BF16)],
        scratch_shapes=[pltpu.VMEM((wqt.shape[0], tm), F32), pltpu.VMEM((nsel, tm), F32),
                        pltpu.VMEM((nsel, tm), F32), pltpu.VMEM((2, tm, nsel), F32),
                        pltpu.VMEM((2, tm, nsel), F32), pltpu.VMEM((2, tm, nsel), F32),
                        pltpu.VMEM((group * W_PITCH, LANES), F32)],
        compiler_params=pltpu.CompilerParams(dimension_semantics=("arbitrary",),
                                             vmem_limit_bytes=56 * 1024 * 1024),
        name="peer_route_gates",
    )(x, fnorm, wqt, keys)


def _peer_dense_kernel(h_ref, x_ref, w_ref, u_ref, v_ref, y_ref):
    @pl.when(pl.program_id(1) == 0)
    def _():
        y_ref[...] = x_ref[...]

    act = _nt(h_ref[...], u_ref[...])
    gelu = 0.5 * act * (1.0 + lax.erf(act * (2.0 ** -0.5)))
    z = (w_ref[...].astype(F32) * gelu).astype(BF16)
    y_ref[...] += _nn(z, v_ref[...])


def _peer_dense(h, x, w, u, v, *, tm=1024, te=1024):
    t = h.shape[0]
    tm = min(tm, t)
    n_exp = u.shape[0]
    tok = pl.BlockSpec((tm, D_MODEL), lambda i, e: (i, 0))
    exp = pl.BlockSpec((te, D_MODEL), lambda i, e: (e, 0))
    return pl.pallas_call(
        _peer_dense_kernel,
        grid=(t // tm, n_exp // te),
        in_specs=[tok, tok, pl.BlockSpec((tm, te), lambda i, e: (i, e)), exp, exp],
        out_specs=tok,
        out_shape=jax.ShapeDtypeStruct((t, D_MODEL), F32),
        compiler_params=pltpu.CompilerParams(dimension_semantics=("arbitrary", "arbitrary"),
                                             vmem_limit_bytes=56 * 1024 * 1024),
        name="peer_dense",
    )(h, x, w, u, v)


def _cast_kernel(x_ref, o_ref):
    o_ref[...] = x_ref[...].astype(o_ref.dtype)


def _to_bf16(x, *, rows=1024):
    spec = pl.BlockSpec((rows, x.shape[1]), lambda i: (i, 0))
    return pl.pallas_call(
        _cast_kernel, grid=(x.shape[0] // rows,), in_specs=[spec], out_specs=spec,
        out_shape=jax.ShapeDtypeStruct(x.shape, BF16),
        compiler_params=pltpu.CompilerParams(dimension_semantics=("arbitrary",)),
        name="cast_bf16",
    )(x)


def _rope_tables(pos):
    half = ROPE_DIM // 2
    inv_freq = ROPE_THETA ** (-jnp.arange(half, dtype=F32) / half)
    ang = pos.astype(F32)[:, None] * inv_freq[None, :]
    cos, sin = jnp.cos(ang), jnp.sin(ang)
    n = pos.shape[0]
    c64 = jnp.concatenate([cos, cos, jnp.ones((n, HEAD_DIM - ROPE_DIM), F32)], axis=1)
    s64 = jnp.concatenate([-sin, sin, jnp.zeros((n, HEAD_DIM - ROPE_DIM), F32)], axis=1)
    return jnp.tile(c64, (1, 2)), jnp.tile(s64, (1, 2))


def kernel(x_prompt, x_sample, cache_swa_k, cache_swa_v, state_gla, attn_norm, w_in, q_norm, k_norm, attn_sinks,
           w_gate, b_gate, gla_norm, w_out, ffn_norm, peer_wq, peer_keys, peer_u, peer_v):
    depth = w_in.shape[0]
    assert depth == 1 and x_prompt.shape[0] == 1
    seq = x_prompt.shape[1]
    nseq, dec = x_sample.shape[0], x_sample.shape[1]
    xp = x_prompt[0]
    xs = x_sample.reshape(nseq * dec, D_MODEL)
    l = 0

    win = jnp.pad(w_in[l], ((0, 0), (0, C_END - w_in.shape[2]))).astype(BF16)
    qkg = jnp.concatenate([jnp.tile(q_norm[l], SWA_Q_HEADS), jnp.tile(k_norm[l], 2)])[None, :]
    sinks = jnp.broadcast_to(attn_sinks[l][:, None], (SWA_Q_HEADS, LANES))
    wgate = jnp.pad(w_gate[l], ((0, LANES - GLA_LOWRANK), (0, 0))).astype(BF16)
    wts = (attn_norm[l][None, :], win, qkg, sinks, wgate, b_gate[l][None, :],
           jnp.tile(gla_norm[l], GLA_HEADS)[None, :], w_out[l].astype(BF16))
    fnorm = ffn_norm[l][None, :]
    wqt = jnp.transpose(peer_wq[l]).astype(BF16)
    keys = peer_keys[l].reshape(PEER_HEADS * 2, PEER_NKEYS, peer_keys.shape[-1]).astype(BF16)
    u_b = _to_bf16(peer_u[l])
    v_b = _to_bf16(peer_v[l])

    ct_p, st_p = _rope_tables(jnp.arange(seq, dtype=jnp.int32))
    ct_s, st_s = _rope_tables(PAST_LEN + jnp.arange(nseq * dec, dtype=jnp.int32) % dec)

    xp2, kp, vp, sp = _mixer_prompt(xp, ct_p, st_p, wts)
    hp, wp = _peer_route_gates(xp2, fnorm, wqt, keys)
    yp = _peer_dense(hp, xp2, wp, u_b, v_b)

    ck = cache_swa_k[l].reshape(nseq, WINDOW, LANES)
    cv = cache_swa_v[l].reshape(nseq, WINDOW, LANES)
    xs2, nk, nv, ss = _mixer_sample(xs, ct_s, st_s, wts, ck, cv, state_gla[l], dec=dec)
    ck2, cv2 = _shift_cache(ck.reshape(nseq, WINDOW * LANES), cv.reshape(nseq, WINDOW * LANES),
                            nk.reshape(nseq, dec * LANES), nv.reshape(nseq, dec * LANES))
    hs, ws = _peer_route_gates(xs2, fnorm, wqt, keys)
    ys = _peer_dense(hs, xs2, ws, u_b, v_b)

    kv_shape = (1, 1, WINDOW, 2, HEAD_DIM)
    return (yp[None], ys.reshape(nseq, dec, D_MODEL),
            kp.reshape(kv_shape), vp.reshape(kv_shape), sp[None, None],
            ck2.reshape(1, nseq, WINDOW, 2, HEAD_DIM), cv2.reshape(1, nseq, WINDOW, 2, HEAD_DIM), ss[None])
```

```python
import functools

import jax
import jax.numpy as jnp
from jax import lax
from jax.experimental import pallas as pl
from jax.experimental.pallas import tpu as pltpu

F32 = jnp.float32
BF16 = jnp.bfloat16

D_MODEL = 1024
HEAD_DIM = 64
SWA_Q_HEADS = 8
WINDOW = 128
ROPE_THETA = 500000.0
ROPE_DIM = 16
PAST_LEN = 16384
GLA_HEADS = 4
GLA_DK = 64
GLA_DV = 128
GLA_LOWRANK = 16
GLA_GATE_NORM = 16.0
PEER_HEADS = 8
PEER_NKEYS = 128
PEER_TOPK = 16
NORM_EPS = 1e-6

LANES = 128
QW = SWA_Q_HEADS * HEAD_DIM
GK = GLA_HEADS * GLA_DK
GV = GLA_HEADS * GLA_DV
C_Q, C_K, C_V, C_GQ, C_GK, C_GV, C_GG, C_GA, C_END = 0, 512, 640, 768, 1024, 1280, 1792, 2304, 2432
GLA_PAD = 128
W_PITCH = 132
W_SHIFT = W_PITCH % 8
W_ROWS_ODD = 144
NEG_INF = float("-inf")


def _nn(a, b, precision=None):
    return jnp.dot(a, b, preferred_element_type=F32, precision=precision)


def _nt(a, b):
    return lax.dot_general(a, b, (((1,), (1,)), ((), ())), preferred_element_type=F32)


def _rms(x, gain):
    ms = jnp.mean(x * x, axis=-1, keepdims=True)
    return x * lax.rsqrt(ms + NORM_EPS) * gain


def _head_norm_rope(x, gain, ctab, stab):
    lane = lax.broadcasted_iota(jnp.int32, x.shape, 1)
    lo = lane < HEAD_DIM
    sq = x * x
    ms_lo = jnp.sum(jnp.where(lo, sq, 0.0), axis=-1, keepdims=True)
    ms_hi = jnp.sum(jnp.where(lo, 0.0, sq), axis=-1, keepdims=True)
    ms = jnp.where(lo, ms_lo, ms_hi) * (1.0 / HEAD_DIM)
    xn = x * lax.rsqrt(ms + NORM_EPS) * gain
    first = (lane & (HEAD_DIM - 1)) < (ROPE_DIM // 2)
    partner = jnp.where(first, pltpu.roll(xn, LANES - ROPE_DIM // 2, 1), pltpu.roll(xn, ROPE_DIM // 2, 1))
    return xn * ctab + partner * stab


def _project(x, ct, st, anorm, win, qkg, wgate, bgate):
    h = _rms(x, anorm).astype(BF16)
    p = _nn(h, win)
    qk = [
        _head_norm_rope(p[:, c * LANES:(c + 1) * LANES], qkg[:, c * LANES:(c + 1) * LANES], ct, st)
        for c in range(C_V // LANES)
    ]
    q = jnp.concatenate(qk[:4], axis=1) * (HEAD_DIM ** -0.5)
    k = qk[4]
    v = p[:, C_V:C_GQ]
    gq = p[:, C_GQ:C_GK] * (GLA_DK ** -0.5)
    gk = p[:, C_GK:C_GV]
    gv = p[:, C_GV:C_GG]
    gg = p[:, C_GG:C_GA]
    z = _nn(p[:, C_GA:C_END].astype(BF16), wgate) + bgate
    logf = (jnp.minimum(z, 0.0) - jnp.log(1.0 + jnp.exp(-jnp.abs(z)))) * (1.0 / GLA_GATE_NORM)
    return q, k, v, gq, gk, gv, gg, logf


def _swa_multi(qs, k2s, v2s, masks, sink_ref):
    n = len(qs)
    m = qs[0].shape[0]
    nk = k2s[0].shape[0]
    lo = lax.broadcasted_iota(jnp.int32, (nk, LANES), 1) < HEAD_DIM
    olane = lax.broadcasted_iota(jnp.int32, (2 * m, LANES), 1) < HEAD_DIM
    top = lax.broadcasted_iota(jnp.int32, (2 * m, 1), 0) < m
    zero = jnp.zeros((nk, LANES), F32)
    kexp, vexp = {}, {}
    for p in range(n):
        krot = pltpu.roll(k2s[p], HEAD_DIM, 1)
        vrot = pltpu.roll(v2s[p], HEAD_DIM, 1)
        for g in range(2):
            ka, kb = (k2s[p], krot) if g == 0 else (krot, k2s[p])
            va, vb = (v2s[p], vrot) if g == 0 else (vrot, v2s[p])
            kexp[p, g] = jnp.concatenate([jnp.where(lo, ka, zero), jnp.where(lo, zero, kb)], axis=0).astype(BF16)
            vexp[p, g] = jnp.concatenate([jnp.where(lo, va, zero), jnp.where(lo, zero, vb)], axis=0).astype(BF16)
    s = {}
    for p in range(n):
        for g in range(2):
            qq = jnp.concatenate([qs[p][:, (2 * g) * LANES:(2 * g + 1) * LANES],
                                  qs[p][:, (2 * g + 1) * LANES:(2 * g + 2) * LANES]], axis=0).astype(BF16)
            s[p, g] = _nt(qq, kexp[p, g])
    probs, scale = {}, {}
    for p in range(n):
        mask2 = jnp.concatenate([masks[p], masks[p]], axis=0)
        for g in range(2):
            ps, rs = [], []
            for hh in range(2):
                sh = jnp.where(mask2, s[p, g][:, hh * nk:(hh + 1) * nk], NEG_INF)
                ha, hb = 4 * g + hh, 4 * g + 2 + hh
                sink = jnp.where(top, sink_ref[ha:ha + 1, 0:1], sink_ref[hb:hb + 1, 0:1])
                mx = jnp.maximum(jnp.max(sh, axis=-1, keepdims=True), sink)
                pe = jnp.exp(sh - mx)
                den = jnp.sum(pe, axis=-1, keepdims=True) + jnp.exp(sink - mx)
                ps.append(pe)
                rs.append(1.0 / den)
            probs[p, g] = jnp.concatenate(ps, axis=1).astype(BF16)
            scale[p, g] = jnp.where(olane, rs[0], rs[1])
    outs = []
    for p in range(n):
        o = [_nn(probs[p, g], vexp[p, g]) * scale[p, g] for g in range(2)]
        outs.append(jnp.concatenate([o[0][:m], o[0][m:], o[1][:m], o[1][m:]], axis=1))
    return outs


def _zpad(a, rows):
    if a.shape[0] == rows:
        return a
    return jnp.concatenate([a, jnp.zeros((rows - a.shape[0], a.shape[1]), a.dtype)], axis=0)


def _gla_intra_multi(qs, ks, vs, fs, sub):
    n = len(qs)
    c = qs[0].shape[0]
    nsub = c // sub
    shift = sub.bit_length() - 1
    r = lax.broadcasted_iota(jnp.int32, (c, GLA_PAD), 0)
    cc = lax.broadcasted_iota(jnp.int32, (c, GLA_PAD), 1)
    causal = cc <= r
    tri2 = jnp.concatenate([causal, causal & (cc >= ((r >> shift) << shift))], axis=0).astype(BF16)
    lane_k = lax.broadcasted_iota(jnp.int32, (c, GK), 1) >> 6
    lane_v = lax.broadcasted_iota(jnp.int32, (c, GV), 1) >> 7
    rows = lax.broadcasted_iota(jnp.int32, (c, GK), 0)
    acol = lax.broadcasted_iota(jnp.int32, (c, GLA_HEADS * GLA_PAD), 1) & (GLA_PAD - 1)
    arow = lax.broadcasted_iota(jnp.int32, (c, GLA_HEADS * GLA_PAD), 0)
    amask = (acol <= arow) & (acol >= ((arow >> shift) << shift))

    def expand_k(kt):
        return jnp.concatenate(
            [_zpad(jnp.where(lane_k == h, kt, 0.0), GLA_PAD) for h in range(GLA_HEADS)], axis=0).astype(BF16)

    b, bl, bcol = [], [], []
    for p in range(n):
        fp = _zpad(fs[p], GLA_PAD)
        f_hi = fp.astype(BF16)
        r1 = fp - f_hi.astype(F32)
        f_mid = r1.astype(BF16)
        f_lo = (r1 - f_mid.astype(F32)).astype(BF16)
        cs = _nn(tri2, jnp.concatenate([f_hi, f_mid, f_lo], axis=1))
        tot = (cs[:, :GK] + cs[:, GK:2 * GK]) + cs[:, 2 * GK:]
        b.append(tot[:c])
        bl.append(tot[c:])
        bcol.append(jnp.sum(jnp.transpose(fp), axis=1, keepdims=True))
    qd, kd, qo, ko, qg, kbt, vexp, vpad = [], [], [], [], [], [], [], []
    for p in range(n):
        q, k, v = qs[p], ks[p], vs[p]
        qd.append((q * jnp.exp(bl[p])).astype(BF16))
        kd.append(expand_k(k * jnp.exp(-bl[p])))
        if nsub > 1:
            qj, kj = [], []
            for j in range(nsub - 1):
                e = (j + 1) * sub
                bj = b[p][e - 1:e]
                qj.append(jnp.where(rows >= e, q * jnp.exp(jnp.minimum(b[p] - bj, 0.0)), 0.0).astype(BF16))
                kj.append(expand_k(jnp.where((rows >= e - sub) & (rows < e),
                                             k * jnp.exp(jnp.minimum(bj - b[p], 0.0)), 0.0)))
            qo.append(jnp.concatenate(qj, axis=1))
            ko.append(jnp.concatenate(kj, axis=1))
        qg.append((q * jnp.exp(b[p])).astype(BF16))
        kbt.append(jnp.transpose(_zpad(k * jnp.exp(b[p][c - 1:c] - b[p]), GLA_PAD)).astype(BF16))
        vexp.append(jnp.concatenate(
            [_zpad(jnp.where(lane_v == h, v, 0.0), GLA_PAD) for h in range(GLA_HEADS)], axis=0).astype(BF16))
        vpad.append(_zpad(v, GLA_PAD).astype(BF16))
    a = []
    for p in range(n):
        ap = jnp.where(amask, _nt(qd[p], kd[p]), 0.0)
        if nsub > 1:
            ap = ap + _nt(qo[p], ko[p])
        a.append(ap.astype(BF16))
    return [(_nn(a[p], vexp[p]), qg[p], kbt[p], vpad[p], bcol[p]) for p in range(n)]


def _gla_inter_multi(parts, states):
    srow = lax.broadcasted_iota(jnp.int32, (GK, GV), 0) >> 6
    scol = lax.broadcasted_iota(jnp.int32, (GK, GV), 1) >> 7
    diag = srow == scol
    outs = [o + _nn(qg, s.astype(BF16)) for (o, qg, _, _, _), s in zip(parts, states)]
    new = [s * jnp.exp(bcol) + jnp.where(diag, _nn(kbt, vp), 0.0) for (_, _, kbt, vp, bcol), s in zip(parts, states)]
    return outs, new


def _gla_out(o, gg, gnorm):
    outs = []
    for h in range(GLA_HEADS):
        oh = o[:, h * GLA_DV:(h + 1) * GLA_DV]
        ms = jnp.mean(oh * oh, axis=-1, keepdims=True)
        outs.append(oh * lax.rsqrt(ms + NORM_EPS))
    on = jnp.concatenate(outs, axis=1) * gnorm
    return on * (gg / (1.0 + jnp.exp(-gg)))


def _mixer_prompt_kernel(x_ref, ct_ref, st_ref, anorm_ref, win_ref, qkg_ref, sink_ref, wgate_ref, bgate_ref,
                         gnorm_ref, wout_ref, y_ref, klast_ref, vlast_ref, sfin_ref,
                         kprev_ref, vprev_ref, s_ref, *, tm, chunk, sub):
    i = pl.program_id(0)

    @pl.when(i == 0)
    def _():
        kprev_ref[...] = jnp.zeros_like(kprev_ref)
        vprev_ref[...] = jnp.zeros_like(vprev_ref)
        s_ref[...] = jnp.zeros_like(s_ref)

    x = x_ref[...]
    q, k, v, gq, gk, gv, gg, logf = _project(
        x, ct_ref[...], st_ref[...], anorm_ref[...], win_ref[...], qkg_ref[...], wgate_ref[...], bgate_ref[...])

    kall = jnp.concatenate([kprev_ref[...], k], axis=0)
    vall = jnp.concatenate([vprev_ref[...], v], axis=0)
    qi = lax.broadcasted_iota(jnp.int32, (WINDOW, 2 * WINDOW), 0)
    kj = lax.broadcasted_iota(jnp.int32, (WINDOW, 2 * WINDOW), 1)
    band = (kj >= qi) & (kj <= qi + WINDOW)
    nb = tm // WINDOW
    masks = [band & (kj >= jnp.where(i == 0, WINDOW, 0)) if j == 0 else band for j in range(nb)]
    blocks = _swa_multi([q[j * WINDOW:(j + 1) * WINDOW] for j in range(nb)],
                        [kall[j * WINDOW:(j + 2) * WINDOW] for j in range(nb)],
                        [vall[j * WINDOW:(j + 2) * WINDOW] for j in range(nb)], masks, sink_ref)
    a_out = jnp.concatenate(blocks, axis=0)
    kprev_ref[...] = k[tm - WINDOW:]
    vprev_ref[...] = v[tm - WINDOW:]
    klast_ref[...] = k[tm - WINDOW:]
    vlast_ref[...] = v[tm - WINDOW:]

    sls = [slice(c * chunk, (c + 1) * chunk) for c in range(tm // chunk)]
    parts = _gla_intra_multi([gq[sl] for sl in sls], [gk[sl] for sl in sls], [gv[sl] for sl in sls],
                             [logf[sl] for sl in sls], sub)
    s = s_ref[...]
    os_ = []
    for part in parts:
        (o,), (s,) = _gla_inter_multi([part], [s])
        os_.append(o)
    s_ref[...] = s
    for h in range(GLA_HEADS):
        sfin_ref[h] = s[h * GLA_DK:(h + 1) * GLA_DK, h * GLA_DV:(h + 1) * GLA_DV]
    g_out = _gla_out(jnp.concatenate(os_, axis=0), gg, gnorm_ref[...])

    mix = jnp.concatenate([a_out, g_out], axis=1).astype(BF16)
    y_ref[...] = x + _nn(mix, wout_ref[...])


def _const_spec(shape):
    return pl.BlockSpec(shape, lambda *_: (0,) * len(shape))


def _mixer_prompt(x, ct, st, wts, *, tm=256, chunk=128, sub=32):
    t = x.shape[0]
    tm = min(tm, t)
    row = lambda w: pl.BlockSpec((tm, w), lambda i: (i, 0))
    kern = functools.partial(_mixer_prompt_kernel, tm=tm, chunk=chunk, sub=sub)
    return pl.pallas_call(
        kern,
        grid=(t // tm,),
        in_specs=[row(D_MODEL), row(LANES), row(LANES)] + [_const_spec(w.shape) for w in wts],
        out_specs=[row(D_MODEL), _const_spec((WINDOW, LANES)), _const_spec((WINDOW, LANES)),
                   _const_spec((GLA_HEADS, GLA_DK, GLA_DV))],
        out_shape=[jax.ShapeDtypeStruct((t, D_MODEL), F32), jax.ShapeDtypeStruct((WINDOW, LANES), F32),
                   jax.ShapeDtypeStruct((WINDOW, LANES), F32),
                   jax.ShapeDtypeStruct((GLA_HEADS, GLA_DK, GLA_DV), F32)],
        scratch_shapes=[pltpu.VMEM((WINDOW, LANES), F32), pltpu.VMEM((WINDOW, LANES), F32),
                        pltpu.VMEM((GK, GV), F32)],
        compiler_params=pltpu.CompilerParams(dimension_semantics=("arbitrary",),
                                             vmem_limit_bytes=48 * 1024 * 1024),
        name="mixer_prompt",
    )(x, ct, st, *wts)


def _mixer_sample_kernel(x_ref, ct_ref, st_ref, anorm_ref, win_ref, qkg_ref, sink_ref, wgate_ref, bgate_ref,
                         gnorm_ref, wout_ref, ck_ref, cv_ref, sin_ref,
                         y_ref, knew_ref, vnew_ref, sout_ref, *, sb, dec, par):
    x = x_ref[...]
    q, k, v, gq, gk, gv, gg, logf = _project(
        x, ct_ref[...], st_ref[...], anorm_ref[...], win_ref[...], qkg_ref[...], wgate_ref[...], bgate_ref[...])
    knew_ref[...] = k
    vnew_ref[...] = v

    per = 8 // dec
    qi = lax.broadcasted_iota(jnp.int32, (8, 2 * WINDOW), 0) & (dec - 1)
    kj = lax.broadcasted_iota(jnp.int32, (8, 2 * WINDOW), 1)
    mask = (kj >= qi) & (kj <= qi + WINDOW)
    ztail = jnp.zeros((WINDOW - dec, LANES), F32)
    pad8 = lambda a: jnp.concatenate([a, jnp.zeros((8 - dec, a.shape[1]), F32)], axis=0)

    a_parts, o_parts = [], []
    for b0 in range(0, sb, par):
        seqs = range(b0, b0 + par)
        rows8 = [slice((b // per) * 8, (b // per) * 8 + 8) for b in seqs]
        own = [slice((b % per) * dec, (b % per) * dec + dec) for b in seqs]
        new = [slice(b * dec, (b + 1) * dec) for b in seqs]
        att = _swa_multi([q[r] for r in rows8],
                         [jnp.concatenate([ck_ref[b], k[nw], ztail], axis=0) for b, nw in zip(seqs, new)],
                         [jnp.concatenate([cv_ref[b], v[nw], ztail], axis=0) for b, nw in zip(seqs, new)],
                         [mask] * par, sink_ref)
        a_parts += [a[o] for a, o in zip(att, own)]
        states = []
        for b in seqs:
            st = sin_ref[b]
            srows = []
            for h in range(GLA_HEADS):
                parts = []
                if h > 0:
                    parts.append(jnp.zeros((GLA_DK, h * GLA_DV), F32))
                parts.append(st[h])
                if h < GLA_HEADS - 1:
                    parts.append(jnp.zeros((GLA_DK, (GLA_HEADS - 1 - h) * GLA_DV), F32))
                srows.append(jnp.concatenate(parts, axis=1))
            states.append(jnp.concatenate(srows, axis=0))
        intra = _gla_intra_multi([pad8(gq[nw]) for nw in new], [pad8(gk[nw]) for nw in new],
                                 [pad8(gv[nw]) for nw in new], [pad8(logf[nw]) for nw in new], 8)
        outs, states = _gla_inter_multi(intra, states)
        o_parts += [o[:dec] for o in outs]
        for b, s_new in zip(seqs, states):
            for h in range(GLA_HEADS):
                sout_ref[b, h] = s_new[h * GLA_DK:(h + 1) * GLA_DK, h * GLA_DV:(h + 1) * GLA_DV]

    g_out = _gla_out(jnp.concatenate(o_parts, axis=0), gg, gnorm_ref[...])
    mix = jnp.concatenate([jnp.concatenate(a_parts, axis=0), g_out], axis=1).astype(BF16)
    y_ref[...] = x + _nn(mix, wout_ref[...])


def _mixer_sample(x, ct, st, wts, ck, cv, s0, *, dec, sb=16, par=8):
    n = x.shape[0]
    nseq = n // dec
    m = sb * dec
    row = lambda w: pl.BlockSpec((m, w), lambda i: (i, 0))
    seq3 = pl.BlockSpec((sb, WINDOW, LANES), lambda i: (i, 0, 0))
    seq4 = pl.BlockSpec((sb, GLA_HEADS, GLA_DK, GLA_DV), lambda i: (i, 0, 0, 0))
    kern = functools.partial(_mixer_sample_kernel, sb=sb, dec=dec, par=par)
    return pl.pallas_call(
        kern,
        grid=(nseq // sb,),
        in_specs=[row(D_MODEL), row(LANES), row(LANES)] + [_const_spec(w.shape) for w in wts] + [seq3, seq3, seq4],
        out_specs=[row(D_MODEL), row(LANES), row(LANES), seq4],
        out_shape=[jax.ShapeDtypeStruct((n, D_MODEL), F32), jax.ShapeDtypeStruct((n, LANES), F32),
                   jax.ShapeDtypeStruct((n, LANES), F32), jax.ShapeDtypeStruct(s0.shape, F32)],
        compiler_params=pltpu.CompilerParams(dimension_semantics=("arbitrary",),
                                             vmem_limit_bytes=48 * 1024 * 1024),
        name="mixer_sample",
    )(x, ct, st, *wts, ck, cv, s0)


def _shift_kernel(ck_ref, cv_ref, nk_ref, nv_ref, ok_ref, ov_ref, *, new):
    keep = ck_ref.shape[1] - new
    ok_ref[:, :keep] = ck_ref[:, new:]
    ok_ref[:, keep:] = nk_ref[...]
    ov_ref[:, :keep] = cv_ref[:, new:]
    ov_ref[:, keep:] = nv_ref[...]


def _shift_cache(ck, cv, nk, nv, *, sb=32):
    nseq, width = ck.shape
    new = nk.shape[1]
    big = pl.BlockSpec((sb, width), lambda i: (i, 0))
    small = pl.BlockSpec((sb, new), lambda i: (i, 0))
    return pl.pallas_call(
        functools.partial(_shift_kernel, new=new),
        grid=(nseq // sb,),
        in_specs=[big, big, small, small],
        out_specs=[big, big],
        out_shape=[jax.ShapeDtypeStruct(ck.shape, F32), jax.ShapeDtypeStruct(cv.shape, F32)],
        compiler_params=pltpu.CompilerParams(dimension_semantics=("arbitrary",)),
        name="shift_cache",
    )(ck, cv, nk, nv)


def _sort_network(n):
    pairs = []

    def merge(lo, m, r):
        step = 2 * r
        if step < m:
            merge(lo, m, step)
            merge(lo + r, m, step)
            pairs.extend((i, i + r) for i in range(lo + r, lo + m - r, step))
        else:
            pairs.append((lo, lo + r))

    def sort(lo, m):
        if m > 1:
            sort(lo, m // 2)
            sort(lo + m // 2, m // 2)
            merge(lo, m, 1)

    sort(0, n)
    return pairs


def _top16(s):
    t = s.shape[1]
    nv = PEER_NKEYS // 8
    sub8 = lax.broadcasted_iota(jnp.int32, (8, t), 0).astype(F32)
    v = [s[8 * k:8 * k + 8] for k in range(nv)]
    ix = [sub8 + float(8 * k) for k in range(nv)]
    for p, q in _sort_network(nv):
        swap = (v[q] > v[p]) | ((v[q] == v[p]) & (ix[q] < ix[p]))
        v[p], v[q] = jnp.where(swap, v[q], v[p]), jnp.where(swap, v[p], v[q])
        ix[p], ix[q] = jnp.where(swap, ix[q], ix[p]), jnp.where(swap, ix[p], ix[q])
    vals, idxs = [], []
    for n in range(PEER_TOPK):
        m = jnp.max(v[0], axis=0, keepdims=True)
        imin = jnp.min(jnp.where(v[0] == m, ix[0], float(PEER_NKEYS)), axis=0, keepdims=True)
        hit = ix[0] == imin
        vals.append(m)
        idxs.append(imin)
        for k in range(PEER_TOPK - 1 - n):
            v[k] = jnp.where(hit, v[k + 1], v[k])
            ix[k] = jnp.where(hit, ix[k + 1], ix[k])
    return jnp.concatenate(vals, axis=0), jnp.concatenate(idxs, axis=0)


def _pair_top16(a, ia, b, ib):
    t = a.shape[1]
    sub8 = lax.broadcasted_iota(jnp.int32, (8, t), 0).astype(F32)
    ia_lo = ia[0:8] * float(PEER_NKEYS)
    lv, le = [], []
    for y in range(PEER_TOPK):
        nx = PEER_TOPK // (y + 1)
        val = a[0:8] + b[y:y + 1]
        lv.append(val if nx >= 8 else jnp.where(sub8 < float(nx), val, NEG_INF))
        le.append(ia_lo + ib[y:y + 1])
    hv = a[8:16] + b[0:1]
    he = ia[8:16] * float(PEER_NKEYS) + ib[0:1]
    code_lo = sub8 * float(PEER_TOPK)
    code_hi = (sub8 + 8.0) * float(PEER_TOPK)
    vals, sel = [], []
    for n in range(PEER_TOPK):
        m = jnp.max(jnp.maximum(lv[0], hv), axis=0, keepdims=True)
        cmin = jnp.min(jnp.minimum(jnp.where(lv[0] == m, code_lo, 1e9), jnp.where(hv == m, code_hi, 1e9)),
                       axis=0, keepdims=True)
        hit_lo = code_lo == cmin
        hit_hi = code_hi == cmin
        sel.append(jnp.max(jnp.maximum(jnp.where(hit_lo, le[0], -1.0), jnp.where(hit_hi, he, -1.0)),
                           axis=0, keepdims=True))
        vals.append(m)
        for y in range(PEER_TOPK - 1 - n):
            lv[y] = jnp.where(hit_lo, lv[y + 1], lv[y])
            le[y] = jnp.where(hit_lo, le[y + 1], le[y])
        code_lo = jnp.where(hit_lo, code_lo + 1.0, code_lo)
        hv = jnp.where(hit_hi, NEG_INF, hv)
    return jnp.concatenate(vals, axis=0), jnp.concatenate(sel, axis=0)


def _peer_route_gates_kernel(x_ref, fnorm_ref, wqt_ref, keys_ref, h_ref, w_ref,
                             qt_s, e_s, p_s, i_s, j_s, g_s, t_s, *, tm):
    t = pl.program_id(0)
    cur = lax.rem(t, 2)
    prev = 1 - cur
    group = tm // PEER_HEADS

    @pl.when(t == 0)
    def _():
        i_s[...] = jnp.zeros_like(i_s)
        j_s[...] = jnp.zeros_like(j_s)
        g_s[...] = jnp.zeros_like(g_s)

    hb = _rms(x_ref[...], fnorm_ref[...]).astype(BF16)
    h_ref[...] = hb
    qt_s[...] = _nt(wqt_ref[...], hb)
    sub = lax.broadcasted_iota(jnp.int32, (PEER_NKEYS, LANES), 0).astype(BF16)
    sub_odd = (lax.broadcasted_iota(jnp.int32, (W_ROWS_ODD, LANES), 0) - W_SHIFT).astype(BF16)
    one = jnp.ones((W_ROWS_ODD, LANES), BF16)
    zero = jnp.zeros((W_ROWS_ODD, LANES), BF16)

    def emit(n0, q):
        for i in range(PEER_NKEYS):
            a = t_s[pl.ds((16 * q) * W_PITCH + i, 8, stride=W_PITCH), :]
            b = t_s[pl.ds((16 * q + 8) * W_PITCH + i, 8, stride=W_PITCH), :]
            w_ref[pl.ds(n0 + 16 * q, 16), i * PEER_NKEYS:(i + 1) * PEER_NKEYS] = (
                jnp.concatenate([a, b], axis=0).astype(BF16))

    def head(hh, carry):
        r0 = pl.multiple_of(hh * 2 * PEER_NKEYS, 2 * PEER_NKEYS)
        s0 = _nn(keys_ref[2 * hh], qt_s[pl.ds(r0, PEER_NKEYS), :].astype(BF16))
        s1 = _nn(keys_ref[2 * hh + 1], qt_s[pl.ds(r0 + PEER_NKEYS, PEER_NKEYS), :].astype(BF16))
        n0 = pl.multiple_of(hh * group, group)
        for tok in range(group):
            ri = i_s[prev, pl.ds(n0 + tok, 1), :].astype(BF16)
            rj = j_s[prev, pl.ds(n0 + tok, 1), :].astype(BF16)
            rg = jnp.broadcast_to(g_s[prev, pl.ds(n0 + tok, 1), :].astype(BF16), sub.shape)
            bt = jnp.where(sub == rj, rg, zero[:PEER_NKEYS])
            if tok % 2 == 0:
                at = jnp.where(sub == ri, one[:PEER_NKEYS], zero[:PEER_NKEYS])
                t_s[tok * W_PITCH:tok * W_PITCH + PEER_NKEYS, :] = _nt(at, bt)
            else:
                at = jnp.where(sub_odd == ri, one, zero)
                t_s[tok * W_PITCH - W_SHIFT:tok * W_PITCH - W_SHIFT + W_ROWS_ODD, :] = _nt(at, bt)
        a, ia = _top16(s0)
        for q in range(group // 32):
            emit(n0, 2 * q)
        b, ib = _top16(s1)
        for q in range(group // 32):
            emit(n0, 2 * q + 1)
        c, e = _pair_top16(a, ia, b, ib)
        p = jnp.exp(c - c[0:1])
        rows = pl.ds(pl.multiple_of(hh * PEER_TOPK, PEER_TOPK), PEER_TOPK)
        e_s[rows, :] = e
        p_s[rows, :] = p / jnp.sum(p, axis=0, keepdims=True)
        return carry

    lax.fori_loop(0, PEER_HEADS, head, 0)
    et = jnp.transpose(e_s[...])
    it = jnp.floor(et * (1.0 / PEER_NKEYS))
    i_s[cur] = it
    j_s[cur] = et - it * float(PEER_NKEYS)
    g_s[cur] = jnp.transpose(p_s[...])


def _peer_route_gates(x, fnorm, wq, keys, *, tm=256):
    t = x.shape[0]
    tm = min(tm, t)
    nt = t // tm
    nsel = PEER_HEADS * PEER_TOPK
    n_exp = PEER_NKEYS * PEER_NKEYS
    group = tm // PEER_HEADS
    assert group % 32 == 0
    cur_tile = lambda i: (jnp.minimum(i, nt - 1), 0)
    return pl.pallas_call(
        functools.partial(_peer_route_gates_kernel, tm=tm),
        grid=(nt + 1,),
        in_specs=[pl.BlockSpec((tm, D_MODEL), cur_tile), _const_spec(fnorm.shape), _const_spec(wq.shape),
                  _const_spec(keys.shape)],
        out_specs=[pl.BlockSpec((tm, D_MODEL), cur_tile),
                   pl.BlockSpec((tm, n_exp), lambda i: (jnp.maximum(i - 1, 0), 0))],
        out_shape=[jax.ShapeDtypeStruct((t, D_MODEL), BF16), jax.ShapeDtypeStruct((t, n_exp), BF16)],
        scratch_shapes=[pltpu.VMEM((wq.shape[0], tm), F32), pltpu.VMEM((nsel, tm), F32),
                        pltpu.VMEM((nsel, tm), F32),
                        pltpu.VMEM((2, tm, nsel), F32), pltpu.VMEM((2, tm, nsel), F32),
                        pltpu.VMEM((2, tm, nsel), F32),
                        pltpu.VMEM(((group - 1) * W_PITCH - W_SHIFT + W_ROWS_ODD, LANES), F32)],
        compiler_params=pltpu.CompilerParams(dimension_semantics=("arbitrary",),
                                             vmem_limit_bytes=56 * 1024 * 1024),
        name="peer_route_gates",
    )(x, fnorm, wq, keys)


def _peer_dense_kernel(h_ref, x_ref, w_ref, u_ref, v_ref, y_ref):
    @pl.when(pl.program_id(1) == 0)
    def _():
        y_ref[...] = x_ref[...]

    act = _nt(h_ref[...], u_ref[...])
    gelu = 0.5 * act * (1.0 + lax.erf(act * (2.0 ** -0.5)))
    z = (w_ref[...].astype(F32) * gelu).astype(BF16)
    y_ref[...] += _nn(z, v_ref[...])


def _peer_dense(h, x, w, u, v, *, tm=1024, te=1024):
    t = h.shape[0]
    tm = min(tm, t)
    n_exp = u.shape[0]
    tok = pl.BlockSpec((tm, D_MODEL), lambda i, e: (i, 0))
    exp = pl.BlockSpec((te, D_MODEL), lambda i, e: (e, 0))
    return pl.pallas_call(
        _peer_dense_kernel,
        grid=(t // tm, n_exp // te),
        in_specs=[tok, tok, pl.BlockSpec((tm, te), lambda i, e: (i, e)), exp, exp],
        out_specs=tok,
        out_shape=jax.ShapeDtypeStruct((t, D_MODEL), F32),
        compiler_params=pltpu.CompilerParams(dimension_semantics=("arbitrary", "arbitrary"),
                                             vmem_limit_bytes=56 * 1024 * 1024),
        name="peer_dense",
    )(h, x, w, u, v)


def _cast_kernel(x_ref, o_ref):
    o_ref[...] = x_ref[...].astype(o_ref.dtype)


def _to_bf16(x, *, rows=1024):
    spec = pl.BlockSpec((rows, x.shape[1]), lambda i: (i, 0))
    return pl.pallas_call(
        _cast_kernel, grid=(x.shape[0] // rows,), in_specs=[spec], out_specs=spec,
        out_shape=jax.ShapeDtypeStruct(x.shape, BF16),
        compiler_params=pltpu.CompilerParams(dimension_semantics=("arbitrary",)),
        name="cast_bf16",
    )(x)


def _rope_tables(pos):
    half = ROPE_DIM // 2
    inv_freq = ROPE_THETA ** (-jnp.arange(half, dtype=F32) / half)
    ang = pos.astype(F32)[:, None] * inv_freq[None, :]
    cos, sin = jnp.cos(ang), jnp.sin(ang)
    n = pos.shape[0]
    c64 = jnp.concatenate([cos, cos, jnp.ones((n, HEAD_DIM - ROPE_DIM), F32)], axis=1)
    s64 = jnp.concatenate([-sin, sin, jnp.zeros((n, HEAD_DIM - ROPE_DIM), F32)], axis=1)
    return jnp.tile(c64, (1, 2)), jnp.tile(s64, (1, 2))


def kernel(x_prompt, x_sample, cache_swa_k, cache_swa_v, state_gla, attn_norm, w_in, q_norm, k_norm, attn_sinks,
           w_gate, b_gate, gla_norm, w_out, ffn_norm, peer_wq, peer_keys, peer_u, peer_v):
    depth = w_in.shape[0]
    assert depth == 1 and x_prompt.shape[0] == 1
    seq = x_prompt.shape[1]
    nseq, dec = x_sample.shape[0], x_sample.shape[1]
    xp = x_prompt[0]
    xs = x_sample.reshape(nseq * dec, D_MODEL)
    l = 0

    win = jnp.pad(w_in[l], ((0, 0), (0, C_END - w_in.shape[2]))).astype(BF16)
    qkg = jnp.concatenate([jnp.tile(q_norm[l], SWA_Q_HEADS), jnp.tile(k_norm[l], 2)])[None, :]
    sinks = jnp.broadcast_to(attn_sinks[l][:, None], (SWA_Q_HEADS, LANES))
    wgate = jnp.pad(w_gate[l], ((0, LANES - GLA_LOWRANK), (0, 0))).astype(BF16)
    wts = (attn_norm[l][None, :], win, qkg, sinks, wgate, b_gate[l][None, :],
           jnp.tile(gla_norm[l], GLA_HEADS)[None, :], w_out[l].astype(BF16))
    fnorm = ffn_norm[l][None, :]
    wq = jnp.transpose(peer_wq[l]).astype(BF16)
    keys = peer_keys[l].reshape(PEER_HEADS * 2, PEER_NKEYS, peer_keys.shape[-1]).astype(BF16)
    u_b = _to_bf16(peer_u[l])
    v_b = _to_bf16(peer_v[l])

    ct_p, st_p = _rope_tables(jnp.arange(seq, dtype=jnp.int32))
    ct_s, st_s = _rope_tables(PAST_LEN + jnp.arange(nseq * dec, dtype=jnp.int32) % dec)

    xp2, kp, vp, sp = _mixer_prompt(xp, ct_p, st_p, wts)
    hp, wp = _peer_route_gates(xp2, fnorm, wq, keys)
    yp = _peer_dense(hp, xp2, wp, u_b, v_b)

    ck = cache_swa_k[l].reshape(nseq, WINDOW, LANES)
    cv = cache_swa_v[l].reshape(nseq, WINDOW, LANES)
    xs2, nk, nv, ss = _mixer_sample(xs, ct_s, st_s, wts, ck, cv, state_gla[l], dec=dec)
    ck2, cv2 = _shift_cache(ck.reshape(nseq, WINDOW * LANES), cv.reshape(nseq, WINDOW * LANES),
                            nk.reshape(nseq, dec * LANES), nv.reshape(nseq, dec * LANES))
    hs, ws = _peer_route_gates(xs2, fnorm, wq, keys)
    ys = _peer_dense(hs, xs2, ws, u_b, v_b)

    kv_shape = (1, 1, WINDOW, 2, HEAD_DIM)
    return (yp[None], ys.reshape(nseq, dec, D_MODEL),
            kp.reshape(kv_shape), vp.reshape(kv_shape), sp[None, None],
            ck2.reshape(1, nseq, WINDOW, 2, HEAD_DIM), cv2.reshape(1, nseq, WINDOW, 2, HEAD_DIM), ss[None])
```

```python
import functools

import jax
import jax.numpy as jnp
from jax import lax
from jax.experimental import pallas as pl
from jax.experimental.pallas import tpu as pltpu

F32 = jnp.float32
BF16 = jnp.bfloat16

D_MODEL = 1024
HEAD_DIM = 64
SWA_Q_HEADS = 8
WINDOW = 128
ROPE_THETA = 500000.0
ROPE_DIM = 16
PAST_LEN = 16384
GLA_HEADS = 4
GLA_DK = 64
GLA_DV = 128
GLA_LOWRANK = 16
GLA_GATE_NORM = 16.0
PEER_HEADS = 8
PEER_NKEYS = 128
PEER_TOPK = 16
NORM_EPS = 1e-6

LANES = 128
QW = SWA_Q_HEADS * HEAD_DIM
GK = GLA_HEADS * GLA_DK
GV = GLA_HEADS * GLA_DV
C_Q, C_K, C_V, C_GQ, C_GK, C_GV, C_GG, C_GA, C_END = 0, 512, 640, 768, 1024, 1280, 1792, 2304, 2432
GLA_PAD = 128
W_PITCH = 132
W_SHIFT = W_PITCH % 8
W_ROWS_ODD = 144
NEG_INF = float("-inf")


def _nn(a, b, precision=None):
    return jnp.dot(a, b, preferred_element_type=F32, precision=precision)


def _nt(a, b):
    return lax.dot_general(a, b, (((1,), (1,)), ((), ())), preferred_element_type=F32)


def _rms(x, gain):
    ms = jnp.mean(x * x, axis=-1, keepdims=True)
    return x * lax.rsqrt(ms + NORM_EPS) * gain


def _head_norm_rope(x, gain, ctab, stab):
    lane = lax.broadcasted_iota(jnp.int32, x.shape, 1)
    lo = lane < HEAD_DIM
    sq = x * x
    ms_lo = jnp.sum(jnp.where(lo, sq, 0.0), axis=-1, keepdims=True)
    ms_hi = jnp.sum(jnp.where(lo, 0.0, sq), axis=-1, keepdims=True)
    ms = jnp.where(lo, ms_lo, ms_hi) * (1.0 / HEAD_DIM)
    xn = x * lax.rsqrt(ms + NORM_EPS) * gain
    first = (lane & (HEAD_DIM - 1)) < (ROPE_DIM // 2)
    partner = jnp.where(first, pltpu.roll(xn, LANES - ROPE_DIM // 2, 1), pltpu.roll(xn, ROPE_DIM // 2, 1))
    return xn * ctab + partner * stab


def _project(x, ct, st, anorm, win, qkg, wgate, bgate):
    h = _rms(x, anorm).astype(BF16)
    p = _nn(h, win)
    qk = [
        _head_norm_rope(p[:, c * LANES:(c + 1) * LANES], qkg[:, c * LANES:(c + 1) * LANES], ct, st)
        for c in range(C_V // LANES)
    ]
    q = jnp.concatenate(qk[:4], axis=1) * (HEAD_DIM ** -0.5)
    k = qk[4]
    v = p[:, C_V:C_GQ]
    gq = p[:, C_GQ:C_GK] * (GLA_DK ** -0.5)
    gk = p[:, C_GK:C_GV]
    gv = p[:, C_GV:C_GG]
    gg = p[:, C_GG:C_GA]
    z = _nn(p[:, C_GA:C_END].astype(BF16), wgate) + bgate
    logf = (jnp.minimum(z, 0.0) - jnp.log(1.0 + jnp.exp(-jnp.abs(z)))) * (1.0 / GLA_GATE_NORM)
    return q, k, v, gq, gk, gv, gg, logf


def _swa_multi(qs, k2s, v2s, masks, sink_ref):
    n = len(qs)
    m = qs[0].shape[0]
    nk = k2s[0].shape[0]
    lo = lax.broadcasted_iota(jnp.int32, (nk, LANES), 1) < HEAD_DIM
    olane = lax.broadcasted_iota(jnp.int32, (2 * m, LANES), 1) < HEAD_DIM
    top = lax.broadcasted_iota(jnp.int32, (2 * m, 1), 0) < m
    zero = jnp.zeros((nk, LANES), F32)
    kexp, vexp = {}, {}
    for p in range(n):
        krot = pltpu.roll(k2s[p], HEAD_DIM, 1)
        vrot = pltpu.roll(v2s[p], HEAD_DIM, 1)
        for g in range(2):
            ka, kb = (k2s[p], krot) if g == 0 else (krot, k2s[p])
            va, vb = (v2s[p], vrot) if g == 0 else (vrot, v2s[p])
            kexp[p, g] = jnp.concatenate([jnp.where(lo, ka, zero), jnp.where(lo, zero, kb)], axis=0).astype(BF16)
            vexp[p, g] = jnp.concatenate([jnp.where(lo, va, zero), jnp.where(lo, zero, vb)], axis=0).astype(BF16)
    s = {}
    for p in range(n):
        for g in range(2):
            qq = jnp.concatenate([qs[p][:, (2 * g) * LANES:(2 * g + 1) * LANES],
                                  qs[p][:, (2 * g + 1) * LANES:(2 * g + 2) * LANES]], axis=0).astype(BF16)
            s[p, g] = _nt(qq, kexp[p, g])
    probs, scale = {}, {}
    for p in range(n):
        mask2 = jnp.concatenate([masks[p], masks[p]], axis=0)
        for g in range(2):
            ps, rs = [], []
            for hh in range(2):
                sh = jnp.where(mask2, s[p, g][:, hh * nk:(hh + 1) * nk], NEG_INF)
                ha, hb = 4 * g + hh, 4 * g + 2 + hh
                sink = jnp.where(top, sink_ref[ha:ha + 1, 0:1], sink_ref[hb:hb + 1, 0:1])
                mx = jnp.maximum(jnp.max(sh, axis=-1, keepdims=True), sink)
                pe = jnp.exp(sh - mx)
                den = jnp.sum(pe, axis=-1, keepdims=True) + jnp.exp(sink - mx)
                ps.append(pe)
                rs.append(1.0 / den)
            probs[p, g] = jnp.concatenate(ps, axis=1).astype(BF16)
            scale[p, g] = jnp.where(olane, rs[0], rs[1])
    outs = []
    for p in range(n):
        o = [_nn(probs[p, g], vexp[p, g]) * scale[p, g] for g in range(2)]
        outs.append(jnp.concatenate([o[0][:m], o[0][m:], o[1][:m], o[1][m:]], axis=1))
    return outs


def _zpad(a, rows):
    if a.shape[0] == rows:
        return a
    return jnp.concatenate([a, jnp.zeros((rows - a.shape[0], a.shape[1]), a.dtype)], axis=0)


def _gla_intra_multi(qs, ks, vs, fs, sub):
    n = len(qs)
    c = qs[0].shape[0]
    nsub = c // sub
    shift = sub.bit_length() - 1
    r = lax.broadcasted_iota(jnp.int32, (c, GLA_PAD), 0)
    cc = lax.broadcasted_iota(jnp.int32, (c, GLA_PAD), 1)
    causal = cc <= r
    tri2 = jnp.concatenate([causal, causal & (cc >= ((r >> shift) << shift))], axis=0).astype(BF16)
    lane_k = lax.broadcasted_iota(jnp.int32, (c, GK), 1) >> 6
    lane_v = lax.broadcasted_iota(jnp.int32, (c, GV), 1) >> 7
    rows = lax.broadcasted_iota(jnp.int32, (c, GK), 0)
    acol = lax.broadcasted_iota(jnp.int32, (c, GLA_HEADS * GLA_PAD), 1) & (GLA_PAD - 1)
    arow = lax.broadcasted_iota(jnp.int32, (c, GLA_HEADS * GLA_PAD), 0)
    amask = (acol <= arow) & (acol >= ((arow >> shift) << shift))

    def expand_k(kt):
        return jnp.concatenate(
            [_zpad(jnp.where(lane_k == h, kt, 0.0), GLA_PAD) for h in range(GLA_HEADS)], axis=0).astype(BF16)

    b, bl, bcol = [], [], []
    for p in range(n):
        fp = _zpad(fs[p], GLA_PAD)
        f_hi = fp.astype(BF16)
        r1 = fp - f_hi.astype(F32)
        f_mid = r1.astype(BF16)
        f_lo = (r1 - f_mid.astype(F32)).astype(BF16)
        cs = _nn(tri2, jnp.concatenate([f_hi, f_mid, f_lo], axis=1))
        tot = (cs[:, :GK] + cs[:, GK:2 * GK]) + cs[:, 2 * GK:]
        b.append(tot[:c])
        bl.append(tot[c:])
        bcol.append(jnp.sum(jnp.transpose(fp), axis=1, keepdims=True))
    qd, kd, qo, ko, qg, kbt, vexp, vpad = [], [], [], [], [], [], [], []
    for p in range(n):
        q, k, v = qs[p], ks[p], vs[p]
        qd.append((q * jnp.exp(bl[p])).astype(BF16))
        kd.append(expand_k(k * jnp.exp(-bl[p])))
        if nsub > 1:
            qj, kj = [], []
            for j in range(nsub - 1):
                e = (j + 1) * sub
                bj = b[p][e - 1:e]
                qj.append(jnp.where(rows >= e, q * jnp.exp(jnp.minimum(b[p] - bj, 0.0)), 0.0).astype(BF16))
                kj.append(expand_k(jnp.where((rows >= e - sub) & (rows < e),
                                             k * jnp.exp(jnp.minimum(bj - b[p], 0.0)), 0.0)))
            qo.append(jnp.concatenate(qj, axis=1))
            ko.append(jnp.concatenate(kj, axis=1))
        qg.append((q * jnp.exp(b[p])).astype(BF16))
        kbt.append(jnp.transpose(_zpad(k * jnp.exp(b[p][c - 1:c] - b[p]), GLA_PAD)).astype(BF16))
        vexp.append(jnp.concatenate(
            [_zpad(jnp.where(lane_v == h, v, 0.0), GLA_PAD) for h in range(GLA_HEADS)], axis=0).astype(BF16))
        vpad.append(_zpad(v, GLA_PAD).astype(BF16))
    a = []
    for p in range(n):
        ap = jnp.where(amask, _nt(qd[p], kd[p]), 0.0)
        if nsub > 1:
            ap = ap + _nt(qo[p], ko[p])
        a.append(ap.astype(BF16))
    return [(_nn(a[p], vexp[p]), qg[p], kbt[p], vpad[p], bcol[p]) for p in range(n)]


def _gla_inter_multi(parts, states):
    srow = lax.broadcasted_iota(jnp.int32, (GK, GV), 0) >> 6
    scol = lax.broadcasted_iota(jnp.int32, (GK, GV), 1) >> 7
    diag = srow == scol
    outs = [o + _nn(qg, s.astype(BF16)) for (o, qg, _, _, _), s in zip(parts, states)]
    new = [s * jnp.exp(bcol) + jnp.where(diag, _nn(kbt, vp), 0.0) for (_, _, kbt, vp, bcol), s in zip(parts, states)]
    return outs, new


def _gla_out(o, gg, gnorm):
    outs = []
    for h in range(GLA_HEADS):
        oh = o[:, h * GLA_DV:(h + 1) * GLA_DV]
        ms = jnp.mean(oh * oh, axis=-1, keepdims=True)
        outs.append(oh * lax.rsqrt(ms + NORM_EPS))
    on = jnp.concatenate(outs, axis=1) * gnorm
    return on * (gg / (1.0 + jnp.exp(-gg)))


def _mixer_prompt_kernel(x_ref, ct_ref, st_ref, anorm_ref, win_ref, qkg_ref, sink_ref, wgate_ref, bgate_ref,
                         gnorm_ref, wout_ref, y_ref, klast_ref, vlast_ref, sfin_ref,
                         kprev_ref, vprev_ref, s_ref, *, tm, chunk, sub):
    i = pl.program_id(0)

    @pl.when(i == 0)
    def _():
        kprev_ref[...] = jnp.zeros_like(kprev_ref)
        vprev_ref[...] = jnp.zeros_like(vprev_ref)
        s_ref[...] = jnp.zeros_like(s_ref)

    x = x_ref[...]
    q, k, v, gq, gk, gv, gg, logf = _project(
        x, ct_ref[...], st_ref[...], anorm_ref[...], win_ref[...], qkg_ref[...], wgate_ref[...], bgate_ref[...])

    kall = jnp.concatenate([kprev_ref[...], k], axis=0)
    vall = jnp.concatenate([vprev_ref[...], v], axis=0)
    qi = lax.broadcasted_iota(jnp.int32, (WINDOW, 2 * WINDOW), 0)
    kj = lax.broadcasted_iota(jnp.int32, (WINDOW, 2 * WINDOW), 1)
    band = (kj >= qi) & (kj <= qi + WINDOW)
    nb = tm // WINDOW
    masks = [band & (kj >= jnp.where(i == 0, WINDOW, 0)) if j == 0 else band for j in range(nb)]
    blocks = _swa_multi([q[j * WINDOW:(j + 1) * WINDOW] for j in range(nb)],
                        [kall[j * WINDOW:(j + 2) * WINDOW] for j in range(nb)],
                        [vall[j * WINDOW:(j + 2) * WINDOW] for j in range(nb)], masks, sink_ref)
    a_out = jnp.concatenate(blocks, axis=0)
    kprev_ref[...] = k[tm - WINDOW:]
    vprev_ref[...] = v[tm - WINDOW:]
    klast_ref[...] = k[tm - WINDOW:]
    vlast_ref[...] = v[tm - WINDOW:]

    sls = [slice(c * chunk, (c + 1) * chunk) for c in range(tm // chunk)]
    parts = _gla_intra_multi([gq[sl] for sl in sls], [gk[sl] for sl in sls], [gv[sl] for sl in sls],
                             [logf[sl] for sl in sls], sub)
    s = s_ref[...]
    os_ = []
    for part in parts:
        (o,), (s,) = _gla_inter_multi([part], [s])
        os_.append(o)
    s_ref[...] = s
    for h in range(GLA_HEADS):
        sfin_ref[h] = s[h * GLA_DK:(h + 1) * GLA_DK, h * GLA_DV:(h + 1) * GLA_DV]
    g_out = _gla_out(jnp.concatenate(os_, axis=0), gg, gnorm_ref[...])

    mix = jnp.concatenate([a_out, g_out], axis=1).astype(BF16)
    y_ref[...] = x + _nn(mix, wout_ref[...])


def _const_spec(shape):
    return pl.BlockSpec(shape, lambda *_: (0,) * len(shape))


def _mixer_prompt(x, ct, st, wts, *, tm=256, chunk=128, sub=32):
    t = x.shape[0]
    tm = min(tm, t)
    row = lambda w: pl.BlockSpec((tm, w), lambda i: (i, 0))
    kern = functools.partial(_mixer_prompt_kernel, tm=tm, chunk=chunk, sub=sub)
    return pl.pallas_call(
        kern,
        grid=(t // tm,),
        in_specs=[row(D_MODEL), row(LANES), row(LANES)] + [_const_spec(w.shape) for w in wts],
        out_specs=[row(D_MODEL), _const_spec((WINDOW, LANES)), _const_spec((WINDOW, LANES)),
                   _const_spec((GLA_HEADS, GLA_DK, GLA_DV))],
        out_shape=[jax.ShapeDtypeStruct((t, D_MODEL), F32), jax.ShapeDtypeStruct((WINDOW, LANES), F32),
                   jax.ShapeDtypeStruct((WINDOW, LANES), F32),
                   jax.ShapeDtypeStruct((GLA_HEADS, GLA_DK, GLA_DV), F32)],
        scratch_shapes=[pltpu.VMEM((WINDOW, LANES), F32), pltpu.VMEM((WINDOW, LANES), F32),
                        pltpu.VMEM((GK, GV), F32)],
        compiler_params=pltpu.CompilerParams(dimension_semantics=("arbitrary",),
                                             vmem_limit_bytes=48 * 1024 * 1024),
        name="mixer_prompt",
    )(x, ct, st, *wts)


def _mixer_sample_kernel(x_ref, ct_ref, st_ref, anorm_ref, win_ref, qkg_ref, sink_ref, wgate_ref, bgate_ref,
                         gnorm_ref, wout_ref, ck_ref, cv_ref, sin_ref,
                         y_ref, knew_ref, vnew_ref, sout_ref, *, sb, dec, par):
    x = x_ref[...]
    q, k, v, gq, gk, gv, gg, logf = _project(
        x, ct_ref[...], st_ref[...], anorm_ref[...], win_ref[...], qkg_ref[...], wgate_ref[...], bgate_ref[...])
    knew_ref[...] = k
    vnew_ref[...] = v

    per = 8 // dec
    qi = lax.broadcasted_iota(jnp.int32, (8, 2 * WINDOW), 0) & (dec - 1)
    kj = lax.broadcasted_iota(jnp.int32, (8, 2 * WINDOW), 1)
    mask = (kj >= qi) & (kj <= qi + WINDOW)
    ztail = jnp.zeros((WINDOW - dec, LANES), F32)
    pad8 = lambda a: jnp.concatenate([a, jnp.zeros((8 - dec, a.shape[1]), F32)], axis=0)

    a_parts, o_parts = [], []
    for b0 in range(0, sb, par):
        seqs = range(b0, b0 + par)
        rows8 = [slice((b // per) * 8, (b // per) * 8 + 8) for b in seqs]
        own = [slice((b % per) * dec, (b % per) * dec + dec) for b in seqs]
        new = [slice(b * dec, (b + 1) * dec) for b in seqs]
        att = _swa_multi([q[r] for r in rows8],
                         [jnp.concatenate([ck_ref[b], k[nw], ztail], axis=0) for b, nw in zip(seqs, new)],
                         [jnp.concatenate([cv_ref[b], v[nw], ztail], axis=0) for b, nw in zip(seqs, new)],
                         [mask] * par, sink_ref)
        a_parts += [a[o] for a, o in zip(att, own)]
        states = []
        for b in seqs:
            st = sin_ref[b]
            srows = []
            for h in range(GLA_HEADS):
                parts = []
                if h > 0:
                    parts.append(jnp.zeros((GLA_DK, h * GLA_DV), F32))
                parts.append(st[h])
                if h < GLA_HEADS - 1:
                    parts.append(jnp.zeros((GLA_DK, (GLA_HEADS - 1 - h) * GLA_DV), F32))
                srows.append(jnp.concatenate(parts, axis=1))
            states.append(jnp.concatenate(srows, axis=0))
        intra = _gla_intra_multi([pad8(gq[nw]) for nw in new], [pad8(gk[nw]) for nw in new],
                                 [pad8(gv[nw]) for nw in new], [pad8(logf[nw]) for nw in new], 8)
        outs, states = _gla_inter_multi(intra, states)
        o_parts += [o[:dec] for o in outs]
        for b, s_new in zip(seqs, states):
            for h in range(GLA_HEADS):
                sout_ref[b, h] = s_new[h * GLA_DK:(h + 1) * GLA_DK, h * GLA_DV:(h + 1) * GLA_DV]

    g_out = _gla_out(jnp.concatenate(o_parts, axis=0), gg, gnorm_ref[...])
    mix = jnp.concatenate([jnp.concatenate(a_parts, axis=0), g_out], axis=1).astype(BF16)
    y_ref[...] = x + _nn(mix, wout_ref[...])


def _mixer_sample(x, ct, st, wts, ck, cv, s0, *, dec, sb=16, par=8):
    n = x.shape[0]
    nseq = n // dec
    m = sb * dec
    row = lambda w: pl.BlockSpec((m, w), lambda i: (i, 0))
    seq3 = pl.BlockSpec((sb, WINDOW, LANES), lambda i: (i, 0, 0))
    seq4 = pl.BlockSpec((sb, GLA_HEADS, GLA_DK, GLA_DV), lambda i: (i, 0, 0, 0))
    kern = functools.partial(_mixer_sample_kernel, sb=sb, dec=dec, par=par)
    return pl.pallas_call(
        kern,
        grid=(nseq // sb,),
        in_specs=[row(D_MODEL), row(LANES), row(LANES)] + [_const_spec(w.shape) for w in wts] + [seq3, seq3, seq4],
        out_specs=[row(D_MODEL), row(LANES), row(LANES), seq4],
        out_shape=[jax.ShapeDtypeStruct((n, D_MODEL), F32), jax.ShapeDtypeStruct((n, LANES), F32),
                   jax.ShapeDtypeStruct((n, LANES), F32), jax.ShapeDtypeStruct(s0.shape, F32)],
        compiler_params=pltpu.CompilerParams(dimension_semantics=("arbitrary",),
                                             vmem_limit_bytes=48 * 1024 * 1024),
        name="mixer_sample",
    )(x, ct, st, *wts, ck, cv, s0)


def _shift_kernel(ck_ref, cv_ref, nk_ref, nv_ref, ok_ref, ov_ref, *, new):
    keep = ck_ref.shape[1] - new
    ok_ref[:, :keep] = ck_ref[:, new:]
    ok_ref[:, keep:] = nk_ref[...]
    ov_ref[:, :keep] = cv_ref[:, new:]
    ov_ref[:, keep:] = nv_ref[...]


def _shift_cache(ck, cv, nk, nv, *, sb=32):
    nseq, width = ck.shape
    new = nk.shape[1]
    big = pl.BlockSpec((sb, width), lambda i: (i, 0))
    small = pl.BlockSpec((sb, new), lambda i: (i, 0))
    return pl.pallas_call(
        functools.partial(_shift_kernel, new=new),
        grid=(nseq // sb,),
        in_specs=[big, big, small, small],
        out_specs=[big, big],
        out_shape=[jax.ShapeDtypeStruct(ck.shape, F32), jax.ShapeDtypeStruct(cv.shape, F32)],
        compiler_params=pltpu.CompilerParams(dimension_semantics=("arbitrary",)),
        name="shift_cache",
    )(ck, cv, nk, nv)


def _sort_network(n):
    pairs = []

    def merge(lo, m, r):
        step = 2 * r
        if step < m:
            merge(lo, m, step)
            merge(lo + r, m, step)
            pairs.extend((i, i + r) for i in range(lo + r, lo + m - r, step))
        else:
            pairs.append((lo, lo + r))

    def sort(lo, m):
        if m > 1:
            sort(lo, m // 2)
            sort(lo + m // 2, m // 2)
            merge(lo, m, 1)

    sort(0, n)
    return pairs


def _top16(s):
    t = s.shape[1]
    nv = PEER_NKEYS // 8
    sub8 = lax.broadcasted_iota(jnp.int32, (8, t), 0).astype(F32)
    v = [s[8 * k:8 * k + 8] for k in range(nv)]
    ix = [sub8 + float(8 * k) for k in range(nv)]
    for p, q in _sort_network(nv):
        swap = (v[q] > v[p]) | ((v[q] == v[p]) & (ix[q] < ix[p]))
        v[p], v[q] = jnp.where(swap, v[q], v[p]), jnp.where(swap, v[p], v[q])
        ix[p], ix[q] = jnp.where(swap, ix[q], ix[p]), jnp.where(swap, ix[p], ix[q])
    vals, idxs = [], []
    for n in range(PEER_TOPK):
        m = jnp.max(v[0], axis=0, keepdims=True)
        imin = jnp.min(jnp.where(v[0] == m, ix[0], float(PEER_NKEYS)), axis=0, keepdims=True)
        hit = ix[0] == imin
        vals.append(m)
        idxs.append(imin)
        for k in range(PEER_TOPK - 1 - n):
            v[k] = jnp.where(hit, v[k + 1], v[k])
            ix[k] = jnp.where(hit, ix[k + 1], ix[k])
    return jnp.concatenate(vals, axis=0), jnp.concatenate(idxs, axis=0)


def _pair_top16(a, ia, b, ib):
    t = a.shape[1]
    sub8 = lax.broadcasted_iota(jnp.int32, (8, t), 0).astype(F32)
    ia_lo = ia[0:8] * float(PEER_NKEYS)
    lv, le = [], []
    for y in range(PEER_TOPK):
        nx = PEER_TOPK // (y + 1)
        val = a[0:8] + b[y:y + 1]
        lv.append(val if nx >= 8 else jnp.where(sub8 < float(nx), val, NEG_INF))
        le.append(ia_lo + ib[y:y + 1])
    hv = a[8:16] + b[0:1]
    he = ia[8:16] * float(PEER_NKEYS) + ib[0:1]
    code_lo = sub8 * float(PEER_TOPK)
    code_hi = (sub8 + 8.0) * float(PEER_TOPK)
    vals, sel = [], []
    for n in range(PEER_TOPK):
        m = jnp.max(jnp.maximum(lv[0], hv), axis=0, keepdims=True)
        cmin = jnp.min(jnp.minimum(jnp.where(lv[0] == m, code_lo, 1e9), jnp.where(hv == m, code_hi, 1e9)),
                       axis=0, keepdims=True)
        hit_lo = code_lo == cmin
        hit_hi = code_hi == cmin
        sel.append(jnp.max(jnp.maximum(jnp.where(hit_lo, le[0], -1.0), jnp.where(hit_hi, he, -1.0)),
                           axis=0, keepdims=True))
        vals.append(m)
        for y in range(PEER_TOPK - 1 - n):
            lv[y] = jnp.where(hit_lo, lv[y + 1], lv[y])
            le[y] = jnp.where(hit_lo, le[y + 1], le[y])
        code_lo = jnp.where(hit_lo, code_lo + 1.0, code_lo)
        hv = jnp.where(hit_hi, NEG_INF, hv)
    return jnp.concatenate(vals, axis=0), jnp.concatenate(sel, axis=0)


def _peer_route_gates_kernel(x_ref, fnorm_ref, wqt_ref, keys_ref, h_ref, w_ref,
                             qt_s, e_s, p_s, i_s, j_s, g_s, t_s, *, tm):
    t = pl.program_id(0)
    cur = lax.rem(t, 2)
    prev = 1 - cur
    group = tm // PEER_HEADS

    @pl.when(t == 0)
    def _():
        i_s[...] = jnp.zeros_like(i_s)
        j_s[...] = jnp.zeros_like(j_s)
        g_s[...] = jnp.zeros_like(g_s)

    hb = _rms(x_ref[...], fnorm_ref[...]).astype(BF16)
    h_ref[...] = hb
    qt_s[...] = _nt(wqt_ref[...], hb)
    sub = lax.broadcasted_iota(jnp.int32, (PEER_NKEYS, LANES), 0).astype(BF16)
    sub_odd = (lax.broadcasted_iota(jnp.int32, (W_ROWS_ODD, LANES), 0) - W_SHIFT).astype(BF16)
    one = jnp.ones((W_ROWS_ODD, LANES), BF16)
    zero = jnp.zeros((W_ROWS_ODD, LANES), BF16)

    def emit(n0, q):
        for i in range(PEER_NKEYS):
            a = t_s[pl.ds((16 * q) * W_PITCH + i, 8, stride=W_PITCH), :]
            b = t_s[pl.ds((16 * q + 8) * W_PITCH + i, 8, stride=W_PITCH), :]
            w_ref[pl.ds(n0 + 16 * q, 16), i * PEER_NKEYS:(i + 1) * PEER_NKEYS] = (
                jnp.concatenate([a, b], axis=0).astype(BF16))

    def head(hh, carry):
        r0 = pl.multiple_of(hh * 2 * PEER_NKEYS, 2 * PEER_NKEYS)
        s0 = _nn(keys_ref[2 * hh], qt_s[pl.ds(r0, PEER_NKEYS), :].astype(BF16))
        s1 = _nn(keys_ref[2 * hh + 1], qt_s[pl.ds(r0 + PEER_NKEYS, PEER_NKEYS), :].astype(BF16))
        n0 = pl.multiple_of(hh * group, group)
        for tok in range(group):
            ri = i_s[prev, pl.ds(n0 + tok, 1), :].astype(BF16)
            rj = j_s[prev, pl.ds(n0 + tok, 1), :].astype(BF16)
            rg = jnp.broadcast_to(g_s[prev, pl.ds(n0 + tok, 1), :].astype(BF16), sub.shape)
            bt = jnp.where(sub == rj, rg, zero[:PEER_NKEYS])
            if tok % 2 == 0:
                at = jnp.where(sub == ri, one[:PEER_NKEYS], zero[:PEER_NKEYS])
                t_s[tok * W_PITCH:tok * W_PITCH + PEER_NKEYS, :] = _nt(at, bt)
            else:
                at = jnp.where(sub_odd == ri, one, zero)
                t_s[tok * W_PITCH - W_SHIFT:tok * W_PITCH - W_SHIFT + W_ROWS_ODD, :] = _nt(at, bt)
        a, ia = _top16(s0)
        for q in range(group // 32):
            emit(n0, 2 * q)
        b, ib = _top16(s1)
        for q in range(group // 32):
            emit(n0, 2 * q + 1)
        c, e = _pair_top16(a, ia, b, ib)
        p = jnp.exp(c - c[0:1])
        rows = pl.ds(pl.multiple_of(hh * PEER_TOPK, PEER_TOPK), PEER_TOPK)
        e_s[rows, :] = e
        p_s[rows, :] = p / jnp.sum(p, axis=0, keepdims=True)
        return carry

    lax.fori_loop(0, PEER_HEADS, head, 0)
    et = jnp.transpose(e_s[...])
    it = jnp.floor(et * (1.0 / PEER_NKEYS))
    i_s[cur] = it
    j_s[cur] = et - it * float(PEER_NKEYS)
    g_s[cur] = jnp.transpose(p_s[...])


def _peer_route_gates(x, fnorm, wq, keys, *, tm=256):
    t = x.shape[0]
    tm = min(tm, t)
    nt = t // tm
    nsel = PEER_HEADS * PEER_TOPK
    n_exp = PEER_NKEYS * PEER_NKEYS
    group = tm // PEER_HEADS
    assert group % 32 == 0
    cur_tile = lambda i: (jnp.minimum(i, nt - 1), 0)
    return pl.pallas_call(
        functools.partial(_peer_route_gates_kernel, tm=tm),
        grid=(nt + 1,),
        in_specs=[pl.BlockSpec((tm, D_MODEL), cur_tile), _const_spec(fnorm.shape), _const_spec(wq.shape),
                  _const_spec(keys.shape)],
        out_specs=[pl.BlockSpec((tm, D_MODEL), cur_tile),
                   pl.BlockSpec((tm, n_exp), lambda i: (jnp.maximum(i - 1, 0), 0))],
        out_shape=[jax.ShapeDtypeStruct((t, D_MODEL), BF16), jax.ShapeDtypeStruct((t, n_exp), BF16)],
        scratch_shapes=[pltpu.VMEM((wq.shape[0], tm), F32), pltpu.VMEM((nsel, tm), F32),
                        pltpu.VMEM((nsel, tm), F32),
                        pltpu.VMEM((2, tm, nsel), F32), pltpu.VMEM((2, tm, nsel), F32),
                        pltpu.VMEM((2, tm, nsel), F32),
                        pltpu.VMEM(((group - 1) * W_PITCH - W_SHIFT + W_ROWS_ODD, LANES), F32)],
        compiler_params=pltpu.CompilerParams(dimension_semantics=("arbitrary",),
                                             vmem_limit_bytes=56 * 1024 * 1024),
        name="peer_route_gates",
    )(x, fnorm, wq, keys)


def _peer_dense_kernel(h_ref, x_ref, w_ref, u_ref, v_ref, y_ref):
    @pl.when(pl.program_id(1) == 0)
    def _():
        y_ref[...] = x_ref[...]

    act = _nt(h_ref[...], u_ref[...].astype(BF16))
    gelu = 0.5 * act * (1.0 + lax.erf(act * (2.0 ** -0.5)))
    z = (w_ref[...].astype(F32) * gelu).astype(BF16)
    y_ref[...] += _nn(z, v_ref[...].astype(BF16))


def _peer_dense(h, x, w, u, v, *, tm=1024, te=1024):
    t = h.shape[0]
    tm = min(tm, t)
    n_exp = u.shape[0]
    tok = pl.BlockSpec((tm, D_MODEL), lambda i, e: (i, 0))
    exp = pl.BlockSpec((te, D_MODEL), lambda i, e: (e, 0))
    return pl.pallas_call(
        _peer_dense_kernel,
        grid=(t // tm, n_exp // te),
        in_specs=[tok, tok, pl.BlockSpec((tm, te), lambda i, e: (i, e)), exp, exp],
        out_specs=tok,
        out_shape=jax.ShapeDtypeStruct((t, D_MODEL), F32),
        compiler_params=pltpu.CompilerParams(dimension_semantics=("arbitrary", "arbitrary"),
                                             vmem_limit_bytes=56 * 1024 * 1024),
        name="peer_dense",
    )(h, x, w, u, v)


def _rope_tables(pos):
    half = ROPE_DIM // 2
    inv_freq = ROPE_THETA ** (-jnp.arange(half, dtype=F32) / half)
    ang = pos.astype(F32)[:, None] * inv_freq[None, :]
    cos, sin = jnp.cos(ang), jnp.sin(ang)
    n = pos.shape[0]
    c64 = jnp.concatenate([cos, cos, jnp.ones((n, HEAD_DIM - ROPE_DIM), F32)], axis=1)
    s64 = jnp.concatenate([-sin, sin, jnp.zeros((n, HEAD_DIM - ROPE_DIM), F32)], axis=1)
    return jnp.tile(c64, (1, 2)), jnp.tile(s64, (1, 2))


def kernel(x_prompt, x_sample, cache_swa_k, cache_swa_v, state_gla, attn_norm, w_in, q_norm, k_norm, attn_sinks,
           w_gate, b_gate, gla_norm, w_out, ffn_norm, peer_wq, peer_keys, peer_u, peer_v):
    depth = w_in.shape[0]
    assert depth == 1 and x_prompt.shape[0] == 1
    seq = x_prompt.shape[1]
    nseq, dec = x_sample.shape[0], x_sample.shape[1]
    xp = x_prompt[0]
    xs = x_sample.reshape(nseq * dec, D_MODEL)
    l = 0

    win = jnp.pad(w_in[l], ((0, 0), (0, C_END - w_in.shape[2]))).astype(BF16)
    qkg = jnp.concatenate([jnp.tile(q_norm[l], SWA_Q_HEADS), jnp.tile(k_norm[l], 2)])[None, :]
    sinks = jnp.broadcast_to(attn_sinks[l][:, None], (SWA_Q_HEADS, LANES))
    wgate = jnp.pad(w_gate[l], ((0, LANES - GLA_LOWRANK), (0, 0))).astype(BF16)
    wts = (attn_norm[l][None, :], win, qkg, sinks, wgate, b_gate[l][None, :],
           jnp.tile(gla_norm[l], GLA_HEADS)[None, :], w_out[l].astype(BF16))
    fnorm = ffn_norm[l][None, :]
    wq = jnp.transpose(peer_wq[l]).astype(BF16)
    keys = peer_keys[l].reshape(PEER_HEADS * 2, PEER_NKEYS, peer_keys.shape[-1]).astype(BF16)

    ct_p, st_p = _rope_tables(jnp.arange(seq, dtype=jnp.int32))
    ct_s, st_s = _rope_tables(PAST_LEN + jnp.arange(nseq * dec, dtype=jnp.int32) % dec)

    xp2, kp, vp, sp = _mixer_prompt(xp, ct_p, st_p, wts)
    hp, wp = _peer_route_gates(xp2, fnorm, wq, keys)
    yp = _peer_dense(hp, xp2, wp, peer_u[l], peer_v[l])

    ck = cache_swa_k[l].reshape(nseq, WINDOW, LANES)
    cv = cache_swa_v[l].reshape(nseq, WINDOW, LANES)
    xs2, nk, nv, ss = _mixer_sample(xs, ct_s, st_s, wts, ck, cv, state_gla[l], dec=dec)
    ck2, cv2 = _shift_cache(ck.reshape(nseq, WINDOW * LANES), cv.reshape(nseq, WINDOW * LANES),
                            nk.reshape(nseq, dec * LANES), nv.reshape(nseq, dec * LANES))
    hs, ws = _peer_route_gates(xs2, fnorm, wq, keys)
    ys = _peer_dense(hs, xs2, ws, peer_u[l], peer_v[l])

    kv_shape = (1, 1, WINDOW, 2, HEAD_DIM)
    return (yp[None], ys.reshape(nseq, dec, D_MODEL),
            kp.reshape(kv_shape), vp.reshape(kv_shape), sp[None, None],
            ck2.reshape(1, nseq, WINDOW, 2, HEAD_DIM), cv2.reshape(1, nseq, WINDOW, 2, HEAD_DIM), ss[None])
```

```python
import functools

import jax
import jax.numpy as jnp
from jax import lax
from jax.experimental import pallas as pl
from jax.experimental.pallas import tpu as pltpu

F32 = jnp.float32
BF16 = jnp.bfloat16

D_MODEL = 1024
HEAD_DIM = 64
SWA_Q_HEADS = 8
WINDOW = 128
ROPE_THETA = 500000.0
ROPE_DIM = 16
PAST_LEN = 16384
GLA_HEADS = 4
GLA_DK = 64
GLA_DV = 128
GLA_LOWRANK = 16
GLA_GATE_NORM = 16.0
PEER_HEADS = 8
PEER_NKEYS = 128
PEER_TOPK = 16
NORM_EPS = 1e-6

LANES = 128
QW = SWA_Q_HEADS * HEAD_DIM
GK = GLA_HEADS * GLA_DK
GV = GLA_HEADS * GLA_DV
C_Q, C_K, C_V, C_GQ, C_GK, C_GV, C_GG, C_GA, C_END = 0, 512, 640, 768, 1024, 1280, 1792, 2304, 2432
GLA_PAD = 128
W_PITCH = 132
W_SHIFT = W_PITCH % 8
W_ROWS_ODD = 144
NEG_INF = float("-inf")


def _nn(a, b, precision=None):
    return jnp.dot(a, b, preferred_element_type=F32, precision=precision)


def _nt(a, b):
    return lax.dot_general(a, b, (((1,), (1,)), ((), ())), preferred_element_type=F32)


def _rms(x, gain):
    ms = jnp.mean(x * x, axis=-1, keepdims=True)
    return x * lax.rsqrt(ms + NORM_EPS) * gain


def _head_norm_rope(x, gain, ctab, stab):
    lane = lax.broadcasted_iota(jnp.int32, x.shape, 1)
    lo = lane < HEAD_DIM
    sq = x * x
    ms_lo = jnp.sum(jnp.where(lo, sq, 0.0), axis=-1, keepdims=True)
    ms_hi = jnp.sum(jnp.where(lo, 0.0, sq), axis=-1, keepdims=True)
    ms = jnp.where(lo, ms_lo, ms_hi) * (1.0 / HEAD_DIM)
    xn = x * lax.rsqrt(ms + NORM_EPS) * gain
    first = (lane & (HEAD_DIM - 1)) < (ROPE_DIM // 2)
    partner = jnp.where(first, pltpu.roll(xn, LANES - ROPE_DIM // 2, 1), pltpu.roll(xn, ROPE_DIM // 2, 1))
    return xn * ctab + partner * stab


def _project(x, ct, st, anorm, win, qkg, wgate, bgate):
    h = _rms(x, anorm).astype(BF16)
    p = _nn(h, win)
    qk = [
        _head_norm_rope(p[:, c * LANES:(c + 1) * LANES], qkg[:, c * LANES:(c + 1) * LANES], ct, st)
        for c in range(C_V // LANES)
    ]
    q = jnp.concatenate(qk[:4], axis=1) * (HEAD_DIM ** -0.5)
    k = qk[4]
    v = p[:, C_V:C_GQ]
    gq = p[:, C_GQ:C_GK] * (GLA_DK ** -0.5)
    gk = p[:, C_GK:C_GV]
    gv = p[:, C_GV:C_GG]
    gg = p[:, C_GG:C_GA]
    z = _nn(p[:, C_GA:C_END].astype(BF16), wgate) + bgate
    logf = (jnp.minimum(z, 0.0) - jnp.log(1.0 + jnp.exp(-jnp.abs(z)))) * (1.0 / GLA_GATE_NORM)
    return q, k, v, gq, gk, gv, gg, logf


def _swa_multi(qs, k2s, v2s, masks, sink_ref):
    n = len(qs)
    m = qs[0].shape[0]
    nk = k2s[0].shape[0]
    lo = lax.broadcasted_iota(jnp.int32, (nk, LANES), 1) < HEAD_DIM
    olane = lax.broadcasted_iota(jnp.int32, (2 * m, LANES), 1) < HEAD_DIM
    top = lax.broadcasted_iota(jnp.int32, (2 * m, 1), 0) < m
    zero = jnp.zeros((nk, LANES), F32)
    kexp, vexp = {}, {}
    for p in range(n):
        krot = pltpu.roll(k2s[p], HEAD_DIM, 1)
        vrot = pltpu.roll(v2s[p], HEAD_DIM, 1)
        for g in range(2):
            ka, kb = (k2s[p], krot) if g == 0 else (krot, k2s[p])
            va, vb = (v2s[p], vrot) if g == 0 else (vrot, v2s[p])
            kexp[p, g] = jnp.concatenate([jnp.where(lo, ka, zero), jnp.where(lo, zero, kb)], axis=0).astype(BF16)
            vexp[p, g] = jnp.concatenate([jnp.where(lo, va, zero), jnp.where(lo, zero, vb)], axis=0).astype(BF16)
    s = {}
    for p in range(n):
        for g in range(2):
            qq = jnp.concatenate([qs[p][:, (2 * g) * LANES:(2 * g + 1) * LANES],
                                  qs[p][:, (2 * g + 1) * LANES:(2 * g + 2) * LANES]], axis=0).astype(BF16)
            s[p, g] = _nt(qq, kexp[p, g])
    probs, scale = {}, {}
    for p in range(n):
        mask2 = jnp.concatenate([masks[p], masks[p]], axis=0)
        for g in range(2):
            ps, rs = [], []
            for hh in range(2):
                sh = jnp.where(mask2, s[p, g][:, hh * nk:(hh + 1) * nk], NEG_INF)
                ha, hb = 4 * g + hh, 4 * g + 2 + hh
                sink = jnp.where(top, sink_ref[ha:ha + 1, 0:1], sink_ref[hb:hb + 1, 0:1])
                mx = jnp.maximum(jnp.max(sh, axis=-1, keepdims=True), sink)
                pe = jnp.exp(sh - mx)
                den = jnp.sum(pe, axis=-1, keepdims=True) + jnp.exp(sink - mx)
                ps.append(pe)
                rs.append(1.0 / den)
            probs[p, g] = jnp.concatenate(ps, axis=1).astype(BF16)
            scale[p, g] = jnp.where(olane, rs[0], rs[1])
    outs = []
    for p in range(n):
        o = [_nn(probs[p, g], vexp[p, g]) * scale[p, g] for g in range(2)]
        outs.append(jnp.concatenate([o[0][:m], o[0][m:], o[1][:m], o[1][m:]], axis=1))
    return outs


def _zpad(a, rows):
    if a.shape[0] == rows:
        return a
    return jnp.concatenate([a, jnp.zeros((rows - a.shape[0], a.shape[1]), a.dtype)], axis=0)


def _gla_intra_multi(qs, ks, vs, fs, sub):
    n = len(qs)
    c = qs[0].shape[0]
    nsub = c // sub
    shift = sub.bit_length() - 1
    r = lax.broadcasted_iota(jnp.int32, (c, GLA_PAD), 0)
    cc = lax.broadcasted_iota(jnp.int32, (c, GLA_PAD), 1)
    causal = cc <= r
    tri2 = jnp.concatenate([causal, causal & (cc >= ((r >> shift) << shift))], axis=0).astype(BF16)
    lane_k = lax.broadcasted_iota(jnp.int32, (c, GK), 1) >> 6
    lane_v = lax.broadcasted_iota(jnp.int32, (c, GV), 1) >> 7
    rows = lax.broadcasted_iota(jnp.int32, (c, GK), 0)
    acol = lax.broadcasted_iota(jnp.int32, (c, GLA_HEADS * GLA_PAD), 1) & (GLA_PAD - 1)
    arow = lax.broadcasted_iota(jnp.int32, (c, GLA_HEADS * GLA_PAD), 0)
    amask = (acol <= arow) & (acol >= ((arow >> shift) << shift))

    def expand_k(kt):
        return jnp.concatenate(
            [_zpad(jnp.where(lane_k == h, kt, 0.0), GLA_PAD) for h in range(GLA_HEADS)], axis=0).astype(BF16)

    b, bl, bcol = [], [], []
    for p in range(n):
        fp = _zpad(fs[p], GLA_PAD)
        f_hi = fp.astype(BF16)
        r1 = fp - f_hi.astype(F32)
        f_mid = r1.astype(BF16)
        f_lo = (r1 - f_mid.astype(F32)).astype(BF16)
        cs = _nn(tri2, jnp.concatenate([f_hi, f_mid, f_lo], axis=1))
        tot = (cs[:, :GK] + cs[:, GK:2 * GK]) + cs[:, 2 * GK:]
        b.append(tot[:c])
        bl.append(tot[c:])
        bcol.append(jnp.sum(jnp.transpose(fp), axis=1, keepdims=True))
    qd, kd, qo, ko, qg, kbt, vexp, vpad = [], [], [], [], [], [], [], []
    for p in range(n):
        q, k, v = qs[p], ks[p], vs[p]
        qd.append((q * jnp.exp(bl[p])).astype(BF16))
        kd.append(expand_k(k * jnp.exp(-bl[p])))
        if nsub > 1:
            qj, kj = [], []
            for j in range(nsub - 1):
                e = (j + 1) * sub
                bj = b[p][e - 1:e]
                qj.append(jnp.where(rows >= e, q * jnp.exp(jnp.minimum(b[p] - bj, 0.0)), 0.0).astype(BF16))
                kj.append(expand_k(jnp.where((rows >= e - sub) & (rows < e),
                                             k * jnp.exp(jnp.minimum(bj - b[p], 0.0)), 0.0)))
            qo.append(jnp.concatenate(qj, axis=1))
            ko.append(jnp.concatenate(kj, axis=1))
        qg.append((q * jnp.exp(b[p])).astype(BF16))
        kbt.append(jnp.transpose(_zpad(k * jnp.exp(b[p][c - 1:c] - b[p]), GLA_PAD)).astype(BF16))
        vexp.append(jnp.concatenate(
            [_zpad(jnp.where(lane_v == h, v, 0.0), GLA_PAD) for h in range(GLA_HEADS)], axis=0).astype(BF16))
        vpad.append(_zpad(v, GLA_PAD).astype(BF16))
    a = []
    for p in range(n):
        ap = jnp.where(amask, _nt(qd[p], kd[p]), 0.0)
        if nsub > 1:
            ap = ap + _nt(qo[p], ko[p])
        a.append(ap.astype(BF16))
    return [(_nn(a[p], vexp[p]), qg[p], kbt[p], vpad[p], bcol[p]) for p in range(n)]


def _gla_inter_multi(parts, states):
    srow = lax.broadcasted_iota(jnp.int32, (GK, GV), 0) >> 6
    scol = lax.broadcasted_iota(jnp.int32, (GK, GV), 1) >> 7
    diag = srow == scol
    outs = [o + _nn(qg, s.astype(BF16)) for (o, qg, _, _, _), s in zip(parts, states)]
    new = [s * jnp.exp(bcol) + jnp.where(diag, _nn(kbt, vp), 0.0) for (_, _, kbt, vp, bcol), s in zip(parts, states)]
    return outs, new


def _gla_out(o, gg, gnorm):
    outs = []
    for h in range(GLA_HEADS):
        oh = o[:, h * GLA_DV:(h + 1) * GLA_DV]
        ms = jnp.mean(oh * oh, axis=-1, keepdims=True)
        outs.append(oh * lax.rsqrt(ms + NORM_EPS))
    on = jnp.concatenate(outs, axis=1) * gnorm
    return on * (gg / (1.0 + jnp.exp(-gg)))


def _mixer_prompt_kernel(x_ref, ct_ref, st_ref, anorm_ref, win_ref, qkg_ref, sink_ref, wgate_ref, bgate_ref,
                         gnorm_ref, wout_ref, y_ref, klast_ref, vlast_ref, sfin_ref,
                         kprev_ref, vprev_ref, s_ref, *, tm, chunk, sub):
    i = pl.program_id(0)

    @pl.when(i == 0)
    def _():
        kprev_ref[...] = jnp.zeros_like(kprev_ref)
        vprev_ref[...] = jnp.zeros_like(vprev_ref)
        s_ref[...] = jnp.zeros_like(s_ref)

    x = x_ref[...]
    q, k, v, gq, gk, gv, gg, logf = _project(
        x, ct_ref[...], st_ref[...], anorm_ref[...], win_ref[...], qkg_ref[...], wgate_ref[...], bgate_ref[...])

    kall = jnp.concatenate([kprev_ref[...], k], axis=0)
    vall = jnp.concatenate([vprev_ref[...], v], axis=0)
    qi = lax.broadcasted_iota(jnp.int32, (WINDOW, 2 * WINDOW), 0)
    kj = lax.broadcasted_iota(jnp.int32, (WINDOW, 2 * WINDOW), 1)
    band = (kj >= qi) & (kj <= qi + WINDOW)
    nb = tm // WINDOW
    masks = [band & (kj >= jnp.where(i == 0, WINDOW, 0)) if j == 0 else band for j in range(nb)]
    blocks = _swa_multi([q[j * WINDOW:(j + 1) * WINDOW] for j in range(nb)],
                        [kall[j * WINDOW:(j + 2) * WINDOW] for j in range(nb)],
                        [vall[j * WINDOW:(j + 2) * WINDOW] for j in range(nb)], masks, sink_ref)
    a_out = jnp.concatenate(blocks, axis=0)
    kprev_ref[...] = k[tm - WINDOW:]
    vprev_ref[...] = v[tm - WINDOW:]
    klast_ref[...] = k[tm - WINDOW:]
    vlast_ref[...] = v[tm - WINDOW:]

    sls = [slice(c * chunk, (c + 1) * chunk) for c in range(tm // chunk)]
    parts = _gla_intra_multi([gq[sl] for sl in sls], [gk[sl] for sl in sls], [gv[sl] for sl in sls],
                             [logf[sl] for sl in sls], sub)
    s = s_ref[...]
    os_ = []
    for part in parts:
        (o,), (s,) = _gla_inter_multi([part], [s])
        os_.append(o)
    s_ref[...] = s
    for h in range(GLA_HEADS):
        sfin_ref[h] = s[h * GLA_DK:(h + 1) * GLA_DK, h * GLA_DV:(h + 1) * GLA_DV]
    g_out = _gla_out(jnp.concatenate(os_, axis=0), gg, gnorm_ref[...])

    mix = jnp.concatenate([a_out, g_out], axis=1).astype(BF16)
    y_ref[...] = x + _nn(mix, wout_ref[...])


def _const_spec(shape):
    return pl.BlockSpec(shape, lambda *_: (0,) * len(shape))


def _mixer_prompt(x, ct, st, wts, *, tm=256, chunk=128, sub=32):
    t = x.shape[0]
    tm = min(tm, t)
    row = lambda w: pl.BlockSpec((tm, w), lambda i: (i, 0))
    kern = functools.partial(_mixer_prompt_kernel, tm=tm, chunk=chunk, sub=sub)
    return pl.pallas_call(
        kern,
        grid=(t // tm,),
        in_specs=[row(D_MODEL), row(LANES), row(LANES)] + [_const_spec(w.shape) for w in wts],
        out_specs=[row(D_MODEL), _const_spec((WINDOW, LANES)), _const_spec((WINDOW, LANES)),
                   _const_spec((GLA_HEADS, GLA_DK, GLA_DV))],
        out_shape=[jax.ShapeDtypeStruct((t, D_MODEL), F32), jax.ShapeDtypeStruct((WINDOW, LANES), F32),
                   jax.ShapeDtypeStruct((WINDOW, LANES), F32),
                   jax.ShapeDtypeStruct((GLA_HEADS, GLA_DK, GLA_DV), F32)],
        scratch_shapes=[pltpu.VMEM((WINDOW, LANES), F32), pltpu.VMEM((WINDOW, LANES), F32),
                        pltpu.VMEM((GK, GV), F32)],
        compiler_params=pltpu.CompilerParams(dimension_semantics=("arbitrary",),
                                             vmem_limit_bytes=48 * 1024 * 1024),
        name="mixer_prompt",
    )(x, ct, st, *wts)


def _mixer_sample_kernel(x_ref, ct_ref, st_ref, anorm_ref, win_ref, qkg_ref, sink_ref, wgate_ref, bgate_ref,
                         gnorm_ref, wout_ref, ck_ref, cv_ref, sin_ref,
                         y_ref, kout_ref, vout_ref, sout_ref, *, sb, dec, par):
    x = x_ref[...]
    q, k, v, gq, gk, gv, gg, logf = _project(
        x, ct_ref[...], st_ref[...], anorm_ref[...], win_ref[...], qkg_ref[...], wgate_ref[...], bgate_ref[...])
    for b in range(sb):
        kout_ref[b] = jnp.concatenate([ck_ref[b, dec:, :], k[b * dec:(b + 1) * dec]], axis=0)
        vout_ref[b] = jnp.concatenate([cv_ref[b, dec:, :], v[b * dec:(b + 1) * dec]], axis=0)

    per = 8 // dec
    qi = lax.broadcasted_iota(jnp.int32, (8, 2 * WINDOW), 0) & (dec - 1)
    kj = lax.broadcasted_iota(jnp.int32, (8, 2 * WINDOW), 1)
    mask = (kj >= qi) & (kj <= qi + WINDOW)
    ztail = jnp.zeros((WINDOW - dec, LANES), F32)
    pad8 = lambda a: jnp.concatenate([a, jnp.zeros((8 - dec, a.shape[1]), F32)], axis=0)

    a_parts, o_parts = [], []
    for b0 in range(0, sb, par):
        seqs = range(b0, b0 + par)
        rows8 = [slice((b // per) * 8, (b // per) * 8 + 8) for b in seqs]
        own = [slice((b % per) * dec, (b % per) * dec + dec) for b in seqs]
        new = [slice(b * dec, (b + 1) * dec) for b in seqs]
        att = _swa_multi([q[r] for r in rows8],
                         [jnp.concatenate([ck_ref[b], k[nw], ztail], axis=0) for b, nw in zip(seqs, new)],
                         [jnp.concatenate([cv_ref[b], v[nw], ztail], axis=0) for b, nw in zip(seqs, new)],
                         [mask] * par, sink_ref)
        a_parts += [a[o] for a, o in zip(att, own)]
        states = []
        for b in seqs:
            st = sin_ref[b]
            srows = []
            for h in range(GLA_HEADS):
                parts = []
                if h > 0:
                    parts.append(jnp.zeros((GLA_DK, h * GLA_DV), F32))
                parts.append(st[h])
                if h < GLA_HEADS - 1:
                    parts.append(jnp.zeros((GLA_DK, (GLA_HEADS - 1 - h) * GLA_DV), F32))
                srows.append(jnp.concatenate(parts, axis=1))
            states.append(jnp.concatenate(srows, axis=0))
        intra = _gla_intra_multi([pad8(gq[nw]) for nw in new], [pad8(gk[nw]) for nw in new],
                                 [pad8(gv[nw]) for nw in new], [pad8(logf[nw]) for nw in new], 8)
        outs, states = _gla_inter_multi(intra, states)
        o_parts += [o[:dec] for o in outs]
        for b, s_new in zip(seqs, states):
            for h in range(GLA_HEADS):
                sout_ref[b, h] = s_new[h * GLA_DK:(h + 1) * GLA_DK, h * GLA_DV:(h + 1) * GLA_DV]

    g_out = _gla_out(jnp.concatenate(o_parts, axis=0), gg, gnorm_ref[...])
    mix = jnp.concatenate([jnp.concatenate(a_parts, axis=0), g_out], axis=1).astype(BF16)
    y_ref[...] = x + _nn(mix, wout_ref[...])


def _mixer_sample(x, ct, st, wts, ck, cv, s0, *, dec, sb=16, par=8):
    n = x.shape[0]
    nseq = n // dec
    m = sb * dec
    row = lambda w: pl.BlockSpec((m, w), lambda i: (i, 0))
    seq3 = pl.BlockSpec((sb, WINDOW, LANES), lambda i: (i, 0, 0))
    seq4 = pl.BlockSpec((sb, GLA_HEADS, GLA_DK, GLA_DV), lambda i: (i, 0, 0, 0))
    kern = functools.partial(_mixer_sample_kernel, sb=sb, dec=dec, par=par)
    return pl.pallas_call(
        kern,
        grid=(nseq // sb,),
        in_specs=[row(D_MODEL), row(LANES), row(LANES)] + [_const_spec(w.shape) for w in wts] + [seq3, seq3, seq4],
        out_specs=[row(D_MODEL), seq3, seq3, seq4],
        out_shape=[jax.ShapeDtypeStruct((n, D_MODEL), F32), jax.ShapeDtypeStruct(ck.shape, F32),
                   jax.ShapeDtypeStruct(cv.shape, F32), jax.ShapeDtypeStruct(s0.shape, F32)],
        compiler_params=pltpu.CompilerParams(dimension_semantics=("arbitrary",),
                                             vmem_limit_bytes=48 * 1024 * 1024),
        name="mixer_sample",
    )(x, ct, st, *wts, ck, cv, s0)


def _sort_network(n):
    pairs = []

    def merge(lo, m, r):
        step = 2 * r
        if step < m:
            merge(lo, m, step)
            merge(lo + r, m, step)
            pairs.extend((i, i + r) for i in range(lo + r, lo + m - r, step))
        else:
            pairs.append((lo, lo + r))

    def sort(lo, m):
        if m > 1:
            sort(lo, m // 2)
            sort(lo + m // 2, m // 2)
            merge(lo, m, 1)

    sort(0, n)
    return pairs


def _top16(s):
    t = s.shape[1]
    nv = PEER_NKEYS // 8
    sub8 = lax.broadcasted_iota(jnp.int32, (8, t), 0).astype(F32)
    v = [s[8 * k:8 * k + 8] for k in range(nv)]
    ix = [sub8 + float(8 * k) for k in range(nv)]
    for p, q in _sort_network(nv):
        swap = (v[q] > v[p]) | ((v[q] == v[p]) & (ix[q] < ix[p]))
        v[p], v[q] = jnp.where(swap, v[q], v[p]), jnp.where(swap, v[p], v[q])
        ix[p], ix[q] = jnp.where(swap, ix[q], ix[p]), jnp.where(swap, ix[p], ix[q])
    vals, idxs = [], []
    for n in range(PEER_TOPK):
        m = jnp.max(v[0], axis=0, keepdims=True)
        imin = jnp.min(jnp.where(v[0] == m, ix[0], float(PEER_NKEYS)), axis=0, keepdims=True)
        hit = ix[0] == imin
        vals.append(m)
        idxs.append(imin)
        for k in range(PEER_TOPK - 1 - n):
            v[k] = jnp.where(hit, v[k + 1], v[k])
            ix[k] = jnp.where(hit, ix[k + 1], ix[k])
    return jnp.concatenate(vals, axis=0), jnp.concatenate(idxs, axis=0)


def _pair_top16(a, ia, b, ib):
    t = a.shape[1]
    sub8 = lax.broadcasted_iota(jnp.int32, (8, t), 0).astype(F32)
    ia_lo = ia[0:8] * float(PEER_NKEYS)
    lv, le = [], []
    for y in range(PEER_TOPK):
        nx = PEER_TOPK // (y + 1)
        val = a[0:8] + b[y:y + 1]
        lv.append(val if nx >= 8 else jnp.where(sub8 < float(nx), val, NEG_INF))
        le.append(ia_lo + ib[y:y + 1])
    hv = a[8:16] + b[0:1]
    he = ia[8:16] * float(PEER_NKEYS) + ib[0:1]
    code_lo = sub8 * float(PEER_TOPK)
    code_hi = (sub8 + 8.0) * float(PEER_TOPK)
    vals, sel = [], []
    for n in range(PEER_TOPK):
        m = jnp.max(jnp.maximum(lv[0], hv), axis=0, keepdims=True)
        cmin = jnp.min(jnp.minimum(jnp.where(lv[0] == m, code_lo, 1e9), jnp.where(hv == m, code_hi, 1e9)),
                       axis=0, keepdims=True)
        hit_lo = code_lo == cmin
        hit_hi = code_hi == cmin
        sel.append(jnp.max(jnp.maximum(jnp.where(hit_lo, le[0], -1.0), jnp.where(hit_hi, he, -1.0)),
                           axis=0, keepdims=True))
        vals.append(m)
        for y in range(PEER_TOPK - 1 - n):
            lv[y] = jnp.where(hit_lo, lv[y + 1], lv[y])
            le[y] = jnp.where(hit_lo, le[y + 1], le[y])
        code_lo = jnp.where(hit_lo, code_lo + 1.0, code_lo)
        hv = jnp.where(hit_hi, NEG_INF, hv)
    return jnp.concatenate(vals, axis=0), jnp.concatenate(sel, axis=0)


def _peer_route_gates_kernel(x_ref, fnorm_ref, wqt_ref, keys_ref, h_ref, w_ref,
                             qt_s, e_s, p_s, i_s, j_s, g_s, t_s, *, tm):
    t = pl.program_id(0)
    cur = lax.rem(t, 2)
    prev = 1 - cur
    group = tm // PEER_HEADS

    @pl.when(t == 0)
    def _():
        i_s[...] = jnp.zeros_like(i_s)
        j_s[...] = jnp.zeros_like(j_s)
        g_s[...] = jnp.zeros_like(g_s)

    hb = _rms(x_ref[...], fnorm_ref[...]).astype(BF16)
    h_ref[...] = hb
    qt_s[...] = _nt(wqt_ref[...], hb)
    sub = lax.broadcasted_iota(jnp.int32, (PEER_NKEYS, LANES), 0).astype(BF16)
    sub_odd = (lax.broadcasted_iota(jnp.int32, (W_ROWS_ODD, LANES), 0) - W_SHIFT).astype(BF16)
    one = jnp.ones((W_ROWS_ODD, LANES), BF16)
    zero = jnp.zeros((W_ROWS_ODD, LANES), BF16)

    def emit(n0, q):
        for i in range(PEER_NKEYS):
            a = t_s[pl.ds((16 * q) * W_PITCH + i, 8, stride=W_PITCH), :]
            b = t_s[pl.ds((16 * q + 8) * W_PITCH + i, 8, stride=W_PITCH), :]
            w_ref[pl.ds(n0 + 16 * q, 16), i * PEER_NKEYS:(i + 1) * PEER_NKEYS] = (
                jnp.concatenate([a, b], axis=0).astype(BF16))

    def head(hh, carry):
        r0 = pl.multiple_of(hh * 2 * PEER_NKEYS, 2 * PEER_NKEYS)
        s0 = _nn(keys_ref[2 * hh], qt_s[pl.ds(r0, PEER_NKEYS), :].astype(BF16))
        s1 = _nn(keys_ref[2 * hh + 1], qt_s[pl.ds(r0 + PEER_NKEYS, PEER_NKEYS), :].astype(BF16))
        n0 = pl.multiple_of(hh * group, group)
        for tok in range(group):
            ri = i_s[prev, pl.ds(n0 + tok, 1), :].astype(BF16)
            rj = j_s[prev, pl.ds(n0 + tok, 1), :].astype(BF16)
            rg = jnp.broadcast_to(g_s[prev, pl.ds(n0 + tok, 1), :].astype(BF16), sub.shape)
            bt = jnp.where(sub == rj, rg, zero[:PEER_NKEYS])
            if tok % 2 == 0:
                at = jnp.where(sub == ri, one[:PEER_NKEYS], zero[:PEER_NKEYS])
                t_s[tok * W_PITCH:tok * W_PITCH + PEER_NKEYS, :] = _nt(at, bt)
            else:
                at = jnp.where(sub_odd == ri, one, zero)
                t_s[tok * W_PITCH - W_SHIFT:tok * W_PITCH - W_SHIFT + W_ROWS_ODD, :] = _nt(at, bt)
        a, ia = _top16(s0)
        for q in range(group // 32):
            emit(n0, 2 * q)
        b, ib = _top16(s1)
        for q in range(group // 32):
            emit(n0, 2 * q + 1)
        c, e = _pair_top16(a, ia, b, ib)
        p = jnp.exp(c - c[0:1])
        rows = pl.ds(pl.multiple_of(hh * PEER_TOPK, PEER_TOPK), PEER_TOPK)
        e_s[rows, :] = e
        p_s[rows, :] = p / jnp.sum(p, axis=0, keepdims=True)
        return carry

    lax.fori_loop(0, PEER_HEADS, head, 0)
    et = jnp.transpose(e_s[...])
    it = jnp.floor(et * (1.0 / PEER_NKEYS))
    i_s[cur] = it
    j_s[cur] = et - it * float(PEER_NKEYS)
    g_s[cur] = jnp.transpose(p_s[...])


def _peer_route_gates(x, fnorm, wq, keys, *, tm=256):
    t = x.shape[0]
    tm = min(tm, t)
    nt = t // tm
    nsel = PEER_HEADS * PEER_TOPK
    n_exp = PEER_NKEYS * PEER_NKEYS
    group = tm // PEER_HEADS
    assert group % 32 == 0
    cur_tile = lambda i: (jnp.minimum(i, nt - 1), 0)
    return pl.pallas_call(
        functools.partial(_peer_route_gates_kernel, tm=tm),
        grid=(nt + 1,),
        in_specs=[pl.BlockSpec((tm, D_MODEL), cur_tile), _const_spec(fnorm.shape), _const_spec(wq.shape),
                  _const_spec(keys.shape)],
        out_specs=[pl.BlockSpec((tm, D_MODEL), cur_tile),
                   pl.BlockSpec((tm, n_exp), lambda i: (jnp.maximum(i - 1, 0), 0))],
        out_shape=[jax.ShapeDtypeStruct((t, D_MODEL), BF16), jax.ShapeDtypeStruct((t, n_exp), BF16)],
        scratch_shapes=[pltpu.VMEM((wq.shape[0], tm), F32), pltpu.VMEM((nsel, tm), F32),
                        pltpu.VMEM((nsel, tm), F32),
                        pltpu.VMEM((2, tm, nsel), F32), pltpu.VMEM((2, tm, nsel), F32),
                        pltpu.VMEM((2, tm, nsel), F32),
                        pltpu.VMEM(((group - 1) * W_PITCH - W_SHIFT + W_ROWS_ODD, LANES), F32)],
        compiler_params=pltpu.CompilerParams(dimension_semantics=("arbitrary",),
                                             vmem_limit_bytes=56 * 1024 * 1024),
        name="peer_route_gates",
    )(x, fnorm, wq, keys)


def _peer_dense_kernel(h_ref, x_ref, w_ref, u_ref, v_ref, y_ref):
    @pl.when(pl.program_id(1) == 0)
    def _():
        y_ref[...] = x_ref[...]

    act = _nt(h_ref[...], u_ref[...].astype(BF16))
    gelu = 0.5 * act * (1.0 + lax.erf(act * (2.0 ** -0.5)))
    z = (w_ref[...].astype(F32) * gelu).astype(BF16)
    y_ref[...] += _nn(z, v_ref[...].astype(BF16))


def _peer_dense(h, x, w, u, v, *, tm=1024, te=1024):
    t = h.shape[0]
    tm = min(tm, t)
    n_exp = u.shape[0]
    tok = pl.BlockSpec((tm, D_MODEL), lambda i, e: (i, 0))
    exp = pl.BlockSpec((te, D_MODEL), lambda i, e: (e, 0))
    return pl.pallas_call(
        _peer_dense_kernel,
        grid=(t // tm, n_exp // te),
        in_specs=[tok, tok, pl.BlockSpec((tm, te), lambda i, e: (i, e)), exp, exp],
        out_specs=tok,
        out_shape=jax.ShapeDtypeStruct((t, D_MODEL), F32),
        compiler_params=pltpu.CompilerParams(dimension_semantics=("arbitrary", "arbitrary"),
                                             vmem_limit_bytes=56 * 1024 * 1024),
        name="peer_dense",
    )(h, x, w, u, v)


def _rope_tables(pos):
    half = ROPE_DIM // 2
    inv_freq = ROPE_THETA ** (-jnp.arange(half, dtype=F32) / half)
    ang = pos.astype(F32)[:, None] * inv_freq[None, :]
    cos, sin = jnp.cos(ang), jnp.sin(ang)
    n = pos.shape[0]
    c64 = jnp.concatenate([cos, cos, jnp.ones((n, HEAD_DIM - ROPE_DIM), F32)], axis=1)
    s64 = jnp.concatenate([-sin, sin, jnp.zeros((n, HEAD_DIM - ROPE_DIM), F32)], axis=1)
    return jnp.tile(c64, (1, 2)), jnp.tile(s64, (1, 2))


def kernel(x_prompt, x_sample, cache_swa_k, cache_swa_v, state_gla, attn_norm, w_in, q_norm, k_norm, attn_sinks,
           w_gate, b_gate, gla_norm, w_out, ffn_norm, peer_wq, peer_keys, peer_u, peer_v):
    depth = w_in.shape[0]
    assert depth == 1 and x_prompt.shape[0] == 1
    seq = x_prompt.shape[1]
    nseq, dec = x_sample.shape[0], x_sample.shape[1]
    xp = x_prompt[0]
    xs = x_sample.reshape(nseq * dec, D_MODEL)
    l = 0

    win = jnp.pad(w_in[l], ((0, 0), (0, C_END - w_in.shape[2]))).astype(BF16)
    qkg = jnp.concatenate([jnp.tile(q_norm[l], SWA_Q_HEADS), jnp.tile(k_norm[l], 2)])[None, :]
    sinks = jnp.broadcast_to(attn_sinks[l][:, None], (SWA_Q_HEADS, LANES))
    wgate = jnp.pad(w_gate[l], ((0, LANES - GLA_LOWRANK), (0, 0))).astype(BF16)
    wts = (attn_norm[l][None, :], win, qkg, sinks, wgate, b_gate[l][None, :],
           jnp.tile(gla_norm[l], GLA_HEADS)[None, :], w_out[l].astype(BF16))
    fnorm = ffn_norm[l][None, :]
    wq = jnp.transpose(peer_wq[l]).astype(BF16)
    keys = peer_keys[l].reshape(PEER_HEADS * 2, PEER_NKEYS, peer_keys.shape[-1]).astype(BF16)

    ct_p, st_p = _rope_tables(jnp.arange(seq, dtype=jnp.int32))
    ct_s, st_s = _rope_tables(PAST_LEN + jnp.arange(nseq * dec, dtype=jnp.int32) % dec)

    xp2, kp, vp, sp = _mixer_prompt(xp, ct_p, st_p, wts)
    hp, wp = _peer_route_gates(xp2, fnorm, wq, keys)
    yp = _peer_dense(hp, xp2, wp, peer_u[l], peer_v[l])

    ck = cache_swa_k[l].reshape(nseq, WINDOW, LANES)
    cv = cache_swa_v[l].reshape(nseq, WINDOW, LANES)
    xs2, ck2, cv2, ss = _mixer_sample(xs, ct_s, st_s, wts, ck, cv, state_gla[l], dec=dec)
    hs, ws = _peer_route_gates(xs2, fnorm, wq, keys)
    ys = _peer_dense(hs, xs2, ws, peer_u[l], peer_v[l])

    kv_shape = (1, 1, WINDOW, 2, HEAD_DIM)
    return (yp[None], ys.reshape(nseq, dec, D_MODEL),
            kp.reshape(kv_shape), vp.reshape(kv_shape), sp[None, None],
            ck2.reshape(1, nseq, WINDOW, 2, HEAD_DIM), cv2.reshape(1, nseq, WINDOW, 2, HEAD_DIM), ss[None])
```

```python
import functools

import jax
import jax.numpy as jnp
from jax import lax
from jax.experimental import pallas as pl
from jax.experimental.pallas import tpu as pltpu

F32 = jnp.float32
BF16 = jnp.bfloat16

D_MODEL = 1024
HEAD_DIM = 64
SWA_Q_HEADS = 8
WINDOW = 128
ROPE_THETA = 500000.0
ROPE_DIM = 16
PAST_LEN = 16384
GLA_HEADS = 4
GLA_DK = 64
GLA_DV = 128
GLA_LOWRANK = 16
GLA_GATE_NORM = 16.0
PEER_HEADS = 8
PEER_NKEYS = 128
PEER_TOPK = 16
NORM_EPS = 1e-6

LANES = 128
QW = SWA_Q_HEADS * HEAD_DIM
GK = GLA_HEADS * GLA_DK
GV = GLA_HEADS * GLA_DV
C_Q, C_K, C_V, C_GQ, C_GK, C_GV, C_GG, C_GA, C_END = 0, 512, 640, 768, 1024, 1280, 1792, 2304, 2432
GLA_PAD = 128
W_PITCH = 132
W_SHIFT = W_PITCH % 8
W_ROWS_ODD = 144
NEG_INF = float("-inf")


def _nn(a, b, precision=None):
    return jnp.dot(a, b, preferred_element_type=F32, precision=precision)


def _nt(a, b):
    return lax.dot_general(a, b, (((1,), (1,)), ((), ())), preferred_element_type=F32)


def _rms(x, gain):
    ms = jnp.mean(x * x, axis=-1, keepdims=True)
    return x * lax.rsqrt(ms + NORM_EPS) * gain


def _head_norm_rope(x, gain, ctab, stab):
    lane = lax.broadcasted_iota(jnp.int32, x.shape, 1)
    lo = lane < HEAD_DIM
    sq = x * x
    ms_lo = jnp.sum(jnp.where(lo, sq, 0.0), axis=-1, keepdims=True)
    ms_hi = jnp.sum(jnp.where(lo, 0.0, sq), axis=-1, keepdims=True)
    ms = jnp.where(lo, ms_lo, ms_hi) * (1.0 / HEAD_DIM)
    xn = x * lax.rsqrt(ms + NORM_EPS) * gain
    first = (lane & (HEAD_DIM - 1)) < (ROPE_DIM // 2)
    partner = jnp.where(first, pltpu.roll(xn, LANES - ROPE_DIM // 2, 1), pltpu.roll(xn, ROPE_DIM // 2, 1))
    return xn * ctab + partner * stab


def _project(x, ct, st, anorm, win, qkg, wgate, bgate):
    h = _rms(x, anorm).astype(BF16)
    p = _nn(h, win)
    qk = [
        _head_norm_rope(p[:, c * LANES:(c + 1) * LANES], qkg[:, c * LANES:(c + 1) * LANES], ct, st)
        for c in range(C_V // LANES)
    ]
    q = jnp.concatenate(qk[:4], axis=1) * (HEAD_DIM ** -0.5)
    k = qk[4]
    v = p[:, C_V:C_GQ]
    gq = p[:, C_GQ:C_GK] * (GLA_DK ** -0.5)
    gk = p[:, C_GK:C_GV]
    gv = p[:, C_GV:C_GG]
    gg = p[:, C_GG:C_GA]
    z = _nn(p[:, C_GA:C_END].astype(BF16), wgate) + bgate
    logf = (jnp.minimum(z, 0.0) - jnp.log(1.0 + jnp.exp(-jnp.abs(z)))) * (1.0 / GLA_GATE_NORM)
    return q, k, v, gq, gk, gv, gg, logf


def _swa_multi(qs, k2s, v2s, masks, sink_ref):
    n = len(qs)
    m = qs[0].shape[0]
    nk = k2s[0].shape[0]
    lo = lax.broadcasted_iota(jnp.int32, (nk, LANES), 1) < HEAD_DIM
    olane = lax.broadcasted_iota(jnp.int32, (2 * m, LANES), 1) < HEAD_DIM
    top = lax.broadcasted_iota(jnp.int32, (2 * m, 1), 0) < m
    zero = jnp.zeros((nk, LANES), F32)
    kexp, vexp = {}, {}
    for p in range(n):
        krot = pltpu.roll(k2s[p], HEAD_DIM, 1)
        vrot = pltpu.roll(v2s[p], HEAD_DIM, 1)
        for g in range(2):
            ka, kb = (k2s[p], krot) if g == 0 else (krot, k2s[p])
            va, vb = (v2s[p], vrot) if g == 0 else (vrot, v2s[p])
            kexp[p, g] = jnp.concatenate([jnp.where(lo, ka, zero), jnp.where(lo, zero, kb)], axis=0).astype(BF16)
            vexp[p, g] = jnp.concatenate([jnp.where(lo, va, zero), jnp.where(lo, zero, vb)], axis=0).astype(BF16)
    s = {}
    for p in range(n):
        for g in range(2):
            qq = jnp.concatenate([qs[p][:, (2 * g) * LANES:(2 * g + 1) * LANES],
                                  qs[p][:, (2 * g + 1) * LANES:(2 * g + 2) * LANES]], axis=0).astype(BF16)
            s[p, g] = _nt(qq, kexp[p, g])
    probs, scale = {}, {}
    for p in range(n):
        mask2 = jnp.concatenate([masks[p], masks[p]], axis=0)
        for g in range(2):
            ps, rs = [], []
            for hh in range(2):
                sh = jnp.where(mask2, s[p, g][:, hh * nk:(hh + 1) * nk], NEG_INF)
                ha, hb = 4 * g + hh, 4 * g + 2 + hh
                sink = jnp.where(top, sink_ref[ha:ha + 1, 0:1], sink_ref[hb:hb + 1, 0:1])
                mx = jnp.maximum(jnp.max(sh, axis=-1, keepdims=True), sink)
                pe = jnp.exp(sh - mx)
                den = jnp.sum(pe, axis=-1, keepdims=True) + jnp.exp(sink - mx)
                ps.append(pe)
                rs.append(1.0 / den)
            probs[p, g] = jnp.concatenate(ps, axis=1).astype(BF16)
            scale[p, g] = jnp.where(olane, rs[0], rs[1])
    outs = []
    for p in range(n):
        o = [_nn(probs[p, g], vexp[p, g]) * scale[p, g] for g in range(2)]
        outs.append(jnp.concatenate([o[0][:m], o[0][m:], o[1][:m], o[1][m:]], axis=1))
    return outs


def _zpad(a, rows):
    if a.shape[0] == rows:
        return a
    return jnp.concatenate([a, jnp.zeros((rows - a.shape[0], a.shape[1]), a.dtype)], axis=0)


def _gla_intra_multi(qs, ks, vs, fs, sub):
    n = len(qs)
    c = qs[0].shape[0]
    nsub = c // sub
    shift = sub.bit_length() - 1
    r = lax.broadcasted_iota(jnp.int32, (c, GLA_PAD), 0)
    cc = lax.broadcasted_iota(jnp.int32, (c, GLA_PAD), 1)
    causal = cc <= r
    tri2 = jnp.concatenate([causal, causal & (cc >= ((r >> shift) << shift))], axis=0).astype(BF16)
    lane_k = lax.broadcasted_iota(jnp.int32, (c, GK), 1) >> 6
    lane_v = lax.broadcasted_iota(jnp.int32, (c, GV), 1) >> 7
    rows = lax.broadcasted_iota(jnp.int32, (c, GK), 0)
    acol = lax.broadcasted_iota(jnp.int32, (c, GLA_HEADS * GLA_PAD), 1) & (GLA_PAD - 1)
    arow = lax.broadcasted_iota(jnp.int32, (c, GLA_HEADS * GLA_PAD), 0)
    amask = (acol <= arow) & (acol >= ((arow >> shift) << shift))

    def expand_k(kt):
        return jnp.concatenate(
            [_zpad(jnp.where(lane_k == h, kt, 0.0), GLA_PAD) for h in range(GLA_HEADS)], axis=0).astype(BF16)

    b, bl, bcol = [], [], []
    for p in range(n):
        fp = _zpad(fs[p], GLA_PAD)
        f_hi = fp.astype(BF16)
        r1 = fp - f_hi.astype(F32)
        f_mid = r1.astype(BF16)
        f_lo = (r1 - f_mid.astype(F32)).astype(BF16)
        cs = _nn(tri2, jnp.concatenate([f_hi, f_mid, f_lo], axis=1))
        tot = (cs[:, :GK] + cs[:, GK:2 * GK]) + cs[:, 2 * GK:]
        b.append(tot[:c])
        bl.append(tot[c:])
        bcol.append(jnp.sum(jnp.transpose(fp), axis=1, keepdims=True))
    qd, kd, qo, ko, qg, kbt, vexp, vpad = [], [], [], [], [], [], [], []
    for p in range(n):
        q, k, v = qs[p], ks[p], vs[p]
        qd.append((q * jnp.exp(bl[p])).astype(BF16))
        kd.append(expand_k(k * jnp.exp(-bl[p])))
        if nsub > 1:
            qj, kj = [], []
            for j in range(nsub - 1):
                e = (j + 1) * sub
                bj = b[p][e - 1:e]
                qj.append(jnp.where(rows >= e, q * jnp.exp(jnp.minimum(b[p] - bj, 0.0)), 0.0).astype(BF16))
                kj.append(expand_k(jnp.where((rows >= e - sub) & (rows < e),
                                             k * jnp.exp(jnp.minimum(bj - b[p], 0.0)), 0.0)))
            qo.append(jnp.concatenate(qj, axis=1))
            ko.append(jnp.concatenate(kj, axis=1))
        qg.append((q * jnp.exp(b[p])).astype(BF16))
        kbt.append(jnp.transpose(_zpad(k * jnp.exp(b[p][c - 1:c] - b[p]), GLA_PAD)).astype(BF16))
        vexp.append(jnp.concatenate(
            [_zpad(jnp.where(lane_v == h, v, 0.0), GLA_PAD) for h in range(GLA_HEADS)], axis=0).astype(BF16))
        vpad.append(_zpad(v, GLA_PAD).astype(BF16))
    a = []
    for p in range(n):
        ap = jnp.where(amask, _nt(qd[p], kd[p]), 0.0)
        if nsub > 1:
            ap = ap + _nt(qo[p], ko[p])
        a.append(ap.astype(BF16))
    return [(_nn(a[p], vexp[p]), qg[p], kbt[p], vpad[p], bcol[p]) for p in range(n)]


def _gla_inter_multi(parts, states):
    srow = lax.broadcasted_iota(jnp.int32, (GK, GV), 0) >> 6
    scol = lax.broadcasted_iota(jnp.int32, (GK, GV), 1) >> 7
    diag = srow == scol
    outs = [o + _nn(qg, s.astype(BF16)) for (o, qg, _, _, _), s in zip(parts, states)]
    new = [s * jnp.exp(bcol) + jnp.where(diag, _nn(kbt, vp), 0.0) for (_, _, kbt, vp, bcol), s in zip(parts, states)]
    return outs, new


def _gla_out(o, gg, gnorm):
    outs = []
    for h in range(GLA_HEADS):
        oh = o[:, h * GLA_DV:(h + 1) * GLA_DV]
        ms = jnp.mean(oh * oh, axis=-1, keepdims=True)
        outs.append(oh * lax.rsqrt(ms + NORM_EPS))
    on = jnp.concatenate(outs, axis=1) * gnorm
    return on * (gg / (1.0 + jnp.exp(-gg)))


def _mixer_prompt_kernel(x_ref, ct_ref, st_ref, anorm_ref, win_ref, qkg_ref, sink_ref, wgate_ref, bgate_ref,
                         gnorm_ref, wout_ref, y_ref, klast_ref, vlast_ref, sfin_ref,
                         kprev_ref, vprev_ref, s_ref, *, tm, chunk, sub):
    i = pl.program_id(0)

    @pl.when(i == 0)
    def _():
        kprev_ref[...] = jnp.zeros_like(kprev_ref)
        vprev_ref[...] = jnp.zeros_like(vprev_ref)
        s_ref[...] = jnp.zeros_like(s_ref)

    x = x_ref[...]
    q, k, v, gq, gk, gv, gg, logf = _project(
        x, ct_ref[...], st_ref[...], anorm_ref[...], win_ref[...], qkg_ref[...], wgate_ref[...], bgate_ref[...])

    kall = jnp.concatenate([kprev_ref[...], k], axis=0)
    vall = jnp.concatenate([vprev_ref[...], v], axis=0)
    qi = lax.broadcasted_iota(jnp.int32, (WINDOW, 2 * WINDOW), 0)
    kj = lax.broadcasted_iota(jnp.int32, (WINDOW, 2 * WINDOW), 1)
    band = (kj >= qi) & (kj <= qi + WINDOW)
    nb = tm // WINDOW
    masks = [band & (kj >= jnp.where(i == 0, WINDOW, 0)) if j == 0 else band for j in range(nb)]
    blocks = _swa_multi([q[j * WINDOW:(j + 1) * WINDOW] for j in range(nb)],
                        [kall[j * WINDOW:(j + 2) * WINDOW] for j in range(nb)],
                        [vall[j * WINDOW:(j + 2) * WINDOW] for j in range(nb)], masks, sink_ref)
    a_out = jnp.concatenate(blocks, axis=0)
    kprev_ref[...] = k[tm - WINDOW:]
    vprev_ref[...] = v[tm - WINDOW:]
    klast_ref[...] = k[tm - WINDOW:]
    vlast_ref[...] = v[tm - WINDOW:]

    sls = [slice(c * chunk, (c + 1) * chunk) for c in range(tm // chunk)]
    parts = _gla_intra_multi([gq[sl] for sl in sls], [gk[sl] for sl in sls], [gv[sl] for sl in sls],
                             [logf[sl] for sl in sls], sub)
    s = s_ref[...]
    os_ = []
    for part in parts:
        (o,), (s,) = _gla_inter_multi([part], [s])
        os_.append(o)
    s_ref[...] = s
    for h in range(GLA_HEADS):
        sfin_ref[h] = s[h * GLA_DK:(h + 1) * GLA_DK, h * GLA_DV:(h + 1) * GLA_DV]
    g_out = _gla_out(jnp.concatenate(os_, axis=0), gg, gnorm_ref[...])

    mix = jnp.concatenate([a_out, g_out], axis=1).astype(BF16)
    y_ref[...] = x + _nn(mix, wout_ref[...])


def _const_spec(shape):
    return pl.BlockSpec(shape, lambda *_: (0,) * len(shape))


def _mixer_prompt(x, ct, st, wts, *, tm=256, chunk=128, sub=32):
    t = x.shape[0]
    tm = min(tm, t)
    row = lambda w: pl.BlockSpec((tm, w), lambda i: (i, 0))
    kern = functools.partial(_mixer_prompt_kernel, tm=tm, chunk=chunk, sub=sub)
    return pl.pallas_call(
        kern,
        grid=(t // tm,),
        in_specs=[row(D_MODEL), row(LANES), row(LANES)] + [_const_spec(w.shape) for w in wts],
        out_specs=[row(D_MODEL), _const_spec((WINDOW, LANES)), _const_spec((WINDOW, LANES)),
                   _const_spec((GLA_HEADS, GLA_DK, GLA_DV))],
        out_shape=[jax.ShapeDtypeStruct((t, D_MODEL), F32), jax.ShapeDtypeStruct((WINDOW, LANES), F32),
                   jax.ShapeDtypeStruct((WINDOW, LANES), F32),
                   jax.ShapeDtypeStruct((GLA_HEADS, GLA_DK, GLA_DV), F32)],
        scratch_shapes=[pltpu.VMEM((WINDOW, LANES), F32), pltpu.VMEM((WINDOW, LANES), F32),
                        pltpu.VMEM((GK, GV), F32)],
        compiler_params=pltpu.CompilerParams(dimension_semantics=("arbitrary",),
                                             vmem_limit_bytes=48 * 1024 * 1024),
        name="mixer_prompt",
    )(x, ct, st, *wts)


def _mixer_sample_kernel(x_ref, ct_ref, st_ref, anorm_ref, win_ref, qkg_ref, sink_ref, wgate_ref, bgate_ref,
                         gnorm_ref, wout_ref, ck_ref, cv_ref, sin_ref,
                         y_ref, kout_ref, vout_ref, sout_ref, *, sb, dec, par):
    x = x_ref[...]
    q, k, v, gq, gk, gv, gg, logf = _project(
        x, ct_ref[...], st_ref[...], anorm_ref[...], win_ref[...], qkg_ref[...], wgate_ref[...], bgate_ref[...])
    for b in range(sb):
        kout_ref[b] = jnp.concatenate([ck_ref[b, dec:, :], k[b * dec:(b + 1) * dec]], axis=0)
        vout_ref[b] = jnp.concatenate([cv_ref[b, dec:, :], v[b * dec:(b + 1) * dec]], axis=0)

    per = 8 // dec
    qi = lax.broadcasted_iota(jnp.int32, (8, 2 * WINDOW), 0) & (dec - 1)
    kj = lax.broadcasted_iota(jnp.int32, (8, 2 * WINDOW), 1)
    mask = (kj >= qi) & (kj <= qi + WINDOW)
    ztail = jnp.zeros((WINDOW - dec, LANES), F32)
    pad8 = lambda a: jnp.concatenate([a, jnp.zeros((8 - dec, a.shape[1]), F32)], axis=0)

    a_parts, o_parts = [], []
    for b0 in range(0, sb, par):
        seqs = range(b0, b0 + par)
        rows8 = [slice((b // per) * 8, (b // per) * 8 + 8) for b in seqs]
        own = [slice((b % per) * dec, (b % per) * dec + dec) for b in seqs]
        new = [slice(b * dec, (b + 1) * dec) for b in seqs]
        att = _swa_multi([q[r] for r in rows8],
                         [jnp.concatenate([ck_ref[b], k[nw], ztail], axis=0) for b, nw in zip(seqs, new)],
                         [jnp.concatenate([cv_ref[b], v[nw], ztail], axis=0) for b, nw in zip(seqs, new)],
                         [mask] * par, sink_ref)
        a_parts += [a[o] for a, o in zip(att, own)]
        states = []
        for b in seqs:
            st = sin_ref[b]
            srows = []
            for h in range(GLA_HEADS):
                parts = []
                if h > 0:
                    parts.append(jnp.zeros((GLA_DK, h * GLA_DV), F32))
                parts.append(st[h])
                if h < GLA_HEADS - 1:
                    parts.append(jnp.zeros((GLA_DK, (GLA_HEADS - 1 - h) * GLA_DV), F32))
                srows.append(jnp.concatenate(parts, axis=1))
            states.append(jnp.concatenate(srows, axis=0))
        intra = _gla_intra_multi([pad8(gq[nw]) for nw in new], [pad8(gk[nw]) for nw in new],
                                 [pad8(gv[nw]) for nw in new], [pad8(logf[nw]) for nw in new], 8)
        outs, states = _gla_inter_multi(intra, states)
        o_parts += [o[:dec] for o in outs]
        for b, s_new in zip(seqs, states):
            for h in range(GLA_HEADS):
                sout_ref[b, h] = s_new[h * GLA_DK:(h + 1) * GLA_DK, h * GLA_DV:(h + 1) * GLA_DV]

    g_out = _gla_out(jnp.concatenate(o_parts, axis=0), gg, gnorm_ref[...])
    mix = jnp.concatenate([jnp.concatenate(a_parts, axis=0), g_out], axis=1).astype(BF16)
    y_ref[...] = x + _nn(mix, wout_ref[...])


def _mixer_sample(x, ct, st, wts, ck, cv, s0, *, dec, sb=16, par=8):
    n = x.shape[0]
    nseq = n // dec
    m = sb * dec
    row = lambda w: pl.BlockSpec((m, w), lambda i: (i, 0))
    seq3 = pl.BlockSpec((sb, WINDOW, LANES), lambda i: (i, 0, 0))
    seq4 = pl.BlockSpec((sb, GLA_HEADS, GLA_DK, GLA_DV), lambda i: (i, 0, 0, 0))
    kern = functools.partial(_mixer_sample_kernel, sb=sb, dec=dec, par=par)
    return pl.pallas_call(
        kern,
        grid=(nseq // sb,),
        in_specs=[row(D_MODEL), row(LANES), row(LANES)] + [_const_spec(w.shape) for w in wts] + [seq3, seq3, seq4],
        out_specs=[row(D_MODEL), seq3, seq3, seq4],
        out_shape=[jax.ShapeDtypeStruct((n, D_MODEL), F32), jax.ShapeDtypeStruct(ck.shape, F32),
                   jax.ShapeDtypeStruct(cv.shape, F32), jax.ShapeDtypeStruct(s0.shape, F32)],
        compiler_params=pltpu.CompilerParams(dimension_semantics=("arbitrary",),
                                             vmem_limit_bytes=48 * 1024 * 1024),
        name="mixer_sample",
    )(x, ct, st, *wts, ck, cv, s0)


def _sort_network(n):
    pairs = []

    def merge(lo, m, r):
        step = 2 * r
        if step < m:
            merge(lo, m, step)
            merge(lo + r, m, step)
            pairs.extend((i, i + r) for i in range(lo + r, lo + m - r, step))
        else:
            pairs.append((lo, lo + r))

    def sort(lo, m):
        if m > 1:
            sort(lo, m // 2)
            sort(lo + m // 2, m // 2)
            merge(lo, m, 1)

    sort(0, n)
    return pairs


def _top16(s):
    t = s.shape[1]
    nv = PEER_NKEYS // 8
    sub8 = lax.broadcasted_iota(jnp.int32, (8, t), 0).astype(F32)
    v = [s[8 * k:8 * k + 8] for k in range(nv)]
    ix = [sub8 + float(8 * k) for k in range(nv)]
    for p, q in _sort_network(nv):
        swap = (v[q] > v[p]) | ((v[q] == v[p]) & (ix[q] < ix[p]))
        v[p], v[q] = jnp.where(swap, v[q], v[p]), jnp.where(swap, v[p], v[q])
        ix[p], ix[q] = jnp.where(swap, ix[q], ix[p]), jnp.where(swap, ix[p], ix[q])
    vals, idxs = [], []
    for n in range(PEER_TOPK):
        m = jnp.max(v[0], axis=0, keepdims=True)
        imin = jnp.min(jnp.where(v[0] == m, ix[0], float(PEER_NKEYS)), axis=0, keepdims=True)
        hit = ix[0] == imin
        vals.append(m)
        idxs.append(imin)
        for k in range(PEER_TOPK - 1 - n):
            v[k] = jnp.where(hit, v[k + 1], v[k])
            ix[k] = jnp.where(hit, ix[k + 1], ix[k])
    return jnp.concatenate(vals, axis=0), jnp.concatenate(idxs, axis=0)


def _pair_top16(a, ia, b, ib):
    t = a.shape[1]
    sub8 = lax.broadcasted_iota(jnp.int32, (8, t), 0).astype(F32)
    ia_lo = ia[0:8] * float(PEER_NKEYS)
    lv, le = [], []
    for y in range(PEER_TOPK):
        nx = PEER_TOPK // (y + 1)
        val = a[0:8] + b[y:y + 1]
        lv.append(val if nx >= 8 else jnp.where(sub8 < float(nx), val, NEG_INF))
        le.append(ia_lo + ib[y:y + 1])
    hv = a[8:16] + b[0:1]
    he = ia[8:16] * float(PEER_NKEYS) + ib[0:1]
    code_lo = sub8 * float(PEER_TOPK)
    code_hi = (sub8 + 8.0) * float(PEER_TOPK)
    vals, sel = [], []
    for n in range(PEER_TOPK):
        m = jnp.max(jnp.maximum(lv[0], hv), axis=0, keepdims=True)
        cmin = jnp.min(jnp.minimum(jnp.where(lv[0] == m, code_lo, 1e9), jnp.where(hv == m, code_hi, 1e9)),
                       axis=0, keepdims=True)
        hit_lo = code_lo == cmin
        hit_hi = code_hi == cmin
        sel.append(jnp.max(jnp.maximum(jnp.where(hit_lo, le[0], -1.0), jnp.where(hit_hi, he, -1.0)),
                           axis=0, keepdims=True))
        vals.append(m)
        for y in range(PEER_TOPK - 1 - n):
            lv[y] = jnp.where(hit_lo, lv[y + 1], lv[y])
            le[y] = jnp.where(hit_lo, le[y + 1], le[y])
        code_lo = jnp.where(hit_lo, code_lo + 1.0, code_lo)
        hv = jnp.where(hit_hi, NEG_INF, hv)
    return jnp.concatenate(vals, axis=0), jnp.concatenate(sel, axis=0)


def _peer_route_gates_kernel(x_ref, fnorm_ref, wqt_ref, keys_ref, h_ref, w_ref,
                             qt_s, e_s, p_s, i_s, j_s, g_s, t_s, *, tm):
    t = pl.program_id(0)
    cur = lax.rem(t, 2)
    prev = 1 - cur
    group = tm // PEER_HEADS

    @pl.when(t == 0)
    def _():
        i_s[...] = jnp.zeros_like(i_s)
        j_s[...] = jnp.zeros_like(j_s)
        g_s[...] = jnp.zeros_like(g_s)

    hb = _rms(x_ref[...], fnorm_ref[...]).astype(BF16)
    h_ref[...] = hb
    qt_s[...] = _nt(wqt_ref[...], hb)
    sub = lax.broadcasted_iota(jnp.int32, (PEER_NKEYS, LANES), 0).astype(BF16)
    sub_odd = (lax.broadcasted_iota(jnp.int32, (W_ROWS_ODD, LANES), 0) - W_SHIFT).astype(BF16)
    one = jnp.ones((W_ROWS_ODD, LANES), BF16)
    zero = jnp.zeros((W_ROWS_ODD, LANES), BF16)

    def emit(n0, q):
        for i in range(PEER_NKEYS):
            a = t_s[pl.ds((16 * q) * W_PITCH + i, 8, stride=W_PITCH), :]
            b = t_s[pl.ds((16 * q + 8) * W_PITCH + i, 8, stride=W_PITCH), :]
            w_ref[pl.ds(n0 + 16 * q, 16), i * PEER_NKEYS:(i + 1) * PEER_NKEYS] = (
                jnp.concatenate([a, b], axis=0).astype(BF16))

    def head(hh, carry):
        r0 = pl.multiple_of(hh * 2 * PEER_NKEYS, 2 * PEER_NKEYS)
        s0 = _nn(keys_ref[2 * hh], qt_s[pl.ds(r0, PEER_NKEYS), :].astype(BF16))
        s1 = _nn(keys_ref[2 * hh + 1], qt_s[pl.ds(r0 + PEER_NKEYS, PEER_NKEYS), :].astype(BF16))
        n0 = pl.multiple_of(hh * group, group)
        for tok in range(group):
            ri = i_s[prev, pl.ds(n0 + tok, 1), :].astype(BF16)
            rj = j_s[prev, pl.ds(n0 + tok, 1), :].astype(BF16)
            rg = jnp.broadcast_to(g_s[prev, pl.ds(n0 + tok, 1), :].astype(BF16), sub.shape)
            bt = jnp.where(sub == rj, rg, zero[:PEER_NKEYS])
            if tok % 2 == 0:
                at = jnp.where(sub == ri, one[:PEER_NKEYS], zero[:PEER_NKEYS])
                t_s[tok * W_PITCH:tok * W_PITCH + PEER_NKEYS, :] = _nt(at, bt)
            else:
                at = jnp.where(sub_odd == ri, one, zero)
                t_s[tok * W_PITCH - W_SHIFT:tok * W_PITCH - W_SHIFT + W_ROWS_ODD, :] = _nt(at, bt)
        a, ia = _top16(s0)
        for q in range(group // 32):
            emit(n0, 2 * q)
        b, ib = _top16(s1)
        for q in range(group // 32):
            emit(n0, 2 * q + 1)
        c, e = _pair_top16(a, ia, b, ib)
        p = jnp.exp(c - c[0:1])
        rows = pl.ds(pl.multiple_of(hh * PEER_TOPK, PEER_TOPK), PEER_TOPK)
        e_s[rows, :] = e
        p_s[rows, :] = p / jnp.sum(p, axis=0, keepdims=True)
        return carry

    lax.fori_loop(0, PEER_HEADS, head, 0)
    et = jnp.transpose(e_s[...])
    it = jnp.floor(et * (1.0 / PEER_NKEYS))
    i_s[cur] = it
    j_s[cur] = et - it * float(PEER_NKEYS)
    g_s[cur] = jnp.transpose(p_s[...])


def _peer_route_gates(x, fnorm, wq, keys, *, tm=256):
    t = x.shape[0]
    tm = min(tm, t)
    nt = t // tm
    nsel = PEER_HEADS * PEER_TOPK
    n_exp = PEER_NKEYS * PEER_NKEYS
    group = tm // PEER_HEADS
    assert group % 32 == 0
    cur_tile = lambda i: (jnp.minimum(i, nt - 1), 0)
    return pl.pallas_call(
        functools.partial(_peer_route_gates_kernel, tm=tm),
        grid=(nt + 1,),
        in_specs=[pl.BlockSpec((tm, D_MODEL), cur_tile), _const_spec(fnorm.shape), _const_spec(wq.shape),
                  _const_spec(keys.shape)],
        out_specs=[pl.BlockSpec((tm, D_MODEL), cur_tile),
                   pl.BlockSpec((tm, n_exp), lambda i: (jnp.maximum(i - 1, 0), 0))],
        out_shape=[jax.ShapeDtypeStruct((t, D_MODEL), BF16), jax.ShapeDtypeStruct((t, n_exp), BF16)],
        scratch_shapes=[pltpu.VMEM((wq.shape[0], tm), F32), pltpu.VMEM((nsel, tm), F32),
                        pltpu.VMEM((nsel, tm), F32),
                        pltpu.VMEM((2, tm, nsel), F32), pltpu.VMEM((2, tm, nsel), F32),
                        pltpu.VMEM((2, tm, nsel), F32),
                        pltpu.VMEM(((group - 1) * W_PITCH - W_SHIFT + W_ROWS_ODD, LANES), F32)],
        compiler_params=pltpu.CompilerParams(dimension_semantics=("arbitrary",),
                                             vmem_limit_bytes=56 * 1024 * 1024),
        name="peer_route_gates",
    )(x, fnorm, wq, keys)


def _peer_dense_kernel(h_ref, x_ref, w_ref, u_ref, v_ref, y_ref):
    @pl.when(pl.program_id(1) == 0)
    def _():
        y_ref[...] = x_ref[...]

    act = _nt(h_ref[...], u_ref[...].astype(BF16))
    gelu = 0.5 * act * (1.0 + lax.erf(act * (2.0 ** -0.5)))
    z = w_ref[...] * gelu.astype(BF16)
    y_ref[...] += _nn(z, v_ref[...].astype(BF16))


def _peer_dense(h, x, w, u, v, *, tm=1024, te=1024):
    t = h.shape[0]
    tm = min(tm, t)
    n_exp = u.shape[0]
    tok = pl.BlockSpec((tm, D_MODEL), lambda i, e: (i, 0))
    exp = pl.BlockSpec((te, D_MODEL), lambda i, e: (e, 0))
    return pl.pallas_call(
        _peer_dense_kernel,
        grid=(t // tm, n_exp // te),
        in_specs=[tok, tok, pl.BlockSpec((tm, te), lambda i, e: (i, e)), exp, exp],
        out_specs=tok,
        out_shape=jax.ShapeDtypeStruct((t, D_MODEL), F32),
        compiler_params=pltpu.CompilerParams(dimension_semantics=("arbitrary", "arbitrary"),
                                             vmem_limit_bytes=56 * 1024 * 1024),
        name="peer_dense",
    )(h, x, w, u, v)


def _rope_tables(pos):
    half = ROPE_DIM // 2
    inv_freq = ROPE_THETA ** (-jnp.arange(half, dtype=F32) / half)
    ang = pos.astype(F32)[:, None] * inv_freq[None, :]
    cos, sin = jnp.cos(ang), jnp.sin(ang)
    n = pos.shape[0]
    c64 = jnp.concatenate([cos, cos, jnp.ones((n, HEAD_DIM - ROPE_DIM), F32)], axis=1)
    s64 = jnp.concatenate([-sin, sin, jnp.zeros((n, HEAD_DIM - ROPE_DIM), F32)], axis=1)
    return jnp.tile(c64, (1, 2)), jnp.tile(s64, (1, 2))


def kernel(x_prompt, x_sample, cache_swa_k, cache_swa_v, state_gla, attn_norm, w_in, q_norm, k_norm, attn_sinks,
           w_gate, b_gate, gla_norm, w_out, ffn_norm, peer_wq, peer_keys, peer_u, peer_v):
    depth = w_in.shape[0]
    assert depth == 1 and x_prompt.shape[0] == 1
    seq = x_prompt.shape[1]
    nseq, dec = x_sample.shape[0], x_sample.shape[1]
    xp = x_prompt[0]
    xs = x_sample.reshape(nseq * dec, D_MODEL)
    l = 0

    win = jnp.pad(w_in[l], ((0, 0), (0, C_END - w_in.shape[2]))).astype(BF16)
    qkg = jnp.concatenate([jnp.tile(q_norm[l], SWA_Q_HEADS), jnp.tile(k_norm[l], 2)])[None, :]
    sinks = jnp.broadcast_to(attn_sinks[l][:, None], (SWA_Q_HEADS, LANES))
    wgate = jnp.pad(w_gate[l], ((0, LANES - GLA_LOWRANK), (0, 0))).astype(BF16)
    wts = (attn_norm[l][None, :], win, qkg, sinks, wgate, b_gate[l][None, :],
           jnp.tile(gla_norm[l], GLA_HEADS)[None, :], w_out[l].astype(BF16))
    fnorm = ffn_norm[l][None, :]
    wq = jnp.transpose(peer_wq[l]).astype(BF16)
    keys = peer_keys[l].reshape(PEER_HEADS * 2, PEER_NKEYS, peer_keys.shape[-1]).astype(BF16)

    ct_p, st_p = _rope_tables(jnp.arange(seq, dtype=jnp.int32))
    ct_s, st_s = _rope_tables(PAST_LEN + jnp.arange(nseq * dec, dtype=jnp.int32) % dec)

    xp2, kp, vp, sp = _mixer_prompt(xp, ct_p, st_p, wts)
    hp, wp = _peer_route_gates(xp2, fnorm, wq, keys)
    yp = _peer_dense(hp, xp2, wp, peer_u[l], peer_v[l])

    ck = cache_swa_k[l].reshape(nseq, WINDOW, LANES)
    cv = cache_swa_v[l].reshape(nseq, WINDOW, LANES)
    xs2, ck2, cv2, ss = _mixer_sample(xs, ct_s, st_s, wts, ck, cv, state_gla[l], dec=dec)
    hs, ws = _peer_route_gates(xs2, fnorm, wq, keys)
    ys = _peer_dense(hs, xs2, ws, peer_u[l], peer_v[l])

    kv_shape = (1, 1, WINDOW, 2, HEAD_DIM)
    return (yp[None], ys.reshape(nseq, dec, D_MODEL),
            kp.reshape(kv_shape), vp.reshape(kv_shape), sp[None, None],
            ck2.reshape(1, nseq, WINDOW, 2, HEAD_DIM), cv2.reshape(1, nseq, WINDOW, 2, HEAD_DIM), ss[None])
```

```python
import functools

import jax
import jax.numpy as jnp
from jax import lax
from jax.experimental import pallas as pl
from jax.experimental.pallas import tpu as pltpu

F32 = jnp.float32
BF16 = jnp.bfloat16

D_MODEL = 1024
HEAD_DIM = 64
SWA_Q_HEADS = 8
WINDOW = 128
ROPE_THETA = 500000.0
ROPE_DIM = 16
PAST_LEN = 16384
GLA_HEADS = 4
GLA_DK = 64
GLA_DV = 128
GLA_LOWRANK = 16
GLA_GATE_NORM = 16.0
PEER_HEADS = 8
PEER_NKEYS = 128
PEER_TOPK = 16
NORM_EPS = 1e-6

LANES = 128
GK = GLA_HEADS * GLA_DK
GV = GLA_HEADS * GLA_DV
C_Q, C_K, C_V, C_GQ, C_GK, C_GV, C_GG, C_GA, C_END = 0, 512, 640, 768, 1024, 1280, 1792, 2304, 2432
GLA_PAD = 128
W_PITCH = 132
W_SHIFT = W_PITCH % 8
W_ROWS_ODD = 144
NEG_INF = float("-inf")

MIB = 1024 * 1024
MIXER_TM = 256
GLA_CHUNK = 128
GLA_SUB = 32
SAMPLE_SEQS = 16
SAMPLE_PAR = 8
ROUTE_TM = 256
DENSE_TM = 1024
DENSE_TE = 1024
MIXER_VMEM = 48 * MIB
PEER_VMEM = 56 * MIB


def _nn(a, b, precision=None):
    return jnp.dot(a, b, preferred_element_type=F32, precision=precision)


def _nt(a, b):
    return lax.dot_general(a, b, (((1,), (1,)), ((), ())), preferred_element_type=F32)


def _rms(x, gain):
    ms = jnp.mean(x * x, axis=-1, keepdims=True)
    return x * lax.rsqrt(ms + NORM_EPS) * gain


def _head_norm_rope(x, gain, ctab, stab):
    lane = lax.broadcasted_iota(jnp.int32, x.shape, 1)
    lo = lane < HEAD_DIM
    sq = x * x
    ms_lo = jnp.sum(jnp.where(lo, sq, 0.0), axis=-1, keepdims=True)
    ms_hi = jnp.sum(jnp.where(lo, 0.0, sq), axis=-1, keepdims=True)
    ms = jnp.where(lo, ms_lo, ms_hi) * (1.0 / HEAD_DIM)
    xn = x * lax.rsqrt(ms + NORM_EPS) * gain
    first = (lane & (HEAD_DIM - 1)) < (ROPE_DIM // 2)
    partner = jnp.where(first, pltpu.roll(xn, LANES - ROPE_DIM // 2, 1), pltpu.roll(xn, ROPE_DIM // 2, 1))
    return xn * ctab + partner * stab


def _project(x, ct, st, anorm, win, qkg, wgate, bgate):
    h = _rms(x, anorm).astype(BF16)
    p = _nn(h, win)
    qk = [
        _head_norm_rope(p[:, c * LANES:(c + 1) * LANES], qkg[:, c * LANES:(c + 1) * LANES], ct, st)
        for c in range(C_V // LANES)
    ]
    q = jnp.concatenate(qk[:4], axis=1) * (HEAD_DIM ** -0.5)
    k = qk[4]
    v = p[:, C_V:C_GQ]
    gq = p[:, C_GQ:C_GK] * (GLA_DK ** -0.5)
    gk = p[:, C_GK:C_GV]
    gv = p[:, C_GV:C_GG]
    gg = p[:, C_GG:C_GA]
    z = _nn(p[:, C_GA:C_END].astype(BF16), wgate) + bgate
    logf = (jnp.minimum(z, 0.0) - jnp.log(1.0 + jnp.exp(-jnp.abs(z)))) * (1.0 / GLA_GATE_NORM)
    return q, k, v, gq, gk, gv, gg, logf


def _swa_multi(qs, k2s, v2s, masks, sink_ref):
    n = len(qs)
    m = qs[0].shape[0]
    nk = k2s[0].shape[0]
    lo = lax.broadcasted_iota(jnp.int32, (nk, LANES), 1) < HEAD_DIM
    olane = lax.broadcasted_iota(jnp.int32, (2 * m, LANES), 1) < HEAD_DIM
    top = lax.broadcasted_iota(jnp.int32, (2 * m, 1), 0) < m
    zero = jnp.zeros((nk, LANES), F32)
    kexp, vexp = {}, {}
    for p in range(n):
        krot = pltpu.roll(k2s[p], HEAD_DIM, 1)
        vrot = pltpu.roll(v2s[p], HEAD_DIM, 1)
        for g in range(2):
            ka, kb = (k2s[p], krot) if g == 0 else (krot, k2s[p])
            va, vb = (v2s[p], vrot) if g == 0 else (vrot, v2s[p])
            kexp[p, g] = jnp.concatenate([jnp.where(lo, ka, zero), jnp.where(lo, zero, kb)], axis=0).astype(BF16)
            vexp[p, g] = jnp.concatenate([jnp.where(lo, va, zero), jnp.where(lo, zero, vb)], axis=0).astype(BF16)
    s = {}
    for p in range(n):
        for g in range(2):
            qq = jnp.concatenate([qs[p][:, (2 * g) * LANES:(2 * g + 1) * LANES],
                                  qs[p][:, (2 * g + 1) * LANES:(2 * g + 2) * LANES]], axis=0).astype(BF16)
            s[p, g] = _nt(qq, kexp[p, g])
    probs, scale = {}, {}
    for p in range(n):
        mask2 = jnp.concatenate([masks[p], masks[p]], axis=0)
        for g in range(2):
            ps, rs = [], []
            for hh in range(2):
                sh = jnp.where(mask2, s[p, g][:, hh * nk:(hh + 1) * nk], NEG_INF)
                ha, hb = 4 * g + hh, 4 * g + 2 + hh
                sink = jnp.where(top, sink_ref[ha:ha + 1, 0:1], sink_ref[hb:hb + 1, 0:1])
                mx = jnp.maximum(jnp.max(sh, axis=-1, keepdims=True), sink)
                pe = jnp.exp(sh - mx)
                den = jnp.sum(pe, axis=-1, keepdims=True) + jnp.exp(sink - mx)
                ps.append(pe)
                rs.append(1.0 / den)
            probs[p, g] = jnp.concatenate(ps, axis=1).astype(BF16)
            scale[p, g] = jnp.where(olane, rs[0], rs[1])
    outs = []
    for p in range(n):
        o = [_nn(probs[p, g], vexp[p, g]) * scale[p, g] for g in range(2)]
        outs.append(jnp.concatenate([o[0][:m], o[0][m:], o[1][:m], o[1][m:]], axis=1))
    return outs


def _zpad(a, rows):
    if a.shape[0] == rows:
        return a
    return jnp.concatenate([a, jnp.zeros((rows - a.shape[0], a.shape[1]), a.dtype)], axis=0)


def _gla_intra_multi(qs, ks, vs, fs, sub):
    n = len(qs)
    c = qs[0].shape[0]
    nsub = c // sub
    shift = sub.bit_length() - 1
    r = lax.broadcasted_iota(jnp.int32, (c, GLA_PAD), 0)
    cc = lax.broadcasted_iota(jnp.int32, (c, GLA_PAD), 1)
    causal = cc <= r
    tri2 = jnp.concatenate([causal, causal & (cc >= ((r >> shift) << shift))], axis=0).astype(BF16)
    lane_k = lax.broadcasted_iota(jnp.int32, (c, GK), 1) >> 6
    lane_v = lax.broadcasted_iota(jnp.int32, (c, GV), 1) >> 7
    rows = lax.broadcasted_iota(jnp.int32, (c, GK), 0)
    acol = lax.broadcasted_iota(jnp.int32, (c, GLA_HEADS * GLA_PAD), 1) & (GLA_PAD - 1)
    arow = lax.broadcasted_iota(jnp.int32, (c, GLA_HEADS * GLA_PAD), 0)
    amask = (acol <= arow) & (acol >= ((arow >> shift) << shift))

    def expand_k(kt):
        return jnp.concatenate(
            [_zpad(jnp.where(lane_k == h, kt, 0.0), GLA_PAD) for h in range(GLA_HEADS)], axis=0).astype(BF16)

    b, bl, bcol = [], [], []
    for p in range(n):
        fp = _zpad(fs[p], GLA_PAD)
        f_hi = fp.astype(BF16)
        r1 = fp - f_hi.astype(F32)
        f_mid = r1.astype(BF16)
        f_lo = (r1 - f_mid.astype(F32)).astype(BF16)
        cs = _nn(tri2, jnp.concatenate([f_hi, f_mid, f_lo], axis=1))
        tot = (cs[:, :GK] + cs[:, GK:2 * GK]) + cs[:, 2 * GK:]
        b.append(tot[:c])
        bl.append(tot[c:])
        bcol.append(jnp.sum(jnp.transpose(fp), axis=1, keepdims=True))
    qd, kd, qo, ko, qg, kbt, vexp, vpad = [], [], [], [], [], [], [], []
    for p in range(n):
        q, k, v = qs[p], ks[p], vs[p]
        qd.append((q * jnp.exp(bl[p])).astype(BF16))
        kd.append(expand_k(k * jnp.exp(-bl[p])))
        if nsub > 1:
            qj, kj = [], []
            for j in range(nsub - 1):
                e = (j + 1) * sub
                bj = b[p][e - 1:e]
                qj.append(jnp.where(rows >= e, q * jnp.exp(jnp.minimum(b[p] - bj, 0.0)), 0.0).astype(BF16))
                kj.append(expand_k(jnp.where((rows >= e - sub) & (rows < e),
                                             k * jnp.exp(jnp.minimum(bj - b[p], 0.0)), 0.0)))
            qo.append(jnp.concatenate(qj, axis=1))
            ko.append(jnp.concatenate(kj, axis=1))
        qg.append((q * jnp.exp(b[p])).astype(BF16))
        kbt.append(jnp.transpose(_zpad(k * jnp.exp(b[p][c - 1:c] - b[p]), GLA_PAD)).astype(BF16))
        vexp.append(jnp.concatenate(
            [_zpad(jnp.where(lane_v == h, v, 0.0), GLA_PAD) for h in range(GLA_HEADS)], axis=0).astype(BF16))
        vpad.append(_zpad(v, GLA_PAD).astype(BF16))
    a = []
    for p in range(n):
        ap = jnp.where(amask, _nt(qd[p], kd[p]), 0.0)
        if nsub > 1:
            ap = ap + _nt(qo[p], ko[p])
        a.append(ap.astype(BF16))
    return [(_nn(a[p], vexp[p]), qg[p], kbt[p], vpad[p], bcol[p]) for p in range(n)]


def _gla_inter_multi(parts, states):
    srow = lax.broadcasted_iota(jnp.int32, (GK, GV), 0) >> 6
    scol = lax.broadcasted_iota(jnp.int32, (GK, GV), 1) >> 7
    diag = srow == scol
    outs = [o + _nn(qg, s.astype(BF16)) for (o, qg, _, _, _), s in zip(parts, states)]
    new = [s * jnp.exp(bcol) + jnp.where(diag, _nn(kbt, vp), 0.0) for (_, _, kbt, vp, bcol), s in zip(parts, states)]
    return outs, new


def _gla_out(o, gg, gnorm):
    outs = []
    for h in range(GLA_HEADS):
        oh = o[:, h * GLA_DV:(h + 1) * GLA_DV]
        ms = jnp.mean(oh * oh, axis=-1, keepdims=True)
        outs.append(oh * lax.rsqrt(ms + NORM_EPS))
    on = jnp.concatenate(outs, axis=1) * gnorm
    return on * (gg / (1.0 + jnp.exp(-gg)))


def _mixer_prompt_kernel(x_ref, ct_ref, st_ref, anorm_ref, win_ref, qkg_ref, sink_ref, wgate_ref, bgate_ref,
                         gnorm_ref, wout_ref, y_ref, klast_ref, vlast_ref, sfin_ref,
                         kprev_ref, vprev_ref, s_ref, *, tm, chunk, sub):
    i = pl.program_id(0)

    @pl.when(i == 0)
    def _():
        kprev_ref[...] = jnp.zeros_like(kprev_ref)
        vprev_ref[...] = jnp.zeros_like(vprev_ref)
        s_ref[...] = jnp.zeros_like(s_ref)

    x = x_ref[...]
    q, k, v, gq, gk, gv, gg, logf = _project(
        x, ct_ref[...], st_ref[...], anorm_ref[...], win_ref[...], qkg_ref[...], wgate_ref[...], bgate_ref[...])

    kall = jnp.concatenate([kprev_ref[...], k], axis=0)
    vall = jnp.concatenate([vprev_ref[...], v], axis=0)
    qi = lax.broadcasted_iota(jnp.int32, (WINDOW, 2 * WINDOW), 0)
    kj = lax.broadcasted_iota(jnp.int32, (WINDOW, 2 * WINDOW), 1)
    band = (kj >= qi) & (kj <= qi + WINDOW)
    nb = tm // WINDOW
    masks = [band & (kj >= jnp.where(i == 0, WINDOW, 0)) if j == 0 else band for j in range(nb)]
    blocks = _swa_multi([q[j * WINDOW:(j + 1) * WINDOW] for j in range(nb)],
                        [kall[j * WINDOW:(j + 2) * WINDOW] for j in range(nb)],
                        [vall[j * WINDOW:(j + 2) * WINDOW] for j in range(nb)], masks, sink_ref)
    a_out = jnp.concatenate(blocks, axis=0)
    kprev_ref[...] = k[tm - WINDOW:]
    vprev_ref[...] = v[tm - WINDOW:]
    klast_ref[...] = k[tm - WINDOW:]
    vlast_ref[...] = v[tm - WINDOW:]

    sls = [slice(c * chunk, (c + 1) * chunk) for c in range(tm // chunk)]
    parts = _gla_intra_multi([gq[sl] for sl in sls], [gk[sl] for sl in sls], [gv[sl] for sl in sls],
                             [logf[sl] for sl in sls], sub)
    s = s_ref[...]
    os_ = []
    for part in parts:
        (o,), (s,) = _gla_inter_multi([part], [s])
        os_.append(o)
    s_ref[...] = s
    for h in range(GLA_HEADS):
        sfin_ref[h] = s[h * GLA_DK:(h + 1) * GLA_DK, h * GLA_DV:(h + 1) * GLA_DV]
    g_out = _gla_out(jnp.concatenate(os_, axis=0), gg, gnorm_ref[...])

    mix = jnp.concatenate([a_out, g_out], axis=1).astype(BF16)
    y_ref[...] = x + _nn(mix, wout_ref[...])


def _const_spec(shape):
    return pl.BlockSpec(shape, lambda *_: (0,) * len(shape))


def _mixer_prompt(x, ct, st, wts, *, tm=MIXER_TM, chunk=GLA_CHUNK, sub=GLA_SUB):
    t = x.shape[0]
    tm = min(tm, t)
    row = lambda w: pl.BlockSpec((tm, w), lambda i: (i, 0))
    kern = functools.partial(_mixer_prompt_kernel, tm=tm, chunk=chunk, sub=sub)
    return pl.pallas_call(
        kern,
        grid=(t // tm,),
        in_specs=[row(D_MODEL), row(LANES), row(LANES)] + [_const_spec(w.shape) for w in wts],
        out_specs=[row(D_MODEL), _const_spec((WINDOW, LANES)), _const_spec((WINDOW, LANES)),
                   _const_spec((GLA_HEADS, GLA_DK, GLA_DV))],
        out_shape=[jax.ShapeDtypeStruct((t, D_MODEL), F32), jax.ShapeDtypeStruct((WINDOW, LANES), F32),
                   jax.ShapeDtypeStruct((WINDOW, LANES), F32),
                   jax.ShapeDtypeStruct((GLA_HEADS, GLA_DK, GLA_DV), F32)],
        scratch_shapes=[pltpu.VMEM((WINDOW, LANES), F32), pltpu.VMEM((WINDOW, LANES), F32),
                        pltpu.VMEM((GK, GV), F32)],
        compiler_params=pltpu.CompilerParams(dimension_semantics=("arbitrary",),
                                             vmem_limit_bytes=MIXER_VMEM),
        name="mixer_prompt",
    )(x, ct, st, *wts)


def _mixer_sample_kernel(x_ref, ct_ref, st_ref, anorm_ref, win_ref, qkg_ref, sink_ref, wgate_ref, bgate_ref,
                         gnorm_ref, wout_ref, ck_ref, cv_ref, sin_ref,
                         y_ref, kout_ref, vout_ref, sout_ref, *, sb, dec, par):
    x = x_ref[...]
    q, k, v, gq, gk, gv, gg, logf = _project(
        x, ct_ref[...], st_ref[...], anorm_ref[...], win_ref[...], qkg_ref[...], wgate_ref[...], bgate_ref[...])
    for b in range(sb):
        kout_ref[b] = jnp.concatenate([ck_ref[b, dec:, :], k[b * dec:(b + 1) * dec]], axis=0)
        vout_ref[b] = jnp.concatenate([cv_ref[b, dec:, :], v[b * dec:(b + 1) * dec]], axis=0)

    per = 8 // dec
    qi = lax.broadcasted_iota(jnp.int32, (8, 2 * WINDOW), 0) & (dec - 1)
    kj = lax.broadcasted_iota(jnp.int32, (8, 2 * WINDOW), 1)
    mask = (kj >= qi) & (kj <= qi + WINDOW)
    ztail = jnp.zeros((WINDOW - dec, LANES), F32)
    pad8 = lambda a: jnp.concatenate([a, jnp.zeros((8 - dec, a.shape[1]), F32)], axis=0)

    a_parts, o_parts = [], []
    for b0 in range(0, sb, par):
        seqs = range(b0, b0 + par)
        rows8 = [slice((b // per) * 8, (b // per) * 8 + 8) for b in seqs]
        own = [slice((b % per) * dec, (b % per) * dec + dec) for b in seqs]
        new = [slice(b * dec, (b + 1) * dec) for b in seqs]
        att = _swa_multi([q[r] for r in rows8],
                         [jnp.concatenate([ck_ref[b], k[nw], ztail], axis=0) for b, nw in zip(seqs, new)],
                         [jnp.concatenate([cv_ref[b], v[nw], ztail], axis=0) for b, nw in zip(seqs, new)],
                         [mask] * par, sink_ref)
        a_parts += [a[o] for a, o in zip(att, own)]
        states = []
        for b in seqs:
            st = sin_ref[b]
            srows = []
            for h in range(GLA_HEADS):
                parts = []
                if h > 0:
                    parts.append(jnp.zeros((GLA_DK, h * GLA_DV), F32))
                parts.append(st[h])
                if h < GLA_HEADS - 1:
                    parts.append(jnp.zeros((GLA_DK, (GLA_HEADS - 1 - h) * GLA_DV), F32))
                srows.append(jnp.concatenate(parts, axis=1))
            states.append(jnp.concatenate(srows, axis=0))
        intra = _gla_intra_multi([pad8(gq[nw]) for nw in new], [pad8(gk[nw]) for nw in new],
                                 [pad8(gv[nw]) for nw in new], [pad8(logf[nw]) for nw in new], 8)
        outs, states = _gla_inter_multi(intra, states)
        o_parts += [o[:dec] for o in outs]
        for b, s_new in zip(seqs, states):
            for h in range(GLA_HEADS):
                sout_ref[b, h] = s_new[h * GLA_DK:(h + 1) * GLA_DK, h * GLA_DV:(h + 1) * GLA_DV]

    g_out = _gla_out(jnp.concatenate(o_parts, axis=0), gg, gnorm_ref[...])
    mix = jnp.concatenate([jnp.concatenate(a_parts, axis=0), g_out], axis=1).astype(BF16)
    y_ref[...] = x + _nn(mix, wout_ref[...])


def _mixer_sample(x, ct, st, wts, ck, cv, s0, *, dec, sb=SAMPLE_SEQS, par=SAMPLE_PAR):
    n = x.shape[0]
    nseq = n // dec
    m = sb * dec
    row = lambda w: pl.BlockSpec((m, w), lambda i: (i, 0))
    seq3 = pl.BlockSpec((sb, WINDOW, LANES), lambda i: (i, 0, 0))
    seq4 = pl.BlockSpec((sb, GLA_HEADS, GLA_DK, GLA_DV), lambda i: (i, 0, 0, 0))
    kern = functools.partial(_mixer_sample_kernel, sb=sb, dec=dec, par=par)
    return pl.pallas_call(
        kern,
        grid=(nseq // sb,),
        in_specs=[row(D_MODEL), row(LANES), row(LANES)] + [_const_spec(w.shape) for w in wts] + [seq3, seq3, seq4],
        out_specs=[row(D_MODEL), seq3, seq3, seq4],
        out_shape=[jax.ShapeDtypeStruct((n, D_MODEL), F32), jax.ShapeDtypeStruct(ck.shape, F32),
                   jax.ShapeDtypeStruct(cv.shape, F32), jax.ShapeDtypeStruct(s0.shape, F32)],
        compiler_params=pltpu.CompilerParams(dimension_semantics=("arbitrary",),
                                             vmem_limit_bytes=MIXER_VMEM),
        name="mixer_sample",
    )(x, ct, st, *wts, ck, cv, s0)


def _sort_network(n):
    pairs = []

    def merge(lo, m, r):
        step = 2 * r
        if step < m:
            merge(lo, m, step)
            merge(lo + r, m, step)
            pairs.extend((i, i + r) for i in range(lo + r, lo + m - r, step))
        else:
            pairs.append((lo, lo + r))

    def sort(lo, m):
        if m > 1:
            sort(lo, m // 2)
            sort(lo + m // 2, m // 2)
            merge(lo, m, 1)

    sort(0, n)
    return pairs


def _top16(s):
    t = s.shape[1]
    nv = PEER_NKEYS // 8
    sub8 = lax.broadcasted_iota(jnp.int32, (8, t), 0).astype(F32)
    v = [s[8 * k:8 * k + 8] for k in range(nv)]
    ix = [sub8 + float(8 * k) for k in range(nv)]
    for p, q in _sort_network(nv):
        swap = (v[q] > v[p]) | ((v[q] == v[p]) & (ix[q] < ix[p]))
        v[p], v[q] = jnp.where(swap, v[q], v[p]), jnp.where(swap, v[p], v[q])
        ix[p], ix[q] = jnp.where(swap, ix[q], ix[p]), jnp.where(swap, ix[p], ix[q])
    vals, idxs = [], []
    for n in range(PEER_TOPK):
        m = jnp.max(v[0], axis=0, keepdims=True)
        imin = jnp.min(jnp.where(v[0] == m, ix[0], float(PEER_NKEYS)), axis=0, keepdims=True)
        hit = ix[0] == imin
        vals.append(m)
        idxs.append(imin)
        for k in range(PEER_TOPK - 1 - n):
            v[k] = jnp.where(hit, v[k + 1], v[k])
            ix[k] = jnp.where(hit, ix[k + 1], ix[k])
    return jnp.concatenate(vals, axis=0), jnp.concatenate(idxs, axis=0)


def _pair_top16(a, ia, b, ib):
    t = a.shape[1]
    sub8 = lax.broadcasted_iota(jnp.int32, (8, t), 0).astype(F32)
    ia_lo = ia[0:8] * float(PEER_NKEYS)
    lv, le = [], []
    for y in range(PEER_TOPK):
        nx = PEER_TOPK // (y + 1)
        val = a[0:8] + b[y:y + 1]
        lv.append(val if nx >= 8 else jnp.where(sub8 < float(nx), val, NEG_INF))
        le.append(ia_lo + ib[y:y + 1])
    hv = a[8:16] + b[0:1]
    he = ia[8:16] * float(PEER_NKEYS) + ib[0:1]
    code_lo = sub8 * float(PEER_TOPK)
    code_hi = (sub8 + 8.0) * float(PEER_TOPK)
    vals, sel = [], []
    for n in range(PEER_TOPK):
        m = jnp.max(jnp.maximum(lv[0], hv), axis=0, keepdims=True)
        cmin = jnp.min(jnp.minimum(jnp.where(lv[0] == m, code_lo, 1e9), jnp.where(hv == m, code_hi, 1e9)),
                       axis=0, keepdims=True)
        hit_lo = code_lo == cmin
        hit_hi = code_hi == cmin
        sel.append(jnp.max(jnp.maximum(jnp.where(hit_lo, le[0], -1.0), jnp.where(hit_hi, he, -1.0)),
                           axis=0, keepdims=True))
        vals.append(m)
        for y in range(PEER_TOPK - 1 - n):
            lv[y] = jnp.where(hit_lo, lv[y + 1], lv[y])
            le[y] = jnp.where(hit_lo, le[y + 1], le[y])
        code_lo = jnp.where(hit_lo, code_lo + 1.0, code_lo)
        hv = jnp.where(hit_hi, NEG_INF, hv)
    return jnp.concatenate(vals, axis=0), jnp.concatenate(sel, axis=0)


def _peer_route_gates_kernel(x_ref, fnorm_ref, wqt_ref, keys_ref, h_ref, w_ref,
                             qt_s, e_s, p_s, i_s, j_s, g_s, t_s, *, tm):
    t = pl.program_id(0)
    cur = lax.rem(t, 2)
    prev = 1 - cur
    group = tm // PEER_HEADS

    @pl.when(t == 0)
    def _():
        i_s[...] = jnp.zeros_like(i_s)
        j_s[...] = jnp.zeros_like(j_s)
        g_s[...] = jnp.zeros_like(g_s)

    hb = _rms(x_ref[...], fnorm_ref[...]).astype(BF16)
    h_ref[...] = hb
    qt_s[...] = _nt(wqt_ref[...], hb)
    sub = lax.broadcasted_iota(jnp.int32, (PEER_NKEYS, LANES), 0).astype(BF16)
    sub_odd = (lax.broadcasted_iota(jnp.int32, (W_ROWS_ODD, LANES), 0) - W_SHIFT).astype(BF16)
    one = jnp.ones((W_ROWS_ODD, LANES), BF16)
    zero = jnp.zeros((W_ROWS_ODD, LANES), BF16)

    def emit(n0, q):
        for i in range(PEER_NKEYS):
            a = t_s[pl.ds((16 * q) * W_PITCH + i, 8, stride=W_PITCH), :]
            b = t_s[pl.ds((16 * q + 8) * W_PITCH + i, 8, stride=W_PITCH), :]
            w_ref[pl.ds(n0 + 16 * q, 16), i * PEER_NKEYS:(i + 1) * PEER_NKEYS] = (
                jnp.concatenate([a, b], axis=0).astype(BF16))

    def head(hh, carry):
        r0 = pl.multiple_of(hh * 2 * PEER_NKEYS, 2 * PEER_NKEYS)
        s0 = _nn(keys_ref[2 * hh], qt_s[pl.ds(r0, PEER_NKEYS), :].astype(BF16))
        s1 = _nn(keys_ref[2 * hh + 1], qt_s[pl.ds(r0 + PEER_NKEYS, PEER_NKEYS), :].astype(BF16))
        n0 = pl.multiple_of(hh * group, group)
        for tok in range(group):
            ri = i_s[prev, pl.ds(n0 + tok, 1), :].astype(BF16)
            rj = j_s[prev, pl.ds(n0 + tok, 1), :].astype(BF16)
            rg = jnp.broadcast_to(g_s[prev, pl.ds(n0 + tok, 1), :].astype(BF16), sub.shape)
            bt = jnp.where(sub == rj, rg, zero[:PEER_NKEYS])
            if tok % 2 == 0:
                at = jnp.where(sub == ri, one[:PEER_NKEYS], zero[:PEER_NKEYS])
                t_s[tok * W_PITCH:tok * W_PITCH + PEER_NKEYS, :] = _nt(at, bt)
            else:
                at = jnp.where(sub_odd == ri, one, zero)
                t_s[tok * W_PITCH - W_SHIFT:tok * W_PITCH - W_SHIFT + W_ROWS_ODD, :] = _nt(at, bt)
        a, ia = _top16(s0)
        for q in range(group // 32):
            emit(n0, 2 * q)
        b, ib = _top16(s1)
        for q in range(group // 32):
            emit(n0, 2 * q + 1)
        c, e = _pair_top16(a, ia, b, ib)
        p = jnp.exp(c - c[0:1])
        rows = pl.ds(pl.multiple_of(hh * PEER_TOPK, PEER_TOPK), PEER_TOPK)
        e_s[rows, :] = e
        p_s[rows, :] = p / jnp.sum(p, axis=0, keepdims=True)
        return carry

    lax.fori_loop(0, PEER_HEADS, head, 0)
    et = jnp.transpose(e_s[...])
    it = jnp.floor(et * (1.0 / PEER_NKEYS))
    i_s[cur] = it
    j_s[cur] = et - it * float(PEER_NKEYS)
    g_s[cur] = jnp.transpose(p_s[...])


def _peer_route_gates(x, fnorm, wq, keys, *, tm=ROUTE_TM):
    t = x.shape[0]
    tm = min(tm, t)
    nt = t // tm
    nsel = PEER_HEADS * PEER_TOPK
    n_exp = PEER_NKEYS * PEER_NKEYS
    group = tm // PEER_HEADS
    assert group % 32 == 0
    cur_tile = lambda i: (jnp.minimum(i, nt - 1), 0)
    return pl.pallas_call(
        functools.partial(_peer_route_gates_kernel, tm=tm),
        grid=(nt + 1,),
        in_specs=[pl.BlockSpec((tm, D_MODEL), cur_tile), _const_spec(fnorm.shape), _const_spec(wq.shape),
                  _const_spec(keys.shape)],
        out_specs=[pl.BlockSpec((tm, D_MODEL), cur_tile),
                   pl.BlockSpec((tm, n_exp), lambda i: (jnp.maximum(i - 1, 0), 0))],
        out_shape=[jax.ShapeDtypeStruct((t, D_MODEL), BF16), jax.ShapeDtypeStruct((t, n_exp), BF16)],
        scratch_shapes=[pltpu.VMEM((wq.shape[0], tm), F32), pltpu.VMEM((nsel, tm), F32),
                        pltpu.VMEM((nsel, tm), F32),
                        pltpu.VMEM((2, tm, nsel), F32), pltpu.VMEM((2, tm, nsel), F32),
                        pltpu.VMEM((2, tm, nsel), F32),
                        pltpu.VMEM(((group - 1) * W_PITCH - W_SHIFT + W_ROWS_ODD, LANES), F32)],
        compiler_params=pltpu.CompilerParams(dimension_semantics=("arbitrary",),
                                             vmem_limit_bytes=PEER_VMEM),
        name="peer_route_gates",
    )(x, fnorm, wq, keys)


def _peer_dense_kernel(h_ref, x_ref, w_ref, u_ref, v_ref, y_ref):
    @pl.when(pl.program_id(1) == 0)
    def _():
        y_ref[...] = x_ref[...]

    act = _nt(h_ref[...], u_ref[...].astype(BF16))
    gelu = 0.5 * act * (1.0 + lax.erf(act * (2.0 ** -0.5)))
    z = w_ref[...] * gelu.astype(BF16)
    y_ref[...] += _nn(z, v_ref[...].astype(BF16))


def _peer_dense(h, x, w, u, v, *, tm=DENSE_TM, te=DENSE_TE):
    t = h.shape[0]
    tm = min(tm, t)
    n_exp = u.shape[0]
    tok = pl.BlockSpec((tm, D_MODEL), lambda i, e: (i, 0))
    exp = pl.BlockSpec((te, D_MODEL), lambda i, e: (e, 0))
    return pl.pallas_call(
        _peer_dense_kernel,
        grid=(t // tm, n_exp // te),
        in_specs=[tok, tok, pl.BlockSpec((tm, te), lambda i, e: (i, e)), exp, exp],
        out_specs=tok,
        out_shape=jax.ShapeDtypeStruct((t, D_MODEL), F32),
        compiler_params=pltpu.CompilerParams(dimension_semantics=("arbitrary", "arbitrary"),
                                             vmem_limit_bytes=PEER_VMEM),
        name="peer_dense",
    )(h, x, w, u, v)


def _rope_tables(pos):
    half = ROPE_DIM // 2
    inv_freq = ROPE_THETA ** (-jnp.arange(half, dtype=F32) / half)
    ang = pos.astype(F32)[:, None] * inv_freq[None, :]
    cos, sin = jnp.cos(ang), jnp.sin(ang)
    n = pos.shape[0]
    c64 = jnp.concatenate([cos, cos, jnp.ones((n, HEAD_DIM - ROPE_DIM), F32)], axis=1)
    s64 = jnp.concatenate([-sin, sin, jnp.zeros((n, HEAD_DIM - ROPE_DIM), F32)], axis=1)
    return jnp.tile(c64, (1, 2)), jnp.tile(s64, (1, 2))


def kernel(x_prompt, x_sample, cache_swa_k, cache_swa_v, state_gla, attn_norm, w_in, q_norm, k_norm, attn_sinks,
           w_gate, b_gate, gla_norm, w_out, ffn_norm, peer_wq, peer_keys, peer_u, peer_v):
    depth = w_in.shape[0]
    assert depth == 1 and x_prompt.shape[0] == 1
    seq = x_prompt.shape[1]
    nseq, dec = x_sample.shape[0], x_sample.shape[1]
    xp = x_prompt[0]
    xs = x_sample.reshape(nseq * dec, D_MODEL)
    l = 0

    win = jnp.pad(w_in[l], ((0, 0), (0, C_END - w_in.shape[2]))).astype(BF16)
    qkg = jnp.concatenate([jnp.tile(q_norm[l], SWA_Q_HEADS), jnp.tile(k_norm[l], 2)])[None, :]
    sinks = jnp.broadcast_to(attn_sinks[l][:, None], (SWA_Q_HEADS, LANES))
    wgate = jnp.pad(w_gate[l], ((0, LANES - GLA_LOWRANK), (0, 0))).astype(BF16)
    wts = (attn_norm[l][None, :], win, qkg, sinks, wgate, b_gate[l][None, :],
           jnp.tile(gla_norm[l], GLA_HEADS)[None, :], w_out[l].astype(BF16))
    fnorm = ffn_norm[l][None, :]
    wq = jnp.transpose(peer_wq[l]).astype(BF16)
    keys = peer_keys[l].reshape(PEER_HEADS * 2, PEER_NKEYS, peer_keys.shape[-1]).astype(BF16)

    ct_p, st_p = _rope_tables(jnp.arange(seq, dtype=jnp.int32))
    ct_s, st_s = _rope_tables(PAST_LEN + jnp.arange(nseq * dec, dtype=jnp.int32) % dec)

    xp2, kp, vp, sp = _mixer_prompt(xp, ct_p, st_p, wts)
    hp, wp = _peer_route_gates(xp2, fnorm, wq, keys)
    yp = _peer_dense(hp, xp2, wp, peer_u[l], peer_v[l])

    ck = cache_swa_k[l].reshape(nseq, WINDOW, LANES)
    cv = cache_swa_v[l].reshape(nseq, WINDOW, LANES)
    xs2, ck2, cv2, ss = _mixer_sample(xs, ct_s, st_s, wts, ck, cv, state_gla[l], dec=dec)
    hs, ws = _peer_route_gates(xs2, fnorm, wq, keys)
    ys = _peer_dense(hs, xs2, ws, peer_u[l], peer_v[l])

    kv_shape = (1, 1, WINDOW, 2, HEAD_DIM)
    return (yp[None], ys.reshape(nseq, dec, D_MODEL),
            kp.reshape(kv_shape), vp.reshape(kv_shape), sp[None, None],
            ck2.reshape(1, nseq, WINDOW, 2, HEAD_DIM), cv2.reshape(1, nseq, WINDOW, 2, HEAD_DIM), ss[None])
```

```python
import functools

import jax
import jax.numpy as jnp
from jax import lax
from jax.experimental import pallas as pl
from jax.experimental.pallas import tpu as pltpu

F32 = jnp.float32
BF16 = jnp.bfloat16

D_MODEL = 1024
HEAD_DIM = 64
SWA_Q_HEADS = 8
WINDOW = 128
ROPE_THETA = 500000.0
ROPE_DIM = 16
PAST_LEN = 16384
GLA_HEADS = 4
GLA_DK = 64
GLA_DV = 128
GLA_LOWRANK = 16
GLA_GATE_NORM = 16.0
PEER_HEADS = 8
PEER_NKEYS = 128
PEER_TOPK = 16
NORM_EPS = 1e-6

LANES = 128
GK = GLA_HEADS * GLA_DK
GV = GLA_HEADS * GLA_DV
C_Q, C_K, C_V, C_GQ, C_GK, C_GV, C_GG, C_GA, C_END = 0, 512, 640, 768, 1024, 1280, 1792, 2304, 2432
GLA_PAD = 128
W_PITCH = 132
W_SHIFT = W_PITCH % 8
W_ROWS_ODD = 144
NEG_INF = float("-inf")

MIB = 1024 * 1024
MIXER_TM = 256
GLA_CHUNK = 128
GLA_SUB = 32
SAMPLE_SEQS = 16
SAMPLE_PAR = 8
ROUTE_TM = 256
DENSE_TM = 1024
DENSE_TE = 1024
MIXER_VMEM = 48 * MIB
PEER_VMEM = 56 * MIB


def _nn(a, b, precision=None):
    return jnp.dot(a, b, preferred_element_type=F32, precision=precision)


def _nt(a, b):
    return lax.dot_general(a, b, (((1,), (1,)), ((), ())), preferred_element_type=F32)


def _rms(x, gain):
    ms = jnp.mean(x * x, axis=-1, keepdims=True)
    return x * lax.rsqrt(ms + NORM_EPS) * gain


def _head_norm_rope(x, gain, ctab, stab):
    lane = lax.broadcasted_iota(jnp.int32, x.shape, 1)
    lo = lane < HEAD_DIM
    sq = x * x
    ms_lo = jnp.sum(jnp.where(lo, sq, 0.0), axis=-1, keepdims=True)
    ms_hi = jnp.sum(jnp.where(lo, 0.0, sq), axis=-1, keepdims=True)
    ms = jnp.where(lo, ms_lo, ms_hi) * (1.0 / HEAD_DIM)
    xn = x * lax.rsqrt(ms + NORM_EPS) * gain
    first = (lane & (HEAD_DIM - 1)) < (ROPE_DIM // 2)
    partner = jnp.where(first, pltpu.roll(xn, LANES - ROPE_DIM // 2, 1), pltpu.roll(xn, ROPE_DIM // 2, 1))
    return xn * ctab + partner * stab


def _project(x, ct, st, anorm, win, qkg, wgate, bgate):
    h = _rms(x, anorm).astype(BF16)
    p = _nn(h, win)
    qk = [
        _head_norm_rope(p[:, c * LANES:(c + 1) * LANES], qkg[:, c * LANES:(c + 1) * LANES], ct, st)
        for c in range(C_V // LANES)
    ]
    q = jnp.concatenate(qk[:4], axis=1) * (HEAD_DIM ** -0.5)
    k = qk[4]
    v = p[:, C_V:C_GQ]
    gq = p[:, C_GQ:C_GK] * (GLA_DK ** -0.5)
    gk = p[:, C_GK:C_GV]
    gv = p[:, C_GV:C_GG]
    gg = p[:, C_GG:C_GA]
    z = _nn(p[:, C_GA:C_END].astype(BF16), wgate) + bgate
    logf = (jnp.minimum(z, 0.0) - jnp.log(1.0 + jnp.exp(-jnp.abs(z)))) * (1.0 / GLA_GATE_NORM)
    return q, k, v, gq, gk, gv, gg, logf


def _swa_multi(qs, k2s, v2s, masks, sink_ref):
    n = len(qs)
    m = qs[0].shape[0]
    nk = k2s[0].shape[0]
    lo = lax.broadcasted_iota(jnp.int32, (nk, LANES), 1) < HEAD_DIM
    olane = lax.broadcasted_iota(jnp.int32, (2 * m, LANES), 1) < HEAD_DIM
    top = lax.broadcasted_iota(jnp.int32, (2 * m, 1), 0) < m
    zero = jnp.zeros((nk, LANES), F32)
    kexp, vexp = {}, {}
    for p in range(n):
        krot = pltpu.roll(k2s[p], HEAD_DIM, 1)
        vrot = pltpu.roll(v2s[p], HEAD_DIM, 1)
        for g in range(2):
            ka, kb = (k2s[p], krot) if g == 0 else (krot, k2s[p])
            va, vb = (v2s[p], vrot) if g == 0 else (vrot, v2s[p])
            kexp[p, g] = jnp.concatenate([jnp.where(lo, ka, zero), jnp.where(lo, zero, kb)], axis=0).astype(BF16)
            vexp[p, g] = jnp.concatenate([jnp.where(lo, va, zero), jnp.where(lo, zero, vb)], axis=0).astype(BF16)
    s = {}
    for p in range(n):
        for g in range(2):
            qq = jnp.concatenate([qs[p][:, (2 * g) * LANES:(2 * g + 1) * LANES],
                                  qs[p][:, (2 * g + 1) * LANES:(2 * g + 2) * LANES]], axis=0).astype(BF16)
            s[p, g] = _nt(qq, kexp[p, g])
    probs, scale = {}, {}
    for p in range(n):
        mask2 = jnp.concatenate([masks[p], masks[p]], axis=0)
        for g in range(2):
            ps, rs = [], []
            for hh in range(2):
                sh = jnp.where(mask2, s[p, g][:, hh * nk:(hh + 1) * nk], NEG_INF)
                ha, hb = 4 * g + hh, 4 * g + 2 + hh
                sink = jnp.where(top, sink_ref[ha:ha + 1, 0:1], sink_ref[hb:hb + 1, 0:1])
                mx = jnp.maximum(jnp.max(sh, axis=-1, keepdims=True), sink)
                pe = jnp.exp(sh - mx)
                den = jnp.sum(pe, axis=-1, keepdims=True) + jnp.exp(sink - mx)
                ps.append(pe)
                rs.append(1.0 / den)
            probs[p, g] = jnp.concatenate(ps, axis=1).astype(BF16)
            scale[p, g] = jnp.where(olane, rs[0], rs[1])
    outs = []
    for p in range(n):
        o = [_nn(probs[p, g], vexp[p, g]) * scale[p, g] for g in range(2)]
        outs.append(jnp.concatenate([o[0][:m], o[0][m:], o[1][:m], o[1][m:]], axis=1))
    return outs


def _zpad(a, rows):
    if a.shape[0] == rows:
        return a
    return jnp.concatenate([a, jnp.zeros((rows - a.shape[0], a.shape[1]), a.dtype)], axis=0)


def _gla_intra_multi(qs, ks, vs, fs, sub):
    n = len(qs)
    c = qs[0].shape[0]
    nsub = c // sub
    shift = sub.bit_length() - 1
    r = lax.broadcasted_iota(jnp.int32, (c, GLA_PAD), 0)
    cc = lax.broadcasted_iota(jnp.int32, (c, GLA_PAD), 1)
    causal = cc <= r
    tri2 = jnp.concatenate([causal, causal & (cc >= ((r >> shift) << shift))], axis=0).astype(BF16)
    lane_k = lax.broadcasted_iota(jnp.int32, (c, GK), 1) >> 6
    lane_v = lax.broadcasted_iota(jnp.int32, (c, GV), 1) >> 7
    rows = lax.broadcasted_iota(jnp.int32, (c, GK), 0)
    acol = lax.broadcasted_iota(jnp.int32, (c, GLA_HEADS * GLA_PAD), 1) & (GLA_PAD - 1)
    arow = lax.broadcasted_iota(jnp.int32, (c, GLA_HEADS * GLA_PAD), 0)
    amask = (acol <= arow) & (acol >= ((arow >> shift) << shift))

    def expand_k(kt):
        return jnp.concatenate(
            [_zpad(jnp.where(lane_k == h, kt, 0.0), GLA_PAD) for h in range(GLA_HEADS)], axis=0).astype(BF16)

    b, bl, bcol = [], [], []
    for p in range(n):
        fp = _zpad(fs[p], GLA_PAD)
        f_hi = fp.astype(BF16)
        r1 = fp - f_hi.astype(F32)
        f_mid = r1.astype(BF16)
        f_lo = (r1 - f_mid.astype(F32)).astype(BF16)
        cs = _nn(tri2, jnp.concatenate([f_hi, f_mid, f_lo], axis=1))
        tot = (cs[:, :GK] + cs[:, GK:2 * GK]) + cs[:, 2 * GK:]
        b.append(tot[:c])
        bl.append(tot[c:])
        bcol.append(jnp.sum(jnp.transpose(fp), axis=1, keepdims=True))
    qd, kd, qo, ko, qg, kbt, vexp, vpad = [], [], [], [], [], [], [], []
    for p in range(n):
        q, k, v = qs[p], ks[p], vs[p]
        qd.append((q * jnp.exp(bl[p])).astype(BF16))
        kd.append(expand_k(k * jnp.exp(-bl[p])))
        if nsub > 1:
            qj, kj = [], []
            for j in range(nsub - 1):
                e = (j + 1) * sub
                bj = b[p][e - 1:e]
                qj.append(jnp.where(rows >= e, q * jnp.exp(jnp.minimum(b[p] - bj, 0.0)), 0.0).astype(BF16))
                kj.append(expand_k(jnp.where((rows >= e - sub) & (rows < e),
                                             k * jnp.exp(jnp.minimum(bj - b[p], 0.0)), 0.0)))
            qo.append(jnp.concatenate(qj, axis=1))
            ko.append(jnp.concatenate(kj, axis=1))
        qg.append((q * jnp.exp(b[p])).astype(BF16))
        kbt.append(jnp.transpose(_zpad(k * jnp.exp(b[p][c - 1:c] - b[p]), GLA_PAD)).astype(BF16))
        vexp.append(jnp.concatenate(
            [_zpad(jnp.where(lane_v == h, v, 0.0), GLA_PAD) for h in range(GLA_HEADS)], axis=0).astype(BF16))
        vpad.append(_zpad(v, GLA_PAD).astype(BF16))
    a = []
    for p in range(n):
        ap = jnp.where(amask, _nt(qd[p], kd[p]), 0.0)
        if nsub > 1:
            ap = ap + _nt(qo[p], ko[p])
        a.append(ap.astype(BF16))
    return [(_nn(a[p], vexp[p]), qg[p], kbt[p], vpad[p], bcol[p]) for p in range(n)]


def _gla_inter_multi(parts, states):
    srow = lax.broadcasted_iota(jnp.int32, (GK, GV), 0) >> 6
    scol = lax.broadcasted_iota(jnp.int32, (GK, GV), 1) >> 7
    diag = srow == scol
    outs = [o + _nn(qg, s.astype(BF16)) for (o, qg, _, _, _), s in zip(parts, states)]
    new = [s * jnp.exp(bcol) + jnp.where(diag, _nn(kbt, vp), 0.0) for (_, _, kbt, vp, bcol), s in zip(parts, states)]
    return outs, new


def _gla_out(o, gg, gnorm):
    outs = []
    for h in range(GLA_HEADS):
        oh = o[:, h * GLA_DV:(h + 1) * GLA_DV]
        ms = jnp.mean(oh * oh, axis=-1, keepdims=True)
        outs.append(oh * lax.rsqrt(ms + NORM_EPS))
    on = jnp.concatenate(outs, axis=1) * gnorm
    return on * (gg / (1.0 + jnp.exp(-gg)))


def _mixer_prompt_kernel(x_ref, ct_ref, st_ref, anorm_ref, win_ref, qkg_ref, sink_ref, wgate_ref, bgate_ref,
                         gnorm_ref, wout_ref, y_ref, klast_ref, vlast_ref, sfin_ref,
                         kprev_ref, vprev_ref, s_ref, *, tm, chunk, sub):
    i = pl.program_id(0)

    @pl.when(i == 0)
    def _():
        kprev_ref[...] = jnp.zeros_like(kprev_ref)
        vprev_ref[...] = jnp.zeros_like(vprev_ref)
        s_ref[...] = jnp.zeros_like(s_ref)

    x = x_ref[...]
    q, k, v, gq, gk, gv, gg, logf = _project(
        x, ct_ref[...], st_ref[...], anorm_ref[...], win_ref[...], qkg_ref[...], wgate_ref[...], bgate_ref[...])

    kall = jnp.concatenate([kprev_ref[...], k], axis=0)
    vall = jnp.concatenate([vprev_ref[...], v], axis=0)
    qi = lax.broadcasted_iota(jnp.int32, (WINDOW, 2 * WINDOW), 0)
    kj = lax.broadcasted_iota(jnp.int32, (WINDOW, 2 * WINDOW), 1)
    band = (kj >= qi) & (kj <= qi + WINDOW)
    nb = tm // WINDOW
    masks = [band & (kj >= jnp.where(i == 0, WINDOW, 0)) if j == 0 else band for j in range(nb)]
    blocks = _swa_multi([q[j * WINDOW:(j + 1) * WINDOW] for j in range(nb)],
                        [kall[j * WINDOW:(j + 2) * WINDOW] for j in range(nb)],
                        [vall[j * WINDOW:(j + 2) * WINDOW] for j in range(nb)], masks, sink_ref)
    a_out = jnp.concatenate(blocks, axis=0)
    kprev_ref[...] = k[tm - WINDOW:]
    vprev_ref[...] = v[tm - WINDOW:]
    klast_ref[...] = k[tm - WINDOW:]
    vlast_ref[...] = v[tm - WINDOW:]

    sls = [slice(c * chunk, (c + 1) * chunk) for c in range(tm // chunk)]
    parts = _gla_intra_multi([gq[sl] for sl in sls], [gk[sl] for sl in sls], [gv[sl] for sl in sls],
                             [logf[sl] for sl in sls], sub)
    s = s_ref[...]
    os_ = []
    for part in parts:
        (o,), (s,) = _gla_inter_multi([part], [s])
        os_.append(o)
    s_ref[...] = s
    for h in range(GLA_HEADS):
        sfin_ref[h] = s[h * GLA_DK:(h + 1) * GLA_DK, h * GLA_DV:(h + 1) * GLA_DV]
    g_out = _gla_out(jnp.concatenate(os_, axis=0), gg, gnorm_ref[...])

    mix = jnp.concatenate([a_out, g_out], axis=1).astype(BF16)
    y_ref[...] = x + _nn(mix, wout_ref[...])


def _const_spec(shape):
    return pl.BlockSpec(shape, lambda *_: (0,) * len(shape))


def _mixer_prompt(x, ct, st, wts, *, tm=MIXER_TM, chunk=GLA_CHUNK, sub=GLA_SUB):
    t = x.shape[0]
    tm = min(tm, t)
    row = lambda w: pl.BlockSpec((tm, w), lambda i: (i, 0))
    kern = functools.partial(_mixer_prompt_kernel, tm=tm, chunk=chunk, sub=sub)
    return pl.pallas_call(
        kern,
        grid=(t // tm,),
        in_specs=[row(D_MODEL), row(LANES), row(LANES)] + [_const_spec(w.shape) for w in wts],
        out_specs=[row(D_MODEL), _const_spec((WINDOW, LANES)), _const_spec((WINDOW, LANES)),
                   _const_spec((GLA_HEADS, GLA_DK, GLA_DV))],
        out_shape=[jax.ShapeDtypeStruct((t, D_MODEL), F32), jax.ShapeDtypeStruct((WINDOW, LANES), F32),
                   jax.ShapeDtypeStruct((WINDOW, LANES), F32),
                   jax.ShapeDtypeStruct((GLA_HEADS, GLA_DK, GLA_DV), F32)],
        scratch_shapes=[pltpu.VMEM((WINDOW, LANES), F32), pltpu.VMEM((WINDOW, LANES), F32),
                        pltpu.VMEM((GK, GV), F32)],
        compiler_params=pltpu.CompilerParams(dimension_semantics=("arbitrary",),
                                             vmem_limit_bytes=MIXER_VMEM),
        name="mixer_prompt",
    )(x, ct, st, *wts)


def _mixer_sample_kernel(x_ref, ct_ref, st_ref, anorm_ref, win_ref, qkg_ref, sink_ref, wgate_ref, bgate_ref,
                         gnorm_ref, wout_ref, ck_ref, cv_ref, sin_ref,
                         y_ref, kout_ref, vout_ref, sout_ref, *, sb, dec, par):
    x = x_ref[...]
    q, k, v, gq, gk, gv, gg, logf = _project(
        x, ct_ref[...], st_ref[...], anorm_ref[...], win_ref[...], qkg_ref[...], wgate_ref[...], bgate_ref[...])
    for b in range(sb):
        kout_ref[b] = jnp.concatenate([ck_ref[b, dec:, :], k[b * dec:(b + 1) * dec]], axis=0)
        vout_ref[b] = jnp.concatenate([cv_ref[b, dec:, :], v[b * dec:(b + 1) * dec]], axis=0)

    per = 8 // dec
    qi = lax.broadcasted_iota(jnp.int32, (8, 2 * WINDOW), 0) & (dec - 1)
    kj = lax.broadcasted_iota(jnp.int32, (8, 2 * WINDOW), 1)
    mask = (kj >= qi) & (kj <= qi + WINDOW)
    ztail = jnp.zeros((WINDOW - dec, LANES), F32)
    pad8 = lambda a: jnp.concatenate([a, jnp.zeros((8 - dec, a.shape[1]), F32)], axis=0)

    a_parts, o_parts = [], []
    for b0 in range(0, sb, par):
        seqs = range(b0, b0 + par)
        rows8 = [slice((b // per) * 8, (b // per) * 8 + 8) for b in seqs]
        own = [slice((b % per) * dec, (b % per) * dec + dec) for b in seqs]
        new = [slice(b * dec, (b + 1) * dec) for b in seqs]
        att = _swa_multi([q[r] for r in rows8],
                         [jnp.concatenate([ck_ref[b], k[nw], ztail], axis=0) for b, nw in zip(seqs, new)],
                         [jnp.concatenate([cv_ref[b], v[nw], ztail], axis=0) for b, nw in zip(seqs, new)],
                         [mask] * par, sink_ref)
        a_parts += [a[o] for a, o in zip(att, own)]
        states = []
        for b in seqs:
            st = sin_ref[b]
            srows = []
            for h in range(GLA_HEADS):
                parts = []
                if h > 0:
                    parts.append(jnp.zeros((GLA_DK, h * GLA_DV), F32))
                parts.append(st[h])
                if h < GLA_HEADS - 1:
                    parts.append(jnp.zeros((GLA_DK, (GLA_HEADS - 1 - h) * GLA_DV), F32))
                srows.append(jnp.concatenate(parts, axis=1))
            states.append(jnp.concatenate(srows, axis=0))
        intra = _gla_intra_multi([pad8(gq[nw]) for nw in new], [pad8(gk[nw]) for nw in new],
                                 [pad8(gv[nw]) for nw in new], [pad8(logf[nw]) for nw in new], 8)
        outs, states = _gla_inter_multi(intra, states)
        o_parts += [o[:dec] for o in outs]
        for b, s_new in zip(seqs, states):
            for h in range(GLA_HEADS):
                sout_ref[b, h] = s_new[h * GLA_DK:(h + 1) * GLA_DK, h * GLA_DV:(h + 1) * GLA_DV]

    g_out = _gla_out(jnp.concatenate(o_parts, axis=0), gg, gnorm_ref[...])
    mix = jnp.concatenate([jnp.concatenate(a_parts, axis=0), g_out], axis=1).astype(BF16)
    y_ref[...] = x + _nn(mix, wout_ref[...])


def _mixer_sample(x, ct, st, wts, ck, cv, s0, *, dec, sb=SAMPLE_SEQS, par=SAMPLE_PAR):
    n = x.shape[0]
    nseq = n // dec
    m = sb * dec
    row = lambda w: pl.BlockSpec((m, w), lambda i: (i, 0))
    seq3 = pl.BlockSpec((sb, WINDOW, LANES), lambda i: (i, 0, 0))
    seq4 = pl.BlockSpec((sb, GLA_HEADS, GLA_DK, GLA_DV), lambda i: (i, 0, 0, 0))
    kern = functools.partial(_mixer_sample_kernel, sb=sb, dec=dec, par=par)
    return pl.pallas_call(
        kern,
        grid=(nseq // sb,),
        in_specs=[row(D_MODEL), row(LANES), row(LANES)] + [_const_spec(w.shape) for w in wts] + [seq3, seq3, seq4],
        out_specs=[row(D_MODEL), seq3, seq3, seq4],
        out_shape=[jax.ShapeDtypeStruct((n, D_MODEL), F32), jax.ShapeDtypeStruct(ck.shape, F32),
                   jax.ShapeDtypeStruct(cv.shape, F32), jax.ShapeDtypeStruct(s0.shape, F32)],
        compiler_params=pltpu.CompilerParams(dimension_semantics=("arbitrary",),
                                             vmem_limit_bytes=MIXER_VMEM),
        name="mixer_sample",
    )(x, ct, st, *wts, ck, cv, s0)


def _sort_network(n):
    pairs = []

    def merge(lo, m, r):
        step = 2 * r
        if step < m:
            merge(lo, m, step)
            merge(lo + r, m, step)
            pairs.extend((i, i + r) for i in range(lo + r, lo + m - r, step))
        else:
            pairs.append((lo, lo + r))

    def sort(lo, m):
        if m > 1:
            sort(lo, m // 2)
            sort(lo + m // 2, m // 2)
            merge(lo, m, 1)

    sort(0, n)
    return pairs


def _top16(s):
    t = s.shape[1]
    nv = PEER_NKEYS // 8
    sub8 = lax.broadcasted_iota(jnp.int32, (8, t), 0).astype(F32)
    v = [s[8 * k:8 * k + 8] for k in range(nv)]
    ix = [sub8 + float(8 * k) for k in range(nv)]
    for p, q in _sort_network(nv):
        swap = (v[q] > v[p]) | ((v[q] == v[p]) & (ix[q] < ix[p]))
        v[p], v[q] = jnp.where(swap, v[q], v[p]), jnp.where(swap, v[p], v[q])
        ix[p], ix[q] = jnp.where(swap, ix[q], ix[p]), jnp.where(swap, ix[p], ix[q])
    vals, idxs = [], []
    for n in range(PEER_TOPK):
        m = jnp.max(v[0], axis=0, keepdims=True)
        imin = jnp.min(jnp.where(v[0] == m, ix[0], float(PEER_NKEYS)), axis=0, keepdims=True)
        hit = ix[0] == imin
        vals.append(m)
        idxs.append(imin)
        for k in range(PEER_TOPK - 1 - n):
            v[k] = jnp.where(hit, v[k + 1], v[k])
            ix[k] = jnp.where(hit, ix[k + 1], ix[k])
    return jnp.concatenate(vals, axis=0), jnp.concatenate(idxs, axis=0)


def _pair_top16(a, ia, b, ib):
    t = a.shape[1]
    sub8 = lax.broadcasted_iota(jnp.int32, (8, t), 0).astype(F32)
    ia_lo = ia[0:8] * float(PEER_NKEYS)
    lv, le = [], []
    for y in range(PEER_TOPK):
        nx = PEER_TOPK // (y + 1)
        val = a[0:8] + b[y:y + 1]
        lv.append(val if nx >= 8 else jnp.where(sub8 < float(nx), val, NEG_INF))
        le.append(ia_lo + ib[y:y + 1])
    hv = a[8:16] + b[0:1]
    he = ia[8:16] * float(PEER_NKEYS) + ib[0:1]
    code_lo = sub8 * float(PEER_TOPK)
    code_hi = (sub8 + 8.0) * float(PEER_TOPK)
    vals, sel = [], []
    for n in range(PEER_TOPK):
        m = jnp.max(jnp.maximum(lv[0], hv), axis=0, keepdims=True)
        cmin = jnp.min(jnp.minimum(jnp.where(lv[0] == m, code_lo, 1e9), jnp.where(hv == m, code_hi, 1e9)),
                       axis=0, keepdims=True)
        hit_lo = code_lo == cmin
        hit_hi = code_hi == cmin
        sel.append(jnp.max(jnp.maximum(jnp.where(hit_lo, le[0], -1.0), jnp.where(hit_hi, he, -1.0)),
                           axis=0, keepdims=True))
        vals.append(m)
        for y in range(PEER_TOPK - 1 - n):
            lv[y] = jnp.where(hit_lo, lv[y + 1], lv[y])
            le[y] = jnp.where(hit_lo, le[y + 1], le[y])
        code_lo = jnp.where(hit_lo, code_lo + 1.0, code_lo)
        hv = jnp.where(hit_hi, NEG_INF, hv)
    return jnp.concatenate(vals, axis=0), jnp.concatenate(sel, axis=0)


def _peer_route_gates_kernel(xa_ref, xb_ref, fnorm_ref, wqt_ref, keys_ref, h_ref, w_ref,
                             qt_s, e_s, p_s, i_s, j_s, g_s, t_s, *, tm, nta, nt):
    t = pl.program_id(0)
    cur = lax.rem(t, 2)
    prev = 1 - cur
    group = tm // PEER_HEADS

    @pl.when(t == 0)
    def _():
        i_s[...] = jnp.zeros_like(i_s)
        j_s[...] = jnp.zeros_like(j_s)
        g_s[...] = jnp.zeros_like(g_s)

    x = jnp.where(jnp.minimum(t, nt - 1) < nta, xa_ref[...], xb_ref[...])
    hb = _rms(x, fnorm_ref[...]).astype(BF16)
    h_ref[...] = hb
    qt_s[...] = _nt(wqt_ref[...], hb)
    sub = lax.broadcasted_iota(jnp.int32, (PEER_NKEYS, LANES), 0).astype(BF16)
    sub_odd = (lax.broadcasted_iota(jnp.int32, (W_ROWS_ODD, LANES), 0) - W_SHIFT).astype(BF16)
    one = jnp.ones((W_ROWS_ODD, LANES), BF16)
    zero = jnp.zeros((W_ROWS_ODD, LANES), BF16)

    def emit(n0, q):
        for i in range(PEER_NKEYS):
            a = t_s[pl.ds((16 * q) * W_PITCH + i, 8, stride=W_PITCH), :]
            b = t_s[pl.ds((16 * q + 8) * W_PITCH + i, 8, stride=W_PITCH), :]
            w_ref[pl.ds(n0 + 16 * q, 16), i * PEER_NKEYS:(i + 1) * PEER_NKEYS] = (
                jnp.concatenate([a, b], axis=0).astype(BF16))

    def head(hh, carry):
        r0 = pl.multiple_of(hh * 2 * PEER_NKEYS, 2 * PEER_NKEYS)
        s0 = _nn(keys_ref[2 * hh], qt_s[pl.ds(r0, PEER_NKEYS), :].astype(BF16))
        s1 = _nn(keys_ref[2 * hh + 1], qt_s[pl.ds(r0 + PEER_NKEYS, PEER_NKEYS), :].astype(BF16))
        n0 = pl.multiple_of(hh * group, group)
        for tok in range(group):
            ri = i_s[prev, pl.ds(n0 + tok, 1), :].astype(BF16)
            rj = j_s[prev, pl.ds(n0 + tok, 1), :].astype(BF16)
            rg = jnp.broadcast_to(g_s[prev, pl.ds(n0 + tok, 1), :].astype(BF16), sub.shape)
            bt = jnp.where(sub == rj, rg, zero[:PEER_NKEYS])
            if tok % 2 == 0:
                at = jnp.where(sub == ri, one[:PEER_NKEYS], zero[:PEER_NKEYS])
                t_s[tok * W_PITCH:tok * W_PITCH + PEER_NKEYS, :] = _nt(at, bt)
            else:
                at = jnp.where(sub_odd == ri, one, zero)
                t_s[tok * W_PITCH - W_SHIFT:tok * W_PITCH - W_SHIFT + W_ROWS_ODD, :] = _nt(at, bt)
        a, ia = _top16(s0)
        for q in range(group // 32):
            emit(n0, 2 * q)
        b, ib = _top16(s1)
        for q in range(group // 32):
            emit(n0, 2 * q + 1)
        c, e = _pair_top16(a, ia, b, ib)
        p = jnp.exp(c - c[0:1])
        rows = pl.ds(pl.multiple_of(hh * PEER_TOPK, PEER_TOPK), PEER_TOPK)
        e_s[rows, :] = e
        p_s[rows, :] = p / jnp.sum(p, axis=0, keepdims=True)
        return carry

    lax.fori_loop(0, PEER_HEADS, head, 0)
    et = jnp.transpose(e_s[...])
    it = jnp.floor(et * (1.0 / PEER_NKEYS))
    i_s[cur] = it
    j_s[cur] = et - it * float(PEER_NKEYS)
    g_s[cur] = jnp.transpose(p_s[...])


def _peer_route_gates(xa, xb, fnorm, wq, keys, *, tm=ROUTE_TM):
    assert xa.shape[0] % tm == 0 and xb.shape[0] % tm == 0
    nta, ntb = xa.shape[0] // tm, xb.shape[0] // tm
    nt = nta + ntb
    t = nt * tm
    nsel = PEER_HEADS * PEER_TOPK
    n_exp = PEER_NKEYS * PEER_NKEYS
    group = tm // PEER_HEADS
    assert group % 32 == 0
    cur_tile = lambda i: (jnp.minimum(i, nt - 1), 0)
    return pl.pallas_call(
        functools.partial(_peer_route_gates_kernel, tm=tm, nta=nta, nt=nt),
        grid=(nt + 1,),
        in_specs=[pl.BlockSpec((tm, D_MODEL), lambda i: (jnp.minimum(i, nta - 1), 0)),
                  pl.BlockSpec((tm, D_MODEL), lambda i: (jnp.clip(i - nta, 0, ntb - 1), 0)),
                  _const_spec(fnorm.shape), _const_spec(wq.shape), _const_spec(keys.shape)],
        out_specs=[pl.BlockSpec((tm, D_MODEL), cur_tile),
                   pl.BlockSpec((tm, n_exp), lambda i: (jnp.maximum(i - 1, 0), 0))],
        out_shape=[jax.ShapeDtypeStruct((t, D_MODEL), BF16), jax.ShapeDtypeStruct((t, n_exp), BF16)],
        scratch_shapes=[pltpu.VMEM((wq.shape[0], tm), F32), pltpu.VMEM((nsel, tm), F32),
                        pltpu.VMEM((nsel, tm), F32),
                        pltpu.VMEM((2, tm, nsel), F32), pltpu.VMEM((2, tm, nsel), F32),
                        pltpu.VMEM((2, tm, nsel), F32),
                        pltpu.VMEM(((group - 1) * W_PITCH - W_SHIFT + W_ROWS_ODD, LANES), F32)],
        compiler_params=pltpu.CompilerParams(dimension_semantics=("arbitrary",),
                                             vmem_limit_bytes=PEER_VMEM),
        name="peer_route_gates",
    )(xa, xb, fnorm, wq, keys)


def _peer_dense_kernel(h_ref, x_ref, w_ref, u_ref, v_ref, y_ref):
    @pl.when(pl.program_id(1) == 0)
    def _():
        y_ref[...] = x_ref[...]

    act = _nt(h_ref[...], u_ref[...].astype(BF16))
    gelu = 0.5 * act * (1.0 + lax.erf(act * (2.0 ** -0.5)))
    z = w_ref[...] * gelu.astype(BF16)
    y_ref[...] += _nn(z, v_ref[...].astype(BF16))


def _peer_dense(h, x, w, u, v, *, row0=0, tm=DENSE_TM, te=DENSE_TE):
    t = x.shape[0]
    tm = min(tm, t)
    assert row0 % tm == 0
    n_exp = u.shape[0]
    tok = pl.BlockSpec((tm, D_MODEL), lambda i, e: (i, 0))
    exp = pl.BlockSpec((te, D_MODEL), lambda i, e: (e, 0))
    return pl.pallas_call(
        _peer_dense_kernel,
        grid=(t // tm, n_exp // te),
        in_specs=[pl.BlockSpec((tm, D_MODEL), lambda i, e: (i + row0 // tm, 0)), tok,
                  pl.BlockSpec((tm, te), lambda i, e: (i + row0 // tm, e)), exp, exp],
        out_specs=tok,
        out_shape=jax.ShapeDtypeStruct((t, D_MODEL), F32),
        compiler_params=pltpu.CompilerParams(dimension_semantics=("arbitrary", "arbitrary"),
                                             vmem_limit_bytes=PEER_VMEM),
        name="peer_dense",
    )(h, x, w, u, v)


def _rope_tables(pos):
    half = ROPE_DIM // 2
    inv_freq = ROPE_THETA ** (-jnp.arange(half, dtype=F32) / half)
    ang = pos.astype(F32)[:, None] * inv_freq[None, :]
    cos, sin = jnp.cos(ang), jnp.sin(ang)
    n = pos.shape[0]
    c64 = jnp.concatenate([cos, cos, jnp.ones((n, HEAD_DIM - ROPE_DIM), F32)], axis=1)
    s64 = jnp.concatenate([-sin, sin, jnp.zeros((n, HEAD_DIM - ROPE_DIM), F32)], axis=1)
    return jnp.tile(c64, (1, 2)), jnp.tile(s64, (1, 2))


def kernel(x_prompt, x_sample, cache_swa_k, cache_swa_v, state_gla, attn_norm, w_in, q_norm, k_norm, attn_sinks,
           w_gate, b_gate, gla_norm, w_out, ffn_norm, peer_wq, peer_keys, peer_u, peer_v):
    depth = w_in.shape[0]
    assert depth == 1 and x_prompt.shape[0] == 1
    seq = x_prompt.shape[1]
    nseq, dec = x_sample.shape[0], x_sample.shape[1]
    xp = x_prompt[0]
    xs = x_sample.reshape(nseq * dec, D_MODEL)
    l = 0

    win = jnp.pad(w_in[l], ((0, 0), (0, C_END - w_in.shape[2]))).astype(BF16)
    qkg = jnp.concatenate([jnp.tile(q_norm[l], SWA_Q_HEADS), jnp.tile(k_norm[l], 2)])[None, :]
    sinks = jnp.broadcast_to(attn_sinks[l][:, None], (SWA_Q_HEADS, LANES))
    wgate = jnp.pad(w_gate[l], ((0, LANES - GLA_LOWRANK), (0, 0))).astype(BF16)
    wts = (attn_norm[l][None, :], win, qkg, sinks, wgate, b_gate[l][None, :],
           jnp.tile(gla_norm[l], GLA_HEADS)[None, :], w_out[l].astype(BF16))
    fnorm = ffn_norm[l][None, :]
    wq = jnp.transpose(peer_wq[l]).astype(BF16)
    keys = peer_keys[l].reshape(PEER_HEADS * 2, PEER_NKEYS, peer_keys.shape[-1]).astype(BF16)

    ct_p, st_p = _rope_tables(jnp.arange(seq, dtype=jnp.int32))
    ct_s, st_s = _rope_tables(PAST_LEN + jnp.arange(nseq * dec, dtype=jnp.int32) % dec)

    xp2, kp, vp, sp = _mixer_prompt(xp, ct_p, st_p, wts)
    ck = cache_swa_k[l].reshape(nseq, WINDOW, LANES)
    cv = cache_swa_v[l].reshape(nseq, WINDOW, LANES)
    xs2, ck2, cv2, ss = _mixer_sample(xs, ct_s, st_s, wts, ck, cv, state_gla[l], dec=dec)

    h_all, w_all = _peer_route_gates(xp2, xs2, fnorm, wq, keys)
    yp = _peer_dense(h_all, xp2, w_all, peer_u[l], peer_v[l])
    ys = _peer_dense(h_all, xs2, w_all, peer_u[l], peer_v[l], row0=seq, te=2 * DENSE_TE)

    kv_shape = (1, 1, WINDOW, 2, HEAD_DIM)
    return (yp[None], ys.reshape(nseq, dec, D_MODEL),
            kp.reshape(kv_shape), vp.reshape(kv_shape), sp[None, None],
            ck2.reshape(1, nseq, WINDOW, 2, HEAD_DIM), cv2.reshape(1, nseq, WINDOW, 2, HEAD_DIM), ss[None])
```

```python
import functools

import jax
import jax.numpy as jnp
from jax import lax
from jax.experimental import pallas as pl
from jax.experimental.pallas import tpu as pltpu

F32 = jnp.float32
BF16 = jnp.bfloat16

D_MODEL = 1024
HEAD_DIM = 64
SWA_Q_HEADS = 8
WINDOW = 128
ROPE_THETA = 500000.0
ROPE_DIM = 16
PAST_LEN = 16384
GLA_HEADS = 4
GLA_DK = 64
GLA_DV = 128
GLA_LOWRANK = 16
GLA_GATE_NORM = 16.0
PEER_HEADS = 8
PEER_NKEYS = 128
PEER_TOPK = 16
NORM_EPS = 1e-6

LANES = 128
GK = GLA_HEADS * GLA_DK
GV = GLA_HEADS * GLA_DV
C_Q, C_K, C_V, C_GQ, C_GK, C_GV, C_GG, C_GA, C_END = 0, 512, 640, 768, 1024, 1280, 1792, 2304, 2432
GLA_PAD = 128
W_PITCH = 132
W_SHIFT = W_PITCH % 8
W_ROWS_ODD = 144
NEG_INF = float("-inf")

MIB = 1024 * 1024
MIXER_TM = 256
GLA_CHUNK = 128
GLA_SUB = 32
SAMPLE_SEQS = 16
SAMPLE_PAR = 8
ROUTE_TM = 256
DENSE_TM = 1024
DENSE_TE = 1024
MIXER_VMEM = 48 * MIB
PEER_VMEM = 56 * MIB


def _nn(a, b, precision=None):
    return jnp.dot(a, b, preferred_element_type=F32, precision=precision)


def _nt(a, b):
    return lax.dot_general(a, b, (((1,), (1,)), ((), ())), preferred_element_type=F32)


def _rms(x, gain):
    ms = jnp.mean(x * x, axis=-1, keepdims=True)
    return x * lax.rsqrt(ms + NORM_EPS) * gain


def _head_norm_rope(x, gain, ctab, stab):
    lane = lax.broadcasted_iota(jnp.int32, x.shape, 1)
    lo = lane < HEAD_DIM
    sq = x * x
    ms_lo = jnp.sum(jnp.where(lo, sq, 0.0), axis=-1, keepdims=True)
    ms_hi = jnp.sum(jnp.where(lo, 0.0, sq), axis=-1, keepdims=True)
    ms = jnp.where(lo, ms_lo, ms_hi) * (1.0 / HEAD_DIM)
    xn = x * lax.rsqrt(ms + NORM_EPS) * gain
    first = (lane & (HEAD_DIM - 1)) < (ROPE_DIM // 2)
    partner = jnp.where(first, pltpu.roll(xn, LANES - ROPE_DIM // 2, 1), pltpu.roll(xn, ROPE_DIM // 2, 1))
    return xn * ctab + partner * stab


def _project(x, ct, st, anorm, win, qkg, wgate, bgate):
    h = _rms(x, anorm).astype(BF16)
    p = _nn(h, win)
    qk = [
        _head_norm_rope(p[:, c * LANES:(c + 1) * LANES], qkg[:, c * LANES:(c + 1) * LANES], ct, st)
        for c in range(C_V // LANES)
    ]
    q = jnp.concatenate(qk[:4], axis=1) * (HEAD_DIM ** -0.5)
    k = qk[4]
    v = p[:, C_V:C_GQ]
    gq = p[:, C_GQ:C_GK] * (GLA_DK ** -0.5)
    gk = p[:, C_GK:C_GV]
    gv = p[:, C_GV:C_GG]
    gg = p[:, C_GG:C_GA]
    z = _nn(p[:, C_GA:C_END].astype(BF16), wgate) + bgate
    logf = (jnp.minimum(z, 0.0) - jnp.log(1.0 + jnp.exp(-jnp.abs(z)))) * (1.0 / GLA_GATE_NORM)
    return q, k, v, gq, gk, gv, gg, logf


def _swa_multi(qs, k2s, v2s, masks, sink_ref):
    n = len(qs)
    m = qs[0].shape[0]
    nk = k2s[0].shape[0]
    lo = lax.broadcasted_iota(jnp.int32, (nk, LANES), 1) < HEAD_DIM
    olane = lax.broadcasted_iota(jnp.int32, (2 * m, LANES), 1) < HEAD_DIM
    top = lax.broadcasted_iota(jnp.int32, (2 * m, 1), 0) < m
    zero = jnp.zeros((nk, LANES), F32)
    kexp, vexp = {}, {}
    for p in range(n):
        krot = pltpu.roll(k2s[p], HEAD_DIM, 1)
        vrot = pltpu.roll(v2s[p], HEAD_DIM, 1)
        for g in range(2):
            ka, kb = (k2s[p], krot) if g == 0 else (krot, k2s[p])
            va, vb = (v2s[p], vrot) if g == 0 else (vrot, v2s[p])
            kexp[p, g] = jnp.concatenate([jnp.where(lo, ka, zero), jnp.where(lo, zero, kb)], axis=0).astype(BF16)
            vexp[p, g] = jnp.concatenate([jnp.where(lo, va, zero), jnp.where(lo, zero, vb)], axis=0).astype(BF16)
    s = {}
    for p in range(n):
        for g in range(2):
            qq = jnp.concatenate([qs[p][:, (2 * g) * LANES:(2 * g + 1) * LANES],
                                  qs[p][:, (2 * g + 1) * LANES:(2 * g + 2) * LANES]], axis=0).astype(BF16)
            s[p, g] = _nt(qq, kexp[p, g])
    probs, scale = {}, {}
    for p in range(n):
        mask2 = jnp.concatenate([masks[p], masks[p]], axis=0)
        for g in range(2):
            ps, rs = [], []
            for hh in range(2):
                sh = jnp.where(mask2, s[p, g][:, hh * nk:(hh + 1) * nk], NEG_INF)
                ha, hb = 4 * g + hh, 4 * g + 2 + hh
                sink = jnp.where(top, sink_ref[ha:ha + 1, 0:1], sink_ref[hb:hb + 1, 0:1])
                mx = jnp.maximum(jnp.max(sh, axis=-1, keepdims=True), sink)
                pe = jnp.exp(sh - mx)
                den = jnp.sum(pe, axis=-1, keepdims=True) + jnp.exp(sink - mx)
                ps.append(pe)
                rs.append(1.0 / den)
            probs[p, g] = jnp.concatenate(ps, axis=1).astype(BF16)
            scale[p, g] = jnp.where(olane, rs[0], rs[1])
    outs = []
    for p in range(n):
        o = [_nn(probs[p, g], vexp[p, g]) * scale[p, g] for g in range(2)]
        outs.append(jnp.concatenate([o[0][:m], o[0][m:], o[1][:m], o[1][m:]], axis=1))
    return outs


def _zpad(a, rows):
    if a.shape[0] == rows:
        return a
    return jnp.concatenate([a, jnp.zeros((rows - a.shape[0], a.shape[1]), a.dtype)], axis=0)


def _gla_intra_multi(qs, ks, vs, fs, sub):
    n = len(qs)
    c = qs[0].shape[0]
    nsub = c // sub
    shift = sub.bit_length() - 1
    r = lax.broadcasted_iota(jnp.int32, (c, GLA_PAD), 0)
    cc = lax.broadcasted_iota(jnp.int32, (c, GLA_PAD), 1)
    causal = cc <= r
    tri2 = jnp.concatenate([causal, causal & (cc >= ((r >> shift) << shift))], axis=0).astype(BF16)
    lane_k = lax.broadcasted_iota(jnp.int32, (c, GK), 1) >> 6
    lane_v = lax.broadcasted_iota(jnp.int32, (c, GV), 1) >> 7
    rows = lax.broadcasted_iota(jnp.int32, (c, GK), 0)
    acol = lax.broadcasted_iota(jnp.int32, (c, GLA_HEADS * GLA_PAD), 1) & (GLA_PAD - 1)
    arow = lax.broadcasted_iota(jnp.int32, (c, GLA_HEADS * GLA_PAD), 0)
    amask = (acol <= arow) & (acol >= ((arow >> shift) << shift))

    def expand_k(kt):
        return jnp.concatenate(
            [_zpad(jnp.where(lane_k == h, kt, 0.0), GLA_PAD) for h in range(GLA_HEADS)], axis=0).astype(BF16)

    b, bl, bcol = [], [], []
    for p in range(n):
        fp = _zpad(fs[p], GLA_PAD)
        f_hi = fp.astype(BF16)
        r1 = fp - f_hi.astype(F32)
        f_mid = r1.astype(BF16)
        f_lo = (r1 - f_mid.astype(F32)).astype(BF16)
        cs = _nn(tri2, jnp.concatenate([f_hi, f_mid, f_lo], axis=1))
        tot = (cs[:, :GK] + cs[:, GK:2 * GK]) + cs[:, 2 * GK:]
        b.append(tot[:c])
        bl.append(tot[c:])
        bcol.append(jnp.sum(jnp.transpose(fp), axis=1, keepdims=True))
    qd, kd, qo, ko, qg, kbt, vexp, vpad = [], [], [], [], [], [], [], []
    for p in range(n):
        q, k, v = qs[p], ks[p], vs[p]
        qd.append((q * jnp.exp(bl[p])).astype(BF16))
        kd.append(expand_k(k * jnp.exp(-bl[p])))
        if nsub > 1:
            qj, kj = [], []
            for j in range(nsub - 1):
                e = (j + 1) * sub
                bj = b[p][e - 1:e]
                qj.append(jnp.where(rows >= e, q * jnp.exp(jnp.minimum(b[p] - bj, 0.0)), 0.0).astype(BF16))
                kj.append(expand_k(jnp.where((rows >= e - sub) & (rows < e),
                                             k * jnp.exp(jnp.minimum(bj - b[p], 0.0)), 0.0)))
            qo.append(jnp.concatenate(qj, axis=1))
            ko.append(jnp.concatenate(kj, axis=1))
        qg.append((q * jnp.exp(b[p])).astype(BF16))
        kbt.append(jnp.transpose(_zpad(k * jnp.exp(b[p][c - 1:c] - b[p]), GLA_PAD)).astype(BF16))
        vexp.append(jnp.concatenate(
            [_zpad(jnp.where(lane_v == h, v, 0.0), GLA_PAD) for h in range(GLA_HEADS)], axis=0).astype(BF16))
        vpad.append(_zpad(v, GLA_PAD).astype(BF16))
    a = []
    for p in range(n):
        ap = jnp.where(amask, _nt(qd[p], kd[p]), 0.0)
        if nsub > 1:
            ap = ap + _nt(qo[p], ko[p])
        a.append(ap.astype(BF16))
    return [(_nn(a[p], vexp[p]), qg[p], kbt[p], vpad[p], bcol[p]) for p in range(n)]


def _gla_inter_multi(parts, states):
    srow = lax.broadcasted_iota(jnp.int32, (GK, GV), 0) >> 6
    scol = lax.broadcasted_iota(jnp.int32, (GK, GV), 1) >> 7
    diag = srow == scol
    outs = [o + _nn(qg, s.astype(BF16)) for (o, qg, _, _, _), s in zip(parts, states)]
    new = [s * jnp.exp(bcol) + jnp.where(diag, _nn(kbt, vp), 0.0) for (_, _, kbt, vp, bcol), s in zip(parts, states)]
    return outs, new


def _gla_out(o, gg, gnorm):
    outs = []
    for h in range(GLA_HEADS):
        oh = o[:, h * GLA_DV:(h + 1) * GLA_DV]
        ms = jnp.mean(oh * oh, axis=-1, keepdims=True)
        outs.append(oh * lax.rsqrt(ms + NORM_EPS))
    on = jnp.concatenate(outs, axis=1) * gnorm
    return on * (gg / (1.0 + jnp.exp(-gg)))


def _mixer_prompt_kernel(x_ref, ct_ref, st_ref, anorm_ref, win_ref, qkg_ref, sink_ref, wgate_ref, bgate_ref,
                         gnorm_ref, wout_ref, y_ref, klast_ref, vlast_ref, sfin_ref,
                         kprev_ref, vprev_ref, s_ref, *, tm, chunk, sub):
    i = pl.program_id(0)

    @pl.when(i == 0)
    def _():
        kprev_ref[...] = jnp.zeros_like(kprev_ref)
        vprev_ref[...] = jnp.zeros_like(vprev_ref)
        s_ref[...] = jnp.zeros_like(s_ref)

    x = x_ref[...]
    q, k, v, gq, gk, gv, gg, logf = _project(
        x, ct_ref[...], st_ref[...], anorm_ref[...], win_ref[...], qkg_ref[...], wgate_ref[...], bgate_ref[...])

    kall = jnp.concatenate([kprev_ref[...], k], axis=0)
    vall = jnp.concatenate([vprev_ref[...], v], axis=0)
    qi = lax.broadcasted_iota(jnp.int32, (WINDOW, 2 * WINDOW), 0)
    kj = lax.broadcasted_iota(jnp.int32, (WINDOW, 2 * WINDOW), 1)
    band = (kj >= qi) & (kj <= qi + WINDOW)
    nb = tm // WINDOW
    masks = [band & (kj >= jnp.where(i == 0, WINDOW, 0)) if j == 0 else band for j in range(nb)]
    blocks = _swa_multi([q[j * WINDOW:(j + 1) * WINDOW] for j in range(nb)],
                        [kall[j * WINDOW:(j + 2) * WINDOW] for j in range(nb)],
                        [vall[j * WINDOW:(j + 2) * WINDOW] for j in range(nb)], masks, sink_ref)
    a_out = jnp.concatenate(blocks, axis=0)
    kprev_ref[...] = k[tm - WINDOW:]
    vprev_ref[...] = v[tm - WINDOW:]
    klast_ref[...] = k[tm - WINDOW:]
    vlast_ref[...] = v[tm - WINDOW:]

    sls = [slice(c * chunk, (c + 1) * chunk) for c in range(tm // chunk)]
    parts = _gla_intra_multi([gq[sl] for sl in sls], [gk[sl] for sl in sls], [gv[sl] for sl in sls],
                             [logf[sl] for sl in sls], sub)
    s = s_ref[...]
    os_ = []
    for part in parts:
        (o,), (s,) = _gla_inter_multi([part], [s])
        os_.append(o)
    s_ref[...] = s
    for h in range(GLA_HEADS):
        sfin_ref[h] = s[h * GLA_DK:(h + 1) * GLA_DK, h * GLA_DV:(h + 1) * GLA_DV]
    g_out = _gla_out(jnp.concatenate(os_, axis=0), gg, gnorm_ref[...])

    mix = jnp.concatenate([a_out, g_out], axis=1).astype(BF16)
    y_ref[...] = x + _nn(mix, wout_ref[...])


def _const_spec(shape):
    return pl.BlockSpec(shape, lambda *_: (0,) * len(shape))


def _mixer_prompt(x, ct, st, wts, *, tm=MIXER_TM, chunk=GLA_CHUNK, sub=GLA_SUB):
    t = x.shape[0]
    tm = min(tm, t)
    row = lambda w: pl.BlockSpec((tm, w), lambda i: (i, 0))
    kern = functools.partial(_mixer_prompt_kernel, tm=tm, chunk=chunk, sub=sub)
    return pl.pallas_call(
        kern,
        grid=(t // tm,),
        in_specs=[row(D_MODEL), row(LANES), row(LANES)] + [_const_spec(w.shape) for w in wts],
        out_specs=[row(D_MODEL), _const_spec((WINDOW, LANES)), _const_spec((WINDOW, LANES)),
                   _const_spec((GLA_HEADS, GLA_DK, GLA_DV))],
        out_shape=[jax.ShapeDtypeStruct((t, D_MODEL), F32), jax.ShapeDtypeStruct((WINDOW, LANES), F32),
                   jax.ShapeDtypeStruct((WINDOW, LANES), F32),
                   jax.ShapeDtypeStruct((GLA_HEADS, GLA_DK, GLA_DV), F32)],
        scratch_shapes=[pltpu.VMEM((WINDOW, LANES), F32), pltpu.VMEM((WINDOW, LANES), F32),
                        pltpu.VMEM((GK, GV), F32)],
        compiler_params=pltpu.CompilerParams(dimension_semantics=("arbitrary",),
                                             vmem_limit_bytes=MIXER_VMEM),
        name="mixer_prompt",
    )(x, ct, st, *wts)


def _mixer_sample_kernel(x_ref, ct_ref, st_ref, anorm_ref, win_ref, qkg_ref, sink_ref, wgate_ref, bgate_ref,
                         gnorm_ref, wout_ref, ck_ref, cv_ref, sin_ref,
                         y_ref, kout_ref, vout_ref, sout_ref, *, sb, dec, par):
    x = x_ref[...]
    q, k, v, gq, gk, gv, gg, logf = _project(
        x, ct_ref[...], st_ref[...], anorm_ref[...], win_ref[...], qkg_ref[...], wgate_ref[...], bgate_ref[...])
    for b in range(sb):
        kout_ref[b] = jnp.concatenate([ck_ref[b, dec:, :], k[b * dec:(b + 1) * dec]], axis=0)
        vout_ref[b] = jnp.concatenate([cv_ref[b, dec:, :], v[b * dec:(b + 1) * dec]], axis=0)

    per = 8 // dec
    qi = lax.broadcasted_iota(jnp.int32, (8, 2 * WINDOW), 0) & (dec - 1)
    kj = lax.broadcasted_iota(jnp.int32, (8, 2 * WINDOW), 1)
    mask = (kj >= qi) & (kj <= qi + WINDOW)
    ztail = jnp.zeros((WINDOW - dec, LANES), F32)
    pad8 = lambda a: jnp.concatenate([a, jnp.zeros((8 - dec, a.shape[1]), F32)], axis=0)

    a_parts, o_parts = [], []
    for b0 in range(0, sb, par):
        seqs = range(b0, b0 + par)
        rows8 = [slice((b // per) * 8, (b // per) * 8 + 8) for b in seqs]
        own = [slice((b % per) * dec, (b % per) * dec + dec) for b in seqs]
        new = [slice(b * dec, (b + 1) * dec) for b in seqs]
        att = _swa_multi([q[r] for r in rows8],
                         [jnp.concatenate([ck_ref[b], k[nw], ztail], axis=0) for b, nw in zip(seqs, new)],
                         [jnp.concatenate([cv_ref[b], v[nw], ztail], axis=0) for b, nw in zip(seqs, new)],
                         [mask] * par, sink_ref)
        a_parts += [a[o] for a, o in zip(att, own)]
        states = []
        for b in seqs:
            st = sin_ref[b]
            srows = []
            for h in range(GLA_HEADS):
                parts = []
                if h > 0:
                    parts.append(jnp.zeros((GLA_DK, h * GLA_DV), F32))
                parts.append(st[h])
                if h < GLA_HEADS - 1:
                    parts.append(jnp.zeros((GLA_DK, (GLA_HEADS - 1 - h) * GLA_DV), F32))
                srows.append(jnp.concatenate(parts, axis=1))
            states.append(jnp.concatenate(srows, axis=0))
        intra = _gla_intra_multi([pad8(gq[nw]) for nw in new], [pad8(gk[nw]) for nw in new],
                                 [pad8(gv[nw]) for nw in new], [pad8(logf[nw]) for nw in new], 8)
        outs, states = _gla_inter_multi(intra, states)
        o_parts += [o[:dec] for o in outs]
        for b, s_new in zip(seqs, states):
            for h in range(GLA_HEADS):
                sout_ref[b, h] = s_new[h * GLA_DK:(h + 1) * GLA_DK, h * GLA_DV:(h + 1) * GLA_DV]

    g_out = _gla_out(jnp.concatenate(o_parts, axis=0), gg, gnorm_ref[...])
    mix = jnp.concatenate([jnp.concatenate(a_parts, axis=0), g_out], axis=1).astype(BF16)
    y_ref[...] = x + _nn(mix, wout_ref[...])


def _mixer_sample(x, ct, st, wts, ck, cv, s0, *, dec, sb=SAMPLE_SEQS, par=SAMPLE_PAR):
    n = x.shape[0]
    nseq = n // dec
    m = sb * dec
    row = lambda w: pl.BlockSpec((m, w), lambda i: (i, 0))
    seq3 = pl.BlockSpec((sb, WINDOW, LANES), lambda i: (i, 0, 0))
    seq4 = pl.BlockSpec((sb, GLA_HEADS, GLA_DK, GLA_DV), lambda i: (i, 0, 0, 0))
    kern = functools.partial(_mixer_sample_kernel, sb=sb, dec=dec, par=par)
    return pl.pallas_call(
        kern,
        grid=(nseq // sb,),
        in_specs=[row(D_MODEL), row(LANES), row(LANES)] + [_const_spec(w.shape) for w in wts] + [seq3, seq3, seq4],
        out_specs=[row(D_MODEL), seq3, seq3, seq4],
        out_shape=[jax.ShapeDtypeStruct((n, D_MODEL), F32), jax.ShapeDtypeStruct(ck.shape, F32),
                   jax.ShapeDtypeStruct(cv.shape, F32), jax.ShapeDtypeStruct(s0.shape, F32)],
        compiler_params=pltpu.CompilerParams(dimension_semantics=("arbitrary",),
                                             vmem_limit_bytes=MIXER_VMEM),
        name="mixer_sample",
    )(x, ct, st, *wts, ck, cv, s0)


def _sort_network(n):
    pairs = []

    def merge(lo, m, r):
        step = 2 * r
        if step < m:
            merge(lo, m, step)
            merge(lo + r, m, step)
            pairs.extend((i, i + r) for i in range(lo + r, lo + m - r, step))
        else:
            pairs.append((lo, lo + r))

    def sort(lo, m):
        if m > 1:
            sort(lo, m // 2)
            sort(lo + m // 2, m // 2)
            merge(lo, m, 1)

    sort(0, n)
    return pairs


def _top16(s):
    t = s.shape[1]
    nv = PEER_NKEYS // 8
    sub8 = lax.broadcasted_iota(jnp.int32, (8, t), 0).astype(F32)
    v = [s[8 * k:8 * k + 8] for k in range(nv)]
    ix = [sub8 + float(8 * k) for k in range(nv)]
    for p, q in _sort_network(nv):
        swap = (v[q] > v[p]) | ((v[q] == v[p]) & (ix[q] < ix[p]))
        v[p], v[q] = jnp.where(swap, v[q], v[p]), jnp.where(swap, v[p], v[q])
        ix[p], ix[q] = jnp.where(swap, ix[q], ix[p]), jnp.where(swap, ix[p], ix[q])
    vals, idxs = [], []
    for n in range(PEER_TOPK):
        m = jnp.max(v[0], axis=0, keepdims=True)
        imin = jnp.min(jnp.where(v[0] == m, ix[0], float(PEER_NKEYS)), axis=0, keepdims=True)
        hit = ix[0] == imin
        vals.append(m)
        idxs.append(imin)
        for k in range(PEER_TOPK - 1 - n):
            v[k] = jnp.where(hit, v[k + 1], v[k])
            ix[k] = jnp.where(hit, ix[k + 1], ix[k])
    return jnp.concatenate(vals, axis=0), jnp.concatenate(idxs, axis=0)


def _pair_top16(a, ia, b, ib):
    t = a.shape[1]
    sub8 = lax.broadcasted_iota(jnp.int32, (8, t), 0).astype(F32)
    ia_lo = ia[0:8] * float(PEER_NKEYS)
    lv, le = [], []
    for y in range(PEER_TOPK):
        nx = PEER_TOPK // (y + 1)
        val = a[0:8] + b[y:y + 1]
        lv.append(val if nx >= 8 else jnp.where(sub8 < float(nx), val, NEG_INF))
        le.append(ia_lo + ib[y:y + 1])
    hv = a[8:16] + b[0:1]
    he = ia[8:16] * float(PEER_NKEYS) + ib[0:1]
    code_lo = sub8 * float(PEER_TOPK)
    code_hi = (sub8 + 8.0) * float(PEER_TOPK)
    vals, sel = [], []
    for n in range(PEER_TOPK):
        m = jnp.max(jnp.maximum(lv[0], hv), axis=0, keepdims=True)
        cmin = jnp.min(jnp.minimum(jnp.where(lv[0] == m, code_lo, 1e9), jnp.where(hv == m, code_hi, 1e9)),
                       axis=0, keepdims=True)
        hit_lo = code_lo == cmin
        hit_hi = code_hi == cmin
        sel.append(jnp.max(jnp.maximum(jnp.where(hit_lo, le[0], -1.0), jnp.where(hit_hi, he, -1.0)),
                           axis=0, keepdims=True))
        vals.append(m)
        for y in range(PEER_TOPK - 1 - n):
            lv[y] = jnp.where(hit_lo, lv[y + 1], lv[y])
            le[y] = jnp.where(hit_lo, le[y + 1], le[y])
        code_lo = jnp.where(hit_lo, code_lo + 1.0, code_lo)
        hv = jnp.where(hit_hi, NEG_INF, hv)
    return jnp.concatenate(vals, axis=0), jnp.concatenate(sel, axis=0)


def _peer_route_gates_kernel(xa_ref, xb_ref, fnorm_ref, wqt_ref, keys_ref, h_ref, w_ref,
                             qt_s, e_s, p_s, i_s, j_s, g_s, t_s, *, tm, nta, nt):
    t = pl.program_id(0)
    cur = lax.rem(t, 2)
    prev = 1 - cur
    group = tm // PEER_HEADS

    @pl.when(t == 0)
    def _():
        i_s[...] = jnp.zeros_like(i_s)
        j_s[...] = jnp.zeros_like(j_s)
        g_s[...] = jnp.zeros_like(g_s)

    x = jnp.where(jnp.minimum(t, nt - 1) < nta, xa_ref[...], xb_ref[...])
    hb = _rms(x, fnorm_ref[...]).astype(BF16)
    h_ref[...] = hb
    qt_s[...] = _nt(wqt_ref[...], hb)
    sub = lax.broadcasted_iota(jnp.int32, (PEER_NKEYS, LANES), 0).astype(BF16)
    sub_odd = (lax.broadcasted_iota(jnp.int32, (W_ROWS_ODD, LANES), 0) - W_SHIFT).astype(BF16)
    one = jnp.ones((W_ROWS_ODD, LANES), BF16)
    zero = jnp.zeros((W_ROWS_ODD, LANES), BF16)

    def emit(n0, q):
        for i in range(PEER_NKEYS):
            a = t_s[pl.ds((16 * q) * W_PITCH + i, 8, stride=W_PITCH), :]
            b = t_s[pl.ds((16 * q + 8) * W_PITCH + i, 8, stride=W_PITCH), :]
            w_ref[pl.ds(n0 + 16 * q, 16), i * PEER_NKEYS:(i + 1) * PEER_NKEYS] = (
                jnp.concatenate([a, b], axis=0).astype(BF16))

    def head(hh, carry):
        r0 = pl.multiple_of(hh * 2 * PEER_NKEYS, 2 * PEER_NKEYS)
        s0 = _nn(keys_ref[2 * hh], qt_s[pl.ds(r0, PEER_NKEYS), :].astype(BF16))
        s1 = _nn(keys_ref[2 * hh + 1], qt_s[pl.ds(r0 + PEER_NKEYS, PEER_NKEYS), :].astype(BF16))
        n0 = pl.multiple_of(hh * group, group)
        for tok in range(group):
            ri = i_s[prev, pl.ds(n0 + tok, 1), :].astype(BF16)
            rj = j_s[prev, pl.ds(n0 + tok, 1), :].astype(BF16)
            rg = jnp.broadcast_to(g_s[prev, pl.ds(n0 + tok, 1), :].astype(BF16), sub.shape)
            bt = jnp.where(sub == rj, rg, zero[:PEER_NKEYS])
            if tok % 2 == 0:
                at = jnp.where(sub == ri, one[:PEER_NKEYS], zero[:PEER_NKEYS])
                t_s[tok * W_PITCH:tok * W_PITCH + PEER_NKEYS, :] = _nt(at, bt)
            else:
                at = jnp.where(sub_odd == ri, one, zero)
                t_s[tok * W_PITCH - W_SHIFT:tok * W_PITCH - W_SHIFT + W_ROWS_ODD, :] = _nt(at, bt)
        a, ia = _top16(s0)
        for q in range(group // 32):
            emit(n0, 2 * q)
        b, ib = _top16(s1)
        for q in range(group // 32):
            emit(n0, 2 * q + 1)
        c, e = _pair_top16(a, ia, b, ib)
        p = jnp.exp(c - c[0:1])
        rows = pl.ds(pl.multiple_of(hh * PEER_TOPK, PEER_TOPK), PEER_TOPK)
        e_s[rows, :] = e
        p_s[rows, :] = p / jnp.sum(p, axis=0, keepdims=True)
        return carry

    lax.fori_loop(0, PEER_HEADS, head, 0)
    et = jnp.transpose(e_s[...])
    it = jnp.floor(et * (1.0 / PEER_NKEYS))
    i_s[cur] = it
    j_s[cur] = et - it * float(PEER_NKEYS)
    g_s[cur] = jnp.transpose(p_s[...])


def _peer_route_gates(xa, xb, fnorm, wq, keys, *, tm=ROUTE_TM):
    assert xa.shape[0] % tm == 0 and xb.shape[0] % tm == 0
    nta, ntb = xa.shape[0] // tm, xb.shape[0] // tm
    nt = nta + ntb
    t = nt * tm
    nsel = PEER_HEADS * PEER_TOPK
    n_exp = PEER_NKEYS * PEER_NKEYS
    group = tm // PEER_HEADS
    assert group % 32 == 0
    cur_tile = lambda i: (jnp.minimum(i, nt - 1), 0)
    return pl.pallas_call(
        functools.partial(_peer_route_gates_kernel, tm=tm, nta=nta, nt=nt),
        grid=(nt + 1,),
        in_specs=[pl.BlockSpec((tm, D_MODEL), lambda i: (jnp.minimum(i, nta - 1), 0)),
                  pl.BlockSpec((tm, D_MODEL), lambda i: (jnp.clip(i - nta, 0, ntb - 1), 0)),
                  _const_spec(fnorm.shape), _const_spec(wq.shape), _const_spec(keys.shape)],
        out_specs=[pl.BlockSpec((tm, D_MODEL), cur_tile),
                   pl.BlockSpec((tm, n_exp), lambda i: (jnp.maximum(i - 1, 0), 0))],
        out_shape=[jax.ShapeDtypeStruct((t, D_MODEL), BF16), jax.ShapeDtypeStruct((t, n_exp), BF16)],
        scratch_shapes=[pltpu.VMEM((wq.shape[0], tm), F32), pltpu.VMEM((nsel, tm), F32),
                        pltpu.VMEM((nsel, tm), F32),
                        pltpu.VMEM((2, tm, nsel), F32), pltpu.VMEM((2, tm, nsel), F32),
                        pltpu.VMEM((2, tm, nsel), F32),
                        pltpu.VMEM(((group - 1) * W_PITCH - W_SHIFT + W_ROWS_ODD, LANES), F32)],
        compiler_params=pltpu.CompilerParams(dimension_semantics=("arbitrary",),
                                             vmem_limit_bytes=PEER_VMEM),
        name="peer_route_gates",
    )(xa, xb, fnorm, wq, keys)


def _peer_dense_kernel(h_ref, x_ref, w_ref, u_ref, v_ref, y_ref):
    @pl.when(pl.program_id(1) == 0)
    def _():
        y_ref[...] = x_ref[...]

    act = _nt(h_ref[...], u_ref[...].astype(BF16))
    gelu = 0.5 * act * (1.0 + lax.erf(act * (2.0 ** -0.5)))
    z = w_ref[...] * gelu.astype(BF16)
    y_ref[...] += _nn(z, v_ref[...].astype(BF16))


def _peer_dense(h, x, w, u, v, *, row0=0, tm=DENSE_TM, te=DENSE_TE):
    t = x.shape[0]
    tm = min(tm, t)
    assert row0 % tm == 0
    n_exp = u.shape[0]
    tok = pl.BlockSpec((tm, D_MODEL), lambda i, e: (i, 0))
    exp = pl.BlockSpec((te, D_MODEL), lambda i, e: (e, 0))
    return pl.pallas_call(
        _peer_dense_kernel,
        grid=(t // tm, n_exp // te),
        in_specs=[pl.BlockSpec((tm, D_MODEL), lambda i, e: (i + row0 // tm, 0)), tok,
                  pl.BlockSpec((tm, te), lambda i, e: (i + row0 // tm, e)), exp, exp],
        out_specs=tok,
        out_shape=jax.ShapeDtypeStruct((t, D_MODEL), F32),
        compiler_params=pltpu.CompilerParams(dimension_semantics=("arbitrary", "arbitrary"),
                                             vmem_limit_bytes=PEER_VMEM),
        name="peer_dense",
    )(h, x, w, u, v)


def _rope_tables(pos):
    half = ROPE_DIM // 2
    inv_freq = ROPE_THETA ** (-jnp.arange(half, dtype=F32) / half)
    ang = pos.astype(F32)[:, None] * inv_freq[None, :]
    cos, sin = jnp.cos(ang), jnp.sin(ang)
    d = jnp.arange(LANES) % HEAD_DIM
    freq = jnp.arange(half)[:, None]
    rot = (d < ROPE_DIM)[None, :]
    sel_c = jnp.where(rot & (d[None, :] % half == freq), 1.0, 0.0).astype(F32)
    sel_s = sel_c * jnp.where(d < half, -1.0, 1.0)[None, :]
    hi = lax.Precision.HIGHEST
    ct = jnp.dot(cos, sel_c, precision=hi) + jnp.where(d < ROPE_DIM, 0.0, 1.0)[None, :]
    st = jnp.dot(sin, sel_s, precision=hi)
    return ct, st


def kernel(x_prompt, x_sample, cache_swa_k, cache_swa_v, state_gla, attn_norm, w_in, q_norm, k_norm, attn_sinks,
           w_gate, b_gate, gla_norm, w_out, ffn_norm, peer_wq, peer_keys, peer_u, peer_v):
    depth = w_in.shape[0]
    assert depth == 1 and x_prompt.shape[0] == 1
    seq = x_prompt.shape[1]
    nseq, dec = x_sample.shape[0], x_sample.shape[1]
    xp = x_prompt[0]
    xs = x_sample.reshape(nseq * dec, D_MODEL)
    l = 0

    win = jnp.pad(w_in[l], ((0, 0), (0, C_END - w_in.shape[2]))).astype(BF16)
    qkg = jnp.concatenate([jnp.tile(q_norm[l], SWA_Q_HEADS), jnp.tile(k_norm[l], 2)])[None, :]
    sinks = jnp.broadcast_to(attn_sinks[l][:, None], (SWA_Q_HEADS, LANES))
    wgate = jnp.pad(w_gate[l], ((0, LANES - GLA_LOWRANK), (0, 0))).astype(BF16)
    wts = (attn_norm[l][None, :], win, qkg, sinks, wgate, b_gate[l][None, :],
           jnp.tile(gla_norm[l], GLA_HEADS)[None, :], w_out[l].astype(BF16))
    fnorm = ffn_norm[l][None, :]
    wq = jnp.transpose(peer_wq[l]).astype(BF16)
    keys = peer_keys[l].reshape(PEER_HEADS * 2, PEER_NKEYS, peer_keys.shape[-1]).astype(BF16)

    ct_p, st_p = _rope_tables(jnp.arange(seq, dtype=jnp.int32))
    ct_s, st_s = _rope_tables(PAST_LEN + jnp.arange(nseq * dec, dtype=jnp.int32) % dec)

    xp2, kp, vp, sp = _mixer_prompt(xp, ct_p, st_p, wts)
    ck = cache_swa_k[l].reshape(nseq, WINDOW, LANES)
    cv = cache_swa_v[l].reshape(nseq, WINDOW, LANES)
    xs2, ck2, cv2, ss = _mixer_sample(xs, ct_s, st_s, wts, ck, cv, state_gla[l], dec=dec)

    h_all, w_all = _peer_route_gates(xp2, xs2, fnorm, wq, keys)
    yp = _peer_dense(h_all, xp2, w_all, peer_u[l], peer_v[l])
    ys = _peer_dense(h_all, xs2, w_all, peer_u[l], peer_v[l], row0=seq, te=2 * DENSE_TE)

    kv_shape = (1, 1, WINDOW, 2, HEAD_DIM)
    return (yp[None], ys.reshape(nseq, dec, D_MODEL),
            kp.reshape(kv_shape), vp.reshape(kv_shape), sp[None, None],
            ck2.reshape(1, nseq, WINDOW, 2, HEAD_DIM), cv2.reshape(1, nseq, WINDOW, 2, HEAD_DIM), ss[None])
```

```python
import functools

import jax
import jax.numpy as jnp
from jax import lax
from jax.experimental import pallas as pl
from jax.experimental.pallas import tpu as pltpu

F32 = jnp.float32
BF16 = jnp.bfloat16

D_MODEL = 1024
HEAD_DIM = 64
SWA_Q_HEADS = 8
WINDOW = 128
ROPE_THETA = 500000.0
ROPE_DIM = 16
PAST_LEN = 16384
GLA_HEADS = 4
GLA_DK = 64
GLA_DV = 128
GLA_LOWRANK = 16
GLA_GATE_NORM = 16.0
PEER_HEADS = 8
PEER_NKEYS = 128
PEER_TOPK = 16
NORM_EPS = 1e-6

LANES = 128
GK = GLA_HEADS * GLA_DK
GV = GLA_HEADS * GLA_DV
C_Q, C_K, C_V, C_GQ, C_GK, C_GV, C_GG, C_GA, C_END = 0, 512, 640, 768, 1024, 1280, 1792, 2304, 2432
GLA_PAD = 128
W_PITCH = 132
W_SHIFT = W_PITCH % 8
W_ROWS_ODD = 144
NEG_INF = float("-inf")

MIB = 1024 * 1024
MIXER_TM = 256
GLA_CHUNK = 128
GLA_SUB = 32
SAMPLE_SEQS = 16
SAMPLE_PAR = 8
ROUTE_TM = 256
DENSE_TM = 1024
DENSE_TE = 1024
MIXER_VMEM = 48 * MIB
PEER_VMEM = 56 * MIB


def _nn(a, b, precision=None):
    return jnp.dot(a, b, preferred_element_type=F32, precision=precision)


def _nt(a, b):
    return lax.dot_general(a, b, (((1,), (1,)), ((), ())), preferred_element_type=F32)


def _rms(x, gain):
    ms = jnp.mean(x * x, axis=-1, keepdims=True)
    return x * lax.rsqrt(ms + NORM_EPS) * gain


def _head_norm_rope(x, gain, ctab, stab):
    lane = lax.broadcasted_iota(jnp.int32, x.shape, 1)
    lo = lane < HEAD_DIM
    sq = x * x
    ms_lo = jnp.sum(jnp.where(lo, sq, 0.0), axis=-1, keepdims=True)
    ms_hi = jnp.sum(jnp.where(lo, 0.0, sq), axis=-1, keepdims=True)
    ms = jnp.where(lo, ms_lo, ms_hi) * (1.0 / HEAD_DIM)
    xn = x * lax.rsqrt(ms + NORM_EPS) * gain
    first = (lane & (HEAD_DIM - 1)) < (ROPE_DIM // 2)
    partner = jnp.where(first, pltpu.roll(xn, LANES - ROPE_DIM // 2, 1), pltpu.roll(xn, ROPE_DIM // 2, 1))
    return xn * ctab + partner * stab


def _project(x, ct, st, anorm, win, qkg, wgate, bgate):
    h = _rms(x, anorm).astype(BF16)
    p = _nn(h, win)
    qk = [
        _head_norm_rope(p[:, c * LANES:(c + 1) * LANES], qkg[:, c * LANES:(c + 1) * LANES], ct, st)
        for c in range(C_V // LANES)
    ]
    q = jnp.concatenate(qk[:4], axis=1) * (HEAD_DIM ** -0.5)
    k = qk[4]
    v = p[:, C_V:C_GQ]
    gq = p[:, C_GQ:C_GK] * (GLA_DK ** -0.5)
    gk = p[:, C_GK:C_GV]
    gv = p[:, C_GV:C_GG]
    gg = p[:, C_GG:C_GA]
    z = _nn(p[:, C_GA:C_END].astype(BF16), wgate) + bgate
    logf = (jnp.minimum(z, 0.0) - jnp.log(1.0 + jnp.exp(-jnp.abs(z)))) * (1.0 / GLA_GATE_NORM)
    return q, k, v, gq, gk, gv, gg, logf


def _swa_multi(qs, k2s, v2s, masks, sink_ref):
    n = len(qs)
    m = qs[0].shape[0]
    nk = k2s[0].shape[0]
    lo = lax.broadcasted_iota(jnp.int32, (nk, LANES), 1) < HEAD_DIM
    olane = lax.broadcasted_iota(jnp.int32, (2 * m, LANES), 1) < HEAD_DIM
    top = lax.broadcasted_iota(jnp.int32, (2 * m, 1), 0) < m
    zero = jnp.zeros((nk, LANES), F32)
    kexp, vexp = {}, {}
    for p in range(n):
        krot = pltpu.roll(k2s[p], HEAD_DIM, 1)
        vrot = pltpu.roll(v2s[p], HEAD_DIM, 1)
        for g in range(2):
            ka, kb = (k2s[p], krot) if g == 0 else (krot, k2s[p])
            va, vb = (v2s[p], vrot) if g == 0 else (vrot, v2s[p])
            kexp[p, g] = jnp.concatenate([jnp.where(lo, ka, zero), jnp.where(lo, zero, kb)], axis=0).astype(BF16)
            vexp[p, g] = jnp.concatenate([jnp.where(lo, va, zero), jnp.where(lo, zero, vb)], axis=0).astype(BF16)
    s = {}
    for p in range(n):
        for g in range(2):
            qq = jnp.concatenate([qs[p][:, (2 * g) * LANES:(2 * g + 1) * LANES],
                                  qs[p][:, (2 * g + 1) * LANES:(2 * g + 2) * LANES]], axis=0).astype(BF16)
            s[p, g] = _nt(qq, kexp[p, g])
    probs, scale = {}, {}
    for p in range(n):
        mask2 = jnp.concatenate([masks[p], masks[p]], axis=0)
        for g in range(2):
            ps, rs = [], []
            for hh in range(2):
                sh = jnp.where(mask2, s[p, g][:, hh * nk:(hh + 1) * nk], NEG_INF)
                ha, hb = 4 * g + hh, 4 * g + 2 + hh
                sink = jnp.where(top, sink_ref[ha:ha + 1, 0:1], sink_ref[hb:hb + 1, 0:1])
                mx = jnp.maximum(jnp.max(sh, axis=-1, keepdims=True), sink)
                pe = jnp.exp(sh - mx)
                den = jnp.sum(pe, axis=-1, keepdims=True) + jnp.exp(sink - mx)
                ps.append(pe)
                rs.append(1.0 / den)
            probs[p, g] = jnp.concatenate(ps, axis=1).astype(BF16)
            scale[p, g] = jnp.where(olane, rs[0], rs[1])
    outs = []
    for p in range(n):
        o = [_nn(probs[p, g], vexp[p, g]) * scale[p, g] for g in range(2)]
        outs.append(jnp.concatenate([o[0][:m], o[0][m:], o[1][:m], o[1][m:]], axis=1))
    return outs


def _zpad(a, rows):
    if a.shape[0] == rows:
        return a
    return jnp.concatenate([a, jnp.zeros((rows - a.shape[0], a.shape[1]), a.dtype)], axis=0)


def _gla_intra_multi(qs, ks, vs, fs, sub):
    n = len(qs)
    c = qs[0].shape[0]
    nsub = c // sub
    shift = sub.bit_length() - 1
    r = lax.broadcasted_iota(jnp.int32, (c, GLA_PAD), 0)
    cc = lax.broadcasted_iota(jnp.int32, (c, GLA_PAD), 1)
    causal = cc <= r
    tri2 = jnp.concatenate([causal, causal & (cc >= ((r >> shift) << shift))], axis=0).astype(BF16)
    lane_k = lax.broadcasted_iota(jnp.int32, (c, GK), 1) >> 6
    lane_v = lax.broadcasted_iota(jnp.int32, (c, GV), 1) >> 7
    rows = lax.broadcasted_iota(jnp.int32, (c, GK), 0)
    acol = lax.broadcasted_iota(jnp.int32, (c, GLA_HEADS * GLA_PAD), 1) & (GLA_PAD - 1)
    arow = lax.broadcasted_iota(jnp.int32, (c, GLA_HEADS * GLA_PAD), 0)
    amask = (acol <= arow) & (acol >= ((arow >> shift) << shift))

    def expand_k(kt):
        return jnp.concatenate(
            [_zpad(jnp.where(lane_k == h, kt, 0.0), GLA_PAD) for h in range(GLA_HEADS)], axis=0).astype(BF16)

    b, bl, bcol = [], [], []
    for p in range(n):
        fp = _zpad(fs[p], GLA_PAD)
        f_hi = fp.astype(BF16)
        r1 = fp - f_hi.astype(F32)
        f_mid = r1.astype(BF16)
        f_lo = (r1 - f_mid.astype(F32)).astype(BF16)
        cs = _nn(tri2, jnp.concatenate([f_hi, f_mid, f_lo], axis=1))
        tot = (cs[:, :GK] + cs[:, GK:2 * GK]) + cs[:, 2 * GK:]
        b.append(tot[:c])
        bl.append(tot[c:])
        bcol.append(jnp.sum(jnp.transpose(fp), axis=1, keepdims=True))
    qd, kd, qo, ko, qg, kbt, vexp, vpad = [], [], [], [], [], [], [], []
    for p in range(n):
        q, k, v = qs[p], ks[p], vs[p]
        qd.append((q * jnp.exp(bl[p])).astype(BF16))
        kd.append(expand_k(k * jnp.exp(-bl[p])))
        if nsub > 1:
            qj, kj = [], []
            for j in range(nsub - 1):
                e = (j + 1) * sub
                bj = b[p][e - 1:e]
                qj.append(jnp.where(rows >= e, q * jnp.exp(jnp.minimum(b[p] - bj, 0.0)), 0.0).astype(BF16))
                kj.append(expand_k(jnp.where((rows >= e - sub) & (rows < e),
                                             k * jnp.exp(jnp.minimum(bj - b[p], 0.0)), 0.0)))
            qo.append(jnp.concatenate(qj, axis=1))
            ko.append(jnp.concatenate(kj, axis=1))
        qg.append((q * jnp.exp(b[p])).astype(BF16))
        kbt.append(jnp.transpose(_zpad(k * jnp.exp(b[p][c - 1:c] - b[p]), GLA_PAD)).astype(BF16))
        vexp.append(jnp.concatenate(
            [_zpad(jnp.where(lane_v == h, v, 0.0), GLA_PAD) for h in range(GLA_HEADS)], axis=0).astype(BF16))
        vpad.append(_zpad(v, GLA_PAD).astype(BF16))
    a = []
    for p in range(n):
        ap = jnp.where(amask, _nt(qd[p], kd[p]), 0.0)
        if nsub > 1:
            ap = ap + _nt(qo[p], ko[p])
        a.append(ap.astype(BF16))
    return [(_nn(a[p], vexp[p]), qg[p], kbt[p], vpad[p], bcol[p]) for p in range(n)]


def _gla_inter_multi(parts, states):
    srow = lax.broadcasted_iota(jnp.int32, (GK, GV), 0) >> 6
    scol = lax.broadcasted_iota(jnp.int32, (GK, GV), 1) >> 7
    diag = srow == scol
    outs = [o + _nn(qg, s.astype(BF16)) for (o, qg, _, _, _), s in zip(parts, states)]
    new = [s * jnp.exp(bcol) + jnp.where(diag, _nn(kbt, vp), 0.0) for (_, _, kbt, vp, bcol), s in zip(parts, states)]
    return outs, new


def _gla_out(o, gg, gnorm):
    outs = []
    for h in range(GLA_HEADS):
        oh = o[:, h * GLA_DV:(h + 1) * GLA_DV]
        ms = jnp.mean(oh * oh, axis=-1, keepdims=True)
        outs.append(oh * lax.rsqrt(ms + NORM_EPS))
    on = jnp.concatenate(outs, axis=1) * gnorm
    return on * (gg / (1.0 + jnp.exp(-gg)))


def _mixer_prompt_kernel(x_ref, ct_ref, st_ref, anorm_ref, win_ref, qkg_ref, sink_ref, wgate_ref, bgate_ref,
                         gnorm_ref, wout_ref, y_ref, klast_ref, vlast_ref, sfin_ref,
                         kprev_ref, vprev_ref, s_ref, *, tm, chunk, sub):
    i = pl.program_id(0)

    @pl.when(i == 0)
    def _():
        kprev_ref[...] = jnp.zeros_like(kprev_ref)
        vprev_ref[...] = jnp.zeros_like(vprev_ref)
        s_ref[...] = jnp.zeros_like(s_ref)

    x = x_ref[...]
    q, k, v, gq, gk, gv, gg, logf = _project(
        x, ct_ref[...], st_ref[...], anorm_ref[...], win_ref[...], qkg_ref[...], wgate_ref[...], bgate_ref[...])

    kall = jnp.concatenate([kprev_ref[...], k], axis=0)
    vall = jnp.concatenate([vprev_ref[...], v], axis=0)
    qi = lax.broadcasted_iota(jnp.int32, (WINDOW, 2 * WINDOW), 0)
    kj = lax.broadcasted_iota(jnp.int32, (WINDOW, 2 * WINDOW), 1)
    band = (kj >= qi) & (kj <= qi + WINDOW)
    nb = tm // WINDOW
    masks = [band & (kj >= jnp.where(i == 0, WINDOW, 0)) if j == 0 else band for j in range(nb)]
    blocks = _swa_multi([q[j * WINDOW:(j + 1) * WINDOW] for j in range(nb)],
                        [kall[j * WINDOW:(j + 2) * WINDOW] for j in range(nb)],
                        [vall[j * WINDOW:(j + 2) * WINDOW] for j in range(nb)], masks, sink_ref)
    a_out = jnp.concatenate(blocks, axis=0)
    kprev_ref[...] = k[tm - WINDOW:]
    vprev_ref[...] = v[tm - WINDOW:]
    klast_ref[...] = k[tm - WINDOW:]
    vlast_ref[...] = v[tm - WINDOW:]

    sls = [slice(c * chunk, (c + 1) * chunk) for c in range(tm // chunk)]
    parts = _gla_intra_multi([gq[sl] for sl in sls], [gk[sl] for sl in sls], [gv[sl] for sl in sls],
                             [logf[sl] for sl in sls], sub)
    s = s_ref[...]
    os_ = []
    for part in parts:
        (o,), (s,) = _gla_inter_multi([part], [s])
        os_.append(o)
    s_ref[...] = s
    for h in range(GLA_HEADS):
        sfin_ref[h] = s[h * GLA_DK:(h + 1) * GLA_DK, h * GLA_DV:(h + 1) * GLA_DV]
    g_out = _gla_out(jnp.concatenate(os_, axis=0), gg, gnorm_ref[...])

    mix = jnp.concatenate([a_out, g_out], axis=1).astype(BF16)
    y_ref[...] = x + _nn(mix, wout_ref[...])


def _const_spec(shape):
    return pl.BlockSpec(shape, lambda *_: (0,) * len(shape))


def _mixer_prompt(x, ct, st, wts, *, tm=MIXER_TM, chunk=GLA_CHUNK, sub=GLA_SUB):
    t = x.shape[0]
    tm = min(tm, t)
    row = lambda w: pl.BlockSpec((tm, w), lambda i: (i, 0))
    kern = functools.partial(_mixer_prompt_kernel, tm=tm, chunk=chunk, sub=sub)
    return pl.pallas_call(
        kern,
        grid=(t // tm,),
        in_specs=[row(D_MODEL), row(LANES), row(LANES)] + [_const_spec(w.shape) for w in wts],
        out_specs=[row(D_MODEL), _const_spec((WINDOW, LANES)), _const_spec((WINDOW, LANES)),
                   _const_spec((GLA_HEADS, GLA_DK, GLA_DV))],
        out_shape=[jax.ShapeDtypeStruct((t, D_MODEL), F32), jax.ShapeDtypeStruct((WINDOW, LANES), F32),
                   jax.ShapeDtypeStruct((WINDOW, LANES), F32),
                   jax.ShapeDtypeStruct((GLA_HEADS, GLA_DK, GLA_DV), F32)],
        scratch_shapes=[pltpu.VMEM((WINDOW, LANES), F32), pltpu.VMEM((WINDOW, LANES), F32),
                        pltpu.VMEM((GK, GV), F32)],
        compiler_params=pltpu.CompilerParams(dimension_semantics=("arbitrary",),
                                             vmem_limit_bytes=MIXER_VMEM),
        name="mixer_prompt",
    )(x, ct, st, *wts)


def _mixer_sample_kernel(x_ref, ct_ref, st_ref, anorm_ref, win_ref, qkg_ref, sink_ref, wgate_ref, bgate_ref,
                         gnorm_ref, wout_ref, ck_ref, cv_ref, sin_ref,
                         y_ref, kout_ref, vout_ref, sout_ref, *, sb, dec, par):
    x = x_ref[...]
    q, k, v, gq, gk, gv, gg, logf = _project(
        x, ct_ref[...], st_ref[...], anorm_ref[...], win_ref[...], qkg_ref[...], wgate_ref[...], bgate_ref[...])
    for b in range(sb):
        kout_ref[b] = jnp.concatenate([ck_ref[b, dec:, :], k[b * dec:(b + 1) * dec]], axis=0)
        vout_ref[b] = jnp.concatenate([cv_ref[b, dec:, :], v[b * dec:(b + 1) * dec]], axis=0)

    per = 8 // dec
    qi = lax.broadcasted_iota(jnp.int32, (8, 2 * WINDOW), 0) & (dec - 1)
    kj = lax.broadcasted_iota(jnp.int32, (8, 2 * WINDOW), 1)
    mask = (kj >= qi) & (kj <= qi + WINDOW)
    ztail = jnp.zeros((WINDOW - dec, LANES), F32)
    pad8 = lambda a: jnp.concatenate([a, jnp.zeros((8 - dec, a.shape[1]), F32)], axis=0)

    a_parts, o_parts = [], []
    for b0 in range(0, sb, par):
        seqs = range(b0, b0 + par)
        rows8 = [slice((b // per) * 8, (b // per) * 8 + 8) for b in seqs]
        own = [slice((b % per) * dec, (b % per) * dec + dec) for b in seqs]
        new = [slice(b * dec, (b + 1) * dec) for b in seqs]
        att = _swa_multi([q[r] for r in rows8],
                         [jnp.concatenate([ck_ref[b], k[nw], ztail], axis=0) for b, nw in zip(seqs, new)],
                         [jnp.concatenate([cv_ref[b], v[nw], ztail], axis=0) for b, nw in zip(seqs, new)],
                         [mask] * par, sink_ref)
        a_parts += [a[o] for a, o in zip(att, own)]
        states = []
        for b in seqs:
            st = sin_ref[b]
            srows = []
            for h in range(GLA_HEADS):
                parts = []
                if h > 0:
                    parts.append(jnp.zeros((GLA_DK, h * GLA_DV), F32))
                parts.append(st[h])
                if h < GLA_HEADS - 1:
                    parts.append(jnp.zeros((GLA_DK, (GLA_HEADS - 1 - h) * GLA_DV), F32))
                srows.append(jnp.concatenate(parts, axis=1))
            states.append(jnp.concatenate(srows, axis=0))
        intra = _gla_intra_multi([pad8(gq[nw]) for nw in new], [pad8(gk[nw]) for nw in new],
                                 [pad8(gv[nw]) for nw in new], [pad8(logf[nw]) for nw in new], 8)
        outs, states = _gla_inter_multi(intra, states)
        o_parts += [o[:dec] for o in outs]
        for b, s_new in zip(seqs, states):
            for h in range(GLA_HEADS):
                sout_ref[b, h] = s_new[h * GLA_DK:(h + 1) * GLA_DK, h * GLA_DV:(h + 1) * GLA_DV]

    g_out = _gla_out(jnp.concatenate(o_parts, axis=0), gg, gnorm_ref[...])
    mix = jnp.concatenate([jnp.concatenate(a_parts, axis=0), g_out], axis=1).astype(BF16)
    y_ref[...] = x + _nn(mix, wout_ref[...])


def _mixer_sample(x, ct, st, wts, ck, cv, s0, *, dec, sb=SAMPLE_SEQS, par=SAMPLE_PAR):
    n = x.shape[0]
    nseq = n // dec
    m = sb * dec
    row = lambda w: pl.BlockSpec((m, w), lambda i: (i, 0))
    seq3 = pl.BlockSpec((sb, WINDOW, LANES), lambda i: (i, 0, 0))
    seq4 = pl.BlockSpec((sb, GLA_HEADS, GLA_DK, GLA_DV), lambda i: (i, 0, 0, 0))
    kern = functools.partial(_mixer_sample_kernel, sb=sb, dec=dec, par=par)
    return pl.pallas_call(
        kern,
        grid=(nseq // sb,),
        in_specs=[row(D_MODEL), row(LANES), row(LANES)] + [_const_spec(w.shape) for w in wts] + [seq3, seq3, seq4],
        out_specs=[row(D_MODEL), seq3, seq3, seq4],
        out_shape=[jax.ShapeDtypeStruct((n, D_MODEL), F32), jax.ShapeDtypeStruct(ck.shape, F32),
                   jax.ShapeDtypeStruct(cv.shape, F32), jax.ShapeDtypeStruct(s0.shape, F32)],
        compiler_params=pltpu.CompilerParams(dimension_semantics=("arbitrary",),
                                             vmem_limit_bytes=MIXER_VMEM),
        name="mixer_sample",
    )(x, ct, st, *wts, ck, cv, s0)


SORT16 = (
    (0, 13), (1, 12), (2, 15), (3, 14), (4, 8), (5, 6), (7, 11), (9, 10),
    (0, 5), (1, 7), (2, 9), (3, 4), (6, 13), (8, 14), (10, 15), (11, 12),
    (0, 1), (2, 3), (4, 5), (6, 8), (7, 9), (10, 11), (12, 13), (14, 15),
    (0, 2), (1, 3), (4, 10), (5, 11), (6, 7), (8, 9), (12, 14), (13, 15),
    (1, 2), (3, 12), (4, 6), (5, 7), (8, 10), (9, 11), (13, 14),
    (1, 4), (2, 6), (5, 8), (7, 10), (9, 13), (11, 14),
    (2, 4), (3, 6), (9, 12), (11, 13),
    (3, 5), (6, 8), (7, 9), (10, 12),
    (3, 4), (5, 6), (7, 8), (9, 10), (11, 12),
    (6, 7), (8, 9),
)
assert PEER_NKEYS // 8 == 16


def _top16(s):
    t = s.shape[1]
    nv = PEER_NKEYS // 8
    sub8 = lax.broadcasted_iota(jnp.int32, (8, t), 0).astype(F32)
    v = [s[8 * k:8 * k + 8] for k in range(nv)]
    ix = [sub8 + float(8 * k) for k in range(nv)]
    for p, q in SORT16:
        swap = (v[q] > v[p]) | ((v[q] == v[p]) & (ix[q] < ix[p]))
        v[p], v[q] = jnp.where(swap, v[q], v[p]), jnp.where(swap, v[p], v[q])
        ix[p], ix[q] = jnp.where(swap, ix[q], ix[p]), jnp.where(swap, ix[p], ix[q])
    vals, idxs = [], []
    for n in range(PEER_TOPK):
        m = jnp.max(v[0], axis=0, keepdims=True)
        imin = jnp.min(jnp.where(v[0] == m, ix[0], float(PEER_NKEYS)), axis=0, keepdims=True)
        hit = ix[0] == imin
        vals.append(m)
        idxs.append(imin)
        for k in range(PEER_TOPK - 1 - n):
            v[k] = jnp.where(hit, v[k + 1], v[k])
            ix[k] = jnp.where(hit, ix[k + 1], ix[k])
    return jnp.concatenate(vals, axis=0), jnp.concatenate(idxs, axis=0)


def _pair_top16(a, ia, b, ib):
    t = a.shape[1]
    sub8 = lax.broadcasted_iota(jnp.int32, (8, t), 0).astype(F32)
    ia_lo = ia[0:8] * float(PEER_NKEYS)
    lv, le = [], []
    for y in range(PEER_TOPK):
        nx = PEER_TOPK // (y + 1)
        val = a[0:8] + b[y:y + 1]
        lv.append(val if nx >= 8 else jnp.where(sub8 < float(nx), val, NEG_INF))
        le.append(ia_lo + ib[y:y + 1])
    hv = a[8:16] + b[0:1]
    he = ia[8:16] * float(PEER_NKEYS) + ib[0:1]
    code_lo = sub8 * float(PEER_TOPK)
    code_hi = (sub8 + 8.0) * float(PEER_TOPK)
    vals, sel = [], []
    for n in range(PEER_TOPK):
        m = jnp.max(jnp.maximum(lv[0], hv), axis=0, keepdims=True)
        cmin = jnp.min(jnp.minimum(jnp.where(lv[0] == m, code_lo, 1e9), jnp.where(hv == m, code_hi, 1e9)),
                       axis=0, keepdims=True)
        hit_lo = code_lo == cmin
        hit_hi = code_hi == cmin
        sel.append(jnp.max(jnp.maximum(jnp.where(hit_lo, le[0], -1.0), jnp.where(hit_hi, he, -1.0)),
                           axis=0, keepdims=True))
        vals.append(m)
        for y in range(PEER_TOPK - 1 - n):
            lv[y] = jnp.where(hit_lo, lv[y + 1], lv[y])
            le[y] = jnp.where(hit_lo, le[y + 1], le[y])
        code_lo = jnp.where(hit_lo, code_lo + 1.0, code_lo)
        hv = jnp.where(hit_hi, NEG_INF, hv)
    return jnp.concatenate(vals, axis=0), jnp.concatenate(sel, axis=0)


def _peer_route_gates_kernel(xa_ref, xb_ref, fnorm_ref, wqt_ref, keys_ref, h_ref, w_ref,
                             qt_s, e_s, p_s, i_s, j_s, g_s, t_s, *, tm, nta, nt):
    t = pl.program_id(0)
    cur = lax.rem(t, 2)
    prev = 1 - cur
    group = tm // PEER_HEADS

    @pl.when(t == 0)
    def _():
        i_s[...] = jnp.zeros_like(i_s)
        j_s[...] = jnp.zeros_like(j_s)
        g_s[...] = jnp.zeros_like(g_s)

    x = jnp.where(jnp.minimum(t, nt - 1) < nta, xa_ref[...], xb_ref[...])
    hb = _rms(x, fnorm_ref[...]).astype(BF16)
    h_ref[...] = hb
    qt_s[...] = _nt(wqt_ref[...], hb)
    sub = lax.broadcasted_iota(jnp.int32, (PEER_NKEYS, LANES), 0).astype(BF16)
    sub_odd = (lax.broadcasted_iota(jnp.int32, (W_ROWS_ODD, LANES), 0) - W_SHIFT).astype(BF16)
    one = jnp.ones((W_ROWS_ODD, LANES), BF16)
    zero = jnp.zeros((W_ROWS_ODD, LANES), BF16)

    def emit(n0, q):
        for i in range(PEER_NKEYS):
            a = t_s[pl.ds((16 * q) * W_PITCH + i, 8, stride=W_PITCH), :]
            b = t_s[pl.ds((16 * q + 8) * W_PITCH + i, 8, stride=W_PITCH), :]
            w_ref[pl.ds(n0 + 16 * q, 16), i * PEER_NKEYS:(i + 1) * PEER_NKEYS] = (
                jnp.concatenate([a, b], axis=0).astype(BF16))

    def head(hh, carry):
        r0 = pl.multiple_of(hh * 2 * PEER_NKEYS, 2 * PEER_NKEYS)
        s0 = _nn(keys_ref[2 * hh], qt_s[pl.ds(r0, PEER_NKEYS), :].astype(BF16))
        s1 = _nn(keys_ref[2 * hh + 1], qt_s[pl.ds(r0 + PEER_NKEYS, PEER_NKEYS), :].astype(BF16))
        n0 = pl.multiple_of(hh * group, group)
        for tok in range(group):
            ri = i_s[prev, pl.ds(n0 + tok, 1), :].astype(BF16)
            rj = j_s[prev, pl.ds(n0 + tok, 1), :].astype(BF16)
            rg = jnp.broadcast_to(g_s[prev, pl.ds(n0 + tok, 1), :].astype(BF16), sub.shape)
            bt = jnp.where(sub == rj, rg, zero[:PEER_NKEYS])
            if tok % 2 == 0:
                at = jnp.where(sub == ri, one[:PEER_NKEYS], zero[:PEER_NKEYS])
                t_s[tok * W_PITCH:tok * W_PITCH + PEER_NKEYS, :] = _nt(at, bt)
            else:
                at = jnp.where(sub_odd == ri, one, zero)
                t_s[tok * W_PITCH - W_SHIFT:tok * W_PITCH - W_SHIFT + W_ROWS_ODD, :] = _nt(at, bt)
        a, ia = _top16(s0)
        for q in range(group // 32):
            emit(n0, 2 * q)
        b, ib = _top16(s1)
        for q in range(group // 32):
            emit(n0, 2 * q + 1)
        c, e = _pair_top16(a, ia, b, ib)
        p = jnp.exp(c - c[0:1])
        rows = pl.ds(pl.multiple_of(hh * PEER_TOPK, PEER_TOPK), PEER_TOPK)
        e_s[rows, :] = e
        p_s[rows, :] = p / jnp.sum(p, axis=0, keepdims=True)
        return carry

    lax.fori_loop(0, PEER_HEADS, head, 0)
    et = jnp.transpose(e_s[...])
    it = jnp.floor(et * (1.0 / PEER_NKEYS))
    i_s[cur] = it
    j_s[cur] = et - it * float(PEER_NKEYS)
    g_s[cur] = jnp.transpose(p_s[...])


def _peer_route_gates(xa, xb, fnorm, wq, keys, *, tm=ROUTE_TM):
    assert xa.shape[0] % tm == 0 and xb.shape[0] % tm == 0
    nta, ntb = xa.shape[0] // tm, xb.shape[0] // tm
    nt = nta + ntb
    t = nt * tm
    nsel = PEER_HEADS * PEER_TOPK
    n_exp = PEER_NKEYS * PEER_NKEYS
    group = tm // PEER_HEADS
    assert group % 32 == 0
    cur_tile = lambda i: (jnp.minimum(i, nt - 1), 0)
    return pl.pallas_call(
        functools.partial(_peer_route_gates_kernel, tm=tm, nta=nta, nt=nt),
        grid=(nt + 1,),
        in_specs=[pl.BlockSpec((tm, D_MODEL), lambda i: (jnp.minimum(i, nta - 1), 0)),
                  pl.BlockSpec((tm, D_MODEL), lambda i: (jnp.clip(i - nta, 0, ntb - 1), 0)),
                  _const_spec(fnorm.shape), _const_spec(wq.shape), _const_spec(keys.shape)],
        out_specs=[pl.BlockSpec((tm, D_MODEL), cur_tile),
                   pl.BlockSpec((tm, n_exp), lambda i: (jnp.maximum(i - 1, 0), 0))],
        out_shape=[jax.ShapeDtypeStruct((t, D_MODEL), BF16), jax.ShapeDtypeStruct((t, n_exp), BF16)],
        scratch_shapes=[pltpu.VMEM((wq.shape[0], tm), F32), pltpu.VMEM((nsel, tm), F32),
                        pltpu.VMEM((nsel, tm), F32),
                        pltpu.VMEM((2, tm, nsel), F32), pltpu.VMEM((2, tm, nsel), F32),
                        pltpu.VMEM((2, tm, nsel), F32),
                        pltpu.VMEM(((group - 1) * W_PITCH - W_SHIFT + W_ROWS_ODD, LANES), F32)],
        compiler_params=pltpu.CompilerParams(dimension_semantics=("arbitrary",),
                                             vmem_limit_bytes=PEER_VMEM),
        name="peer_route_gates",
    )(xa, xb, fnorm, wq, keys)


def _peer_dense_kernel(h_ref, x_ref, w_ref, u_ref, v_ref, y_ref):
    @pl.when(pl.program_id(1) == 0)
    def _():
        y_ref[...] = x_ref[...]

    act = _nt(h_ref[...], u_ref[...].astype(BF16))
    gelu = 0.5 * act * (1.0 + lax.erf(act * (2.0 ** -0.5)))
    z = w_ref[...] * gelu.astype(BF16)
    y_ref[...] += _nn(z, v_ref[...].astype(BF16))


def _peer_dense(h, x, w, u, v, *, row0=0, tm=DENSE_TM, te=DENSE_TE):
    t = x.shape[0]
    tm = min(tm, t)
    assert row0 % tm == 0
    n_exp = u.shape[0]
    tok = pl.BlockSpec((tm, D_MODEL), lambda i, e: (i, 0))
    exp = pl.BlockSpec((te, D_MODEL), lambda i, e: (e, 0))
    return pl.pallas_call(
        _peer_dense_kernel,
        grid=(t // tm, n_exp // te),
        in_specs=[pl.BlockSpec((tm, D_MODEL), lambda i, e: (i + row0 // tm, 0)), tok,
                  pl.BlockSpec((tm, te), lambda i, e: (i + row0 // tm, e)), exp, exp],
        out_specs=tok,
        out_shape=jax.ShapeDtypeStruct((t, D_MODEL), F32),
        compiler_params=pltpu.CompilerParams(dimension_semantics=("arbitrary", "arbitrary"),
                                             vmem_limit_bytes=PEER_VMEM),
        name="peer_dense",
    )(h, x, w, u, v)


def _rope_tables(pos):
    half = ROPE_DIM // 2
    inv_freq = ROPE_THETA ** (-jnp.arange(half, dtype=F32) / half)
    ang = pos.astype(F32)[:, None] * inv_freq[None, :]
    cos, sin = jnp.cos(ang), jnp.sin(ang)
    n = pos.shape[0]
    c64 = jnp.concatenate([cos, cos, jnp.ones((n, HEAD_DIM - ROPE_DIM), F32)], axis=1)
    s64 = jnp.concatenate([-sin, sin, jnp.zeros((n, HEAD_DIM - ROPE_DIM), F32)], axis=1)
    return jnp.tile(c64, (1, 2)), jnp.tile(s64, (1, 2))


def kernel(x_prompt, x_sample, cache_swa_k, cache_swa_v, state_gla, attn_norm, w_in, q_norm, k_norm, attn_sinks,
           w_gate, b_gate, gla_norm, w_out, ffn_norm, peer_wq, peer_keys, peer_u, peer_v):
    depth = w_in.shape[0]
    assert depth == 1 and x_prompt.shape[0] == 1
    seq = x_prompt.shape[1]
    nseq, dec = x_sample.shape[0], x_sample.shape[1]
    xp = x_prompt[0]
    xs = x_sample.reshape(nseq * dec, D_MODEL)
    l = 0

    win = jnp.pad(w_in[l], ((0, 0), (0, C_END - w_in.shape[2]))).astype(BF16)
    qkg = jnp.concatenate([jnp.tile(q_norm[l], SWA_Q_HEADS), jnp.tile(k_norm[l], 2)])[None, :]
    sinks = jnp.broadcast_to(attn_sinks[l][:, None], (SWA_Q_HEADS, LANES))
    wgate = jnp.pad(w_gate[l], ((0, LANES - GLA_LOWRANK), (0, 0))).astype(BF16)
    wts = (attn_norm[l][None, :], win, qkg, sinks, wgate, b_gate[l][None, :],
           jnp.tile(gla_norm[l], GLA_HEADS)[None, :], w_out[l].astype(BF16))
    fnorm = ffn_norm[l][None, :]
    wq = jnp.transpose(peer_wq[l]).astype(BF16)
    keys = peer_keys[l].reshape(PEER_HEADS * 2, PEER_NKEYS, peer_keys.shape[-1]).astype(BF16)

    ct_p, st_p = _rope_tables(jnp.arange(seq, dtype=jnp.int32))
    ct_s, st_s = _rope_tables(PAST_LEN + jnp.arange(nseq * dec, dtype=jnp.int32) % dec)

    xp2, kp, vp, sp = _mixer_prompt(xp, ct_p, st_p, wts)
    ck = cache_swa_k[l].reshape(nseq, WINDOW, LANES)
    cv = cache_swa_v[l].reshape(nseq, WINDOW, LANES)
    xs2, ck2, cv2, ss = _mixer_sample(xs, ct_s, st_s, wts, ck, cv, state_gla[l], dec=dec)

    h_all, w_all = _peer_route_gates(xp2, xs2, fnorm, wq, keys)
    yp = _peer_dense(h_all, xp2, w_all, peer_u[l], peer_v[l])
    ys = _peer_dense(h_all, xs2, w_all, peer_u[l], peer_v[l], row0=seq, te=2 * DENSE_TE)

    kv_shape = (1, 1, WINDOW, 2, HEAD_DIM)
    return (yp[None], ys.reshape(nseq, dec, D_MODEL),
            kp.reshape(kv_shape), vp.reshape(kv_shape), sp[None, None],
            ck2.reshape(1, nseq, WINDOW, 2, HEAD_DIM), cv2.reshape(1, nseq, WINDOW, 2, HEAD_DIM), ss[None])
```

```python
import functools

import jax
import jax.numpy as jnp
from jax import lax
from jax.experimental import pallas as pl
from jax.experimental.pallas import tpu as pltpu

F32 = jnp.float32
BF16 = jnp.bfloat16

D_MODEL = 1024
HEAD_DIM = 64
SWA_Q_HEADS = 8
WINDOW = 128
ROPE_THETA = 500000.0
ROPE_DIM = 16
PAST_LEN = 16384
GLA_HEADS = 4
GLA_DK = 64
GLA_DV = 128
GLA_LOWRANK = 16
GLA_GATE_NORM = 16.0
PEER_HEADS = 8
PEER_NKEYS = 128
PEER_TOPK = 16
NORM_EPS = 1e-6

LANES = 128
GK = GLA_HEADS * GLA_DK
GV = GLA_HEADS * GLA_DV
C_Q, C_K, C_V, C_GQ, C_GK, C_GV, C_GG, C_GA, C_END = 0, 512, 640, 768, 1024, 1280, 1792, 2304, 2432
GLA_PAD = 128
W_PITCH = 132
W_SHIFT = W_PITCH % 8
W_ROWS_ODD = 144
NEG_INF = float("-inf")

MIB = 1024 * 1024
MIXER_TM = 256
GLA_CHUNK = 128
GLA_SUB = 32
SAMPLE_SEQS = 16
SAMPLE_PAR = 8
ROUTE_TM = 256
DENSE_TM = 1024
DENSE_TE = 1024
MIXER_VMEM = 48 * MIB
PEER_VMEM = 56 * MIB
DENSE_VMEM = 58 * MIB


def _nn(a, b, precision=None):
    return jnp.dot(a, b, preferred_element_type=F32, precision=precision)


def _nt(a, b):
    return lax.dot_general(a, b, (((1,), (1,)), ((), ())), preferred_element_type=F32)


def _rms(x, gain):
    ms = jnp.mean(x * x, axis=-1, keepdims=True)
    return x * lax.rsqrt(ms + NORM_EPS) * gain


def _head_norm_rope(x, gain, ctab, stab):
    lane = lax.broadcasted_iota(jnp.int32, x.shape, 1)
    lo = lane < HEAD_DIM
    sq = x * x
    ms_lo = jnp.sum(jnp.where(lo, sq, 0.0), axis=-1, keepdims=True)
    ms_hi = jnp.sum(jnp.where(lo, 0.0, sq), axis=-1, keepdims=True)
    ms = jnp.where(lo, ms_lo, ms_hi) * (1.0 / HEAD_DIM)
    xn = x * lax.rsqrt(ms + NORM_EPS) * gain
    first = (lane & (HEAD_DIM - 1)) < (ROPE_DIM // 2)
    partner = jnp.where(first, pltpu.roll(xn, LANES - ROPE_DIM // 2, 1), pltpu.roll(xn, ROPE_DIM // 2, 1))
    return xn * ctab + partner * stab


def _project(x, ct, st, anorm, win, qkg, wgate, bgate):
    h = _rms(x, anorm).astype(BF16)
    p = _nn(h, win)
    qk = [
        _head_norm_rope(p[:, c * LANES:(c + 1) * LANES], qkg[:, c * LANES:(c + 1) * LANES], ct, st)
        for c in range(C_V // LANES)
    ]
    q = jnp.concatenate(qk[:4], axis=1) * (HEAD_DIM ** -0.5)
    k = qk[4]
    v = p[:, C_V:C_GQ]
    gq = p[:, C_GQ:C_GK] * (GLA_DK ** -0.5)
    gk = p[:, C_GK:C_GV]
    gv = p[:, C_GV:C_GG]
    gg = p[:, C_GG:C_GA]
    z = _nn(p[:, C_GA:C_END].astype(BF16), wgate) + bgate
    logf = (jnp.minimum(z, 0.0) - jnp.log(1.0 + jnp.exp(-jnp.abs(z)))) * (1.0 / GLA_GATE_NORM)
    return q, k, v, gq, gk, gv, gg, logf


def _swa_multi(qs, k2s, v2s, masks, sink_ref):
    n = len(qs)
    m = qs[0].shape[0]
    nk = k2s[0].shape[0]
    lo = lax.broadcasted_iota(jnp.int32, (nk, LANES), 1) < HEAD_DIM
    olane = lax.broadcasted_iota(jnp.int32, (2 * m, LANES), 1) < HEAD_DIM
    top = lax.broadcasted_iota(jnp.int32, (2 * m, 1), 0) < m
    zero = jnp.zeros((nk, LANES), F32)
    kexp, vexp = {}, {}
    for p in range(n):
        krot = pltpu.roll(k2s[p], HEAD_DIM, 1)
        vrot = pltpu.roll(v2s[p], HEAD_DIM, 1)
        for g in range(2):
            ka, kb = (k2s[p], krot) if g == 0 else (krot, k2s[p])
            va, vb = (v2s[p], vrot) if g == 0 else (vrot, v2s[p])
            kexp[p, g] = jnp.concatenate([jnp.where(lo, ka, zero), jnp.where(lo, zero, kb)], axis=0).astype(BF16)
            vexp[p, g] = jnp.concatenate([jnp.where(lo, va, zero), jnp.where(lo, zero, vb)], axis=0).astype(BF16)
    s = {}
    for p in range(n):
        for g in range(2):
            qq = jnp.concatenate([qs[p][:, (2 * g) * LANES:(2 * g + 1) * LANES],
                                  qs[p][:, (2 * g + 1) * LANES:(2 * g + 2) * LANES]], axis=0).astype(BF16)
            s[p, g] = _nt(qq, kexp[p, g])
    probs, scale = {}, {}
    for p in range(n):
        mask2 = jnp.concatenate([masks[p], masks[p]], axis=0)
        for g in range(2):
            ps, rs = [], []
            for hh in range(2):
                sh = jnp.where(mask2, s[p, g][:, hh * nk:(hh + 1) * nk], NEG_INF)
                ha, hb = 4 * g + hh, 4 * g + 2 + hh
                sink = jnp.where(top, sink_ref[ha:ha + 1, 0:1], sink_ref[hb:hb + 1, 0:1])
                mx = jnp.maximum(jnp.max(sh, axis=-1, keepdims=True), sink)
                pe = jnp.exp(sh - mx)
                den = jnp.sum(pe, axis=-1, keepdims=True) + jnp.exp(sink - mx)
                ps.append(pe)
                rs.append(1.0 / den)
            probs[p, g] = jnp.concatenate(ps, axis=1).astype(BF16)
            scale[p, g] = jnp.where(olane, rs[0], rs[1])
    outs = []
    for p in range(n):
        o = [_nn(probs[p, g], vexp[p, g]) * scale[p, g] for g in range(2)]
        outs.append(jnp.concatenate([o[0][:m], o[0][m:], o[1][:m], o[1][m:]], axis=1))
    return outs


def _zpad(a, rows):
    if a.shape[0] == rows:
        return a
    return jnp.concatenate([a, jnp.zeros((rows - a.shape[0], a.shape[1]), a.dtype)], axis=0)


def _gla_intra_multi(qs, ks, vs, fs, sub):
    n = len(qs)
    c = qs[0].shape[0]
    nsub = c // sub
    shift = sub.bit_length() - 1
    r = lax.broadcasted_iota(jnp.int32, (c, GLA_PAD), 0)
    cc = lax.broadcasted_iota(jnp.int32, (c, GLA_PAD), 1)
    causal = cc <= r
    tri2 = jnp.concatenate([causal, causal & (cc >= ((r >> shift) << shift))], axis=0).astype(BF16)
    lane_k = lax.broadcasted_iota(jnp.int32, (c, GK), 1) >> 6
    lane_v = lax.broadcasted_iota(jnp.int32, (c, GV), 1) >> 7
    rows = lax.broadcasted_iota(jnp.int32, (c, GK), 0)
    acol = lax.broadcasted_iota(jnp.int32, (c, GLA_HEADS * GLA_PAD), 1) & (GLA_PAD - 1)
    arow = lax.broadcasted_iota(jnp.int32, (c, GLA_HEADS * GLA_PAD), 0)
    amask = (acol <= arow) & (acol >= ((arow >> shift) << shift))

    def expand_k(kt):
        return jnp.concatenate(
            [_zpad(jnp.where(lane_k == h, kt, 0.0), GLA_PAD) for h in range(GLA_HEADS)], axis=0).astype(BF16)

    b, bl, bcol = [], [], []
    for p in range(n):
        fp = _zpad(fs[p], GLA_PAD)
        f_hi = fp.astype(BF16)
        r1 = fp - f_hi.astype(F32)
        f_mid = r1.astype(BF16)
        f_lo = (r1 - f_mid.astype(F32)).astype(BF16)
        cs = _nn(tri2, jnp.concatenate([f_hi, f_mid, f_lo], axis=1))
        tot = (cs[:, :GK] + cs[:, GK:2 * GK]) + cs[:, 2 * GK:]
        b.append(tot[:c])
        bl.append(tot[c:])
        bcol.append(jnp.sum(jnp.transpose(fp), axis=1, keepdims=True))
    qd, kd, qo, ko, qg, kbt, vexp, vpad = [], [], [], [], [], [], [], []
    for p in range(n):
        q, k, v = qs[p], ks[p], vs[p]
        qd.append((q * jnp.exp(bl[p])).astype(BF16))
        kd.append(expand_k(k * jnp.exp(-bl[p])))
        if nsub > 1:
            qj, kj = [], []
            for j in range(nsub - 1):
                e = (j + 1) * sub
                bj = b[p][e - 1:e]
                qj.append(jnp.where(rows >= e, q * jnp.exp(jnp.minimum(b[p] - bj, 0.0)), 0.0).astype(BF16))
                kj.append(expand_k(jnp.where((rows >= e - sub) & (rows < e),
                                             k * jnp.exp(jnp.minimum(bj - b[p], 0.0)), 0.0)))
            qo.append(jnp.concatenate(qj, axis=1))
            ko.append(jnp.concatenate(kj, axis=1))
        qg.append((q * jnp.exp(b[p])).astype(BF16))
        kbt.append(jnp.transpose(_zpad(k * jnp.exp(b[p][c - 1:c] - b[p]), GLA_PAD)).astype(BF16))
        vexp.append(jnp.concatenate(
            [_zpad(jnp.where(lane_v == h, v, 0.0), GLA_PAD) for h in range(GLA_HEADS)], axis=0).astype(BF16))
        vpad.append(_zpad(v, GLA_PAD).astype(BF16))
    a = []
    for p in range(n):
        ap = jnp.where(amask, _nt(qd[p], kd[p]), 0.0)
        if nsub > 1:
            ap = ap + _nt(qo[p], ko[p])
        a.append(ap.astype(BF16))
    return [(_nn(a[p], vexp[p]), qg[p], kbt[p], vpad[p], bcol[p]) for p in range(n)]


def _gla_inter_multi(parts, states):
    srow = lax.broadcasted_iota(jnp.int32, (GK, GV), 0) >> 6
    scol = lax.broadcasted_iota(jnp.int32, (GK, GV), 1) >> 7
    diag = srow == scol
    outs = [o + _nn(qg, s.astype(BF16)) for (o, qg, _, _, _), s in zip(parts, states)]
    new = [s * jnp.exp(bcol) + jnp.where(diag, _nn(kbt, vp), 0.0) for (_, _, kbt, vp, bcol), s in zip(parts, states)]
    return outs, new


def _gla_out(o, gg, gnorm):
    outs = []
    for h in range(GLA_HEADS):
        oh = o[:, h * GLA_DV:(h + 1) * GLA_DV]
        ms = jnp.mean(oh * oh, axis=-1, keepdims=True)
        outs.append(oh * lax.rsqrt(ms + NORM_EPS))
    on = jnp.concatenate(outs, axis=1) * gnorm
    return on * (gg / (1.0 + jnp.exp(-gg)))


def _mixer_prompt_kernel(x_ref, ct_ref, st_ref, anorm_ref, win_ref, qkg_ref, sink_ref, wgate_ref, bgate_ref,
                         gnorm_ref, wout_ref, y_ref, klast_ref, vlast_ref, sfin_ref,
                         kprev_ref, vprev_ref, s_ref, *, tm, chunk, sub):
    i = pl.program_id(0)

    @pl.when(i == 0)
    def _():
        kprev_ref[...] = jnp.zeros_like(kprev_ref)
        vprev_ref[...] = jnp.zeros_like(vprev_ref)
        s_ref[...] = jnp.zeros_like(s_ref)

    x = x_ref[...]
    q, k, v, gq, gk, gv, gg, logf = _project(
        x, ct_ref[...], st_ref[...], anorm_ref[...], win_ref[...], qkg_ref[...], wgate_ref[...], bgate_ref[...])

    kall = jnp.concatenate([kprev_ref[...], k], axis=0)
    vall = jnp.concatenate([vprev_ref[...], v], axis=0)
    qi = lax.broadcasted_iota(jnp.int32, (WINDOW, 2 * WINDOW), 0)
    kj = lax.broadcasted_iota(jnp.int32, (WINDOW, 2 * WINDOW), 1)
    band = (kj >= qi) & (kj <= qi + WINDOW)
    nb = tm // WINDOW
    masks = [band & (kj >= jnp.where(i == 0, WINDOW, 0)) if j == 0 else band for j in range(nb)]
    blocks = _swa_multi([q[j * WINDOW:(j + 1) * WINDOW] for j in range(nb)],
                        [kall[j * WINDOW:(j + 2) * WINDOW] for j in range(nb)],
                        [vall[j * WINDOW:(j + 2) * WINDOW] for j in range(nb)], masks, sink_ref)
    a_out = jnp.concatenate(blocks, axis=0)
    kprev_ref[...] = k[tm - WINDOW:]
    vprev_ref[...] = v[tm - WINDOW:]
    klast_ref[...] = k[tm - WINDOW:]
    vlast_ref[...] = v[tm - WINDOW:]

    sls = [slice(c * chunk, (c + 1) * chunk) for c in range(tm // chunk)]
    parts = _gla_intra_multi([gq[sl] for sl in sls], [gk[sl] for sl in sls], [gv[sl] for sl in sls],
                             [logf[sl] for sl in sls], sub)
    s = s_ref[...]
    os_ = []
    for part in parts:
        (o,), (s,) = _gla_inter_multi([part], [s])
        os_.append(o)
    s_ref[...] = s
    for h in range(GLA_HEADS):
        sfin_ref[h] = s[h * GLA_DK:(h + 1) * GLA_DK, h * GLA_DV:(h + 1) * GLA_DV]
    g_out = _gla_out(jnp.concatenate(os_, axis=0), gg, gnorm_ref[...])

    mix = jnp.concatenate([a_out, g_out], axis=1).astype(BF16)
    y_ref[...] = x + _nn(mix, wout_ref[...])


def _const_spec(shape):
    return pl.BlockSpec(shape, lambda *_: (0,) * len(shape))


def _mixer_prompt(x, ct, st, wts, *, tm=MIXER_TM, chunk=GLA_CHUNK, sub=GLA_SUB):
    t = x.shape[0]
    tm = min(tm, t)
    row = lambda w: pl.BlockSpec((tm, w), lambda i: (i, 0))
    kern = functools.partial(_mixer_prompt_kernel, tm=tm, chunk=chunk, sub=sub)
    return pl.pallas_call(
        kern,
        grid=(t // tm,),
        in_specs=[row(D_MODEL), row(LANES), row(LANES)] + [_const_spec(w.shape) for w in wts],
        out_specs=[row(D_MODEL), _const_spec((WINDOW, LANES)), _const_spec((WINDOW, LANES)),
                   _const_spec((GLA_HEADS, GLA_DK, GLA_DV))],
        out_shape=[jax.ShapeDtypeStruct((t, D_MODEL), F32), jax.ShapeDtypeStruct((WINDOW, LANES), F32),
                   jax.ShapeDtypeStruct((WINDOW, LANES), F32),
                   jax.ShapeDtypeStruct((GLA_HEADS, GLA_DK, GLA_DV), F32)],
        scratch_shapes=[pltpu.VMEM((WINDOW, LANES), F32), pltpu.VMEM((WINDOW, LANES), F32),
                        pltpu.VMEM((GK, GV), F32)],
        compiler_params=pltpu.CompilerParams(dimension_semantics=("arbitrary",),
                                             vmem_limit_bytes=MIXER_VMEM),
        name="mixer_prompt",
    )(x, ct, st, *wts)


def _mixer_sample_kernel(x_ref, ct_ref, st_ref, anorm_ref, win_ref, qkg_ref, sink_ref, wgate_ref, bgate_ref,
                         gnorm_ref, wout_ref, ck_ref, cv_ref, sin_ref,
                         y_ref, kout_ref, vout_ref, sout_ref, *, sb, dec, par):
    x = x_ref[...]
    q, k, v, gq, gk, gv, gg, logf = _project(
        x, ct_ref[...], st_ref[...], anorm_ref[...], win_ref[...], qkg_ref[...], wgate_ref[...], bgate_ref[...])
    for b in range(sb):
        kout_ref[b] = jnp.concatenate([ck_ref[b, dec:, :], k[b * dec:(b + 1) * dec]], axis=0)
        vout_ref[b] = jnp.concatenate([cv_ref[b, dec:, :], v[b * dec:(b + 1) * dec]], axis=0)

    per = 8 // dec
    qi = lax.broadcasted_iota(jnp.int32, (8, 2 * WINDOW), 0) & (dec - 1)
    kj = lax.broadcasted_iota(jnp.int32, (8, 2 * WINDOW), 1)
    mask = (kj >= qi) & (kj <= qi + WINDOW)
    ztail = jnp.zeros((WINDOW - dec, LANES), F32)
    pad8 = lambda a: jnp.concatenate([a, jnp.zeros((8 - dec, a.shape[1]), F32)], axis=0)

    a_parts, o_parts = [], []
    for b0 in range(0, sb, par):
        seqs = range(b0, b0 + par)
        rows8 = [slice((b // per) * 8, (b // per) * 8 + 8) for b in seqs]
        own = [slice((b % per) * dec, (b % per) * dec + dec) for b in seqs]
        new = [slice(b * dec, (b + 1) * dec) for b in seqs]
        att = _swa_multi([q[r] for r in rows8],
                         [jnp.concatenate([ck_ref[b], k[nw], ztail], axis=0) for b, nw in zip(seqs, new)],
                         [jnp.concatenate([cv_ref[b], v[nw], ztail], axis=0) for b, nw in zip(seqs, new)],
                         [mask] * par, sink_ref)
        a_parts += [a[o] for a, o in zip(att, own)]
        states = []
        for b in seqs:
            st = sin_ref[b]
            srows = []
            for h in range(GLA_HEADS):
                parts = []
                if h > 0:
                    parts.append(jnp.zeros((GLA_DK, h * GLA_DV), F32))
                parts.append(st[h])
                if h < GLA_HEADS - 1:
                    parts.append(jnp.zeros((GLA_DK, (GLA_HEADS - 1 - h) * GLA_DV), F32))
                srows.append(jnp.concatenate(parts, axis=1))
            states.append(jnp.concatenate(srows, axis=0))
        intra = _gla_intra_multi([pad8(gq[nw]) for nw in new], [pad8(gk[nw]) for nw in new],
                                 [pad8(gv[nw]) for nw in new], [pad8(logf[nw]) for nw in new], 8)
        outs, states = _gla_inter_multi(intra, states)
        o_parts += [o[:dec] for o in outs]
        for b, s_new in zip(seqs, states):
            for h in range(GLA_HEADS):
                sout_ref[b, h] = s_new[h * GLA_DK:(h + 1) * GLA_DK, h * GLA_DV:(h + 1) * GLA_DV]

    g_out = _gla_out(jnp.concatenate(o_parts, axis=0), gg, gnorm_ref[...])
    mix = jnp.concatenate([jnp.concatenate(a_parts, axis=0), g_out], axis=1).astype(BF16)
    y_ref[...] = x + _nn(mix, wout_ref[...])


def _mixer_sample(x, ct, st, wts, ck, cv, s0, *, dec, sb=SAMPLE_SEQS, par=SAMPLE_PAR):
    n = x.shape[0]
    nseq = n // dec
    m = sb * dec
    row = lambda w: pl.BlockSpec((m, w), lambda i: (i, 0))
    seq3 = pl.BlockSpec((sb, WINDOW, LANES), lambda i: (i, 0, 0))
    seq4 = pl.BlockSpec((sb, GLA_HEADS, GLA_DK, GLA_DV), lambda i: (i, 0, 0, 0))
    kern = functools.partial(_mixer_sample_kernel, sb=sb, dec=dec, par=par)
    return pl.pallas_call(
        kern,
        grid=(nseq // sb,),
        in_specs=[row(D_MODEL), row(LANES), row(LANES)] + [_const_spec(w.shape) for w in wts] + [seq3, seq3, seq4],
        out_specs=[row(D_MODEL), seq3, seq3, seq4],
        out_shape=[jax.ShapeDtypeStruct((n, D_MODEL), F32), jax.ShapeDtypeStruct(ck.shape, F32),
                   jax.ShapeDtypeStruct(cv.shape, F32), jax.ShapeDtypeStruct(s0.shape, F32)],
        compiler_params=pltpu.CompilerParams(dimension_semantics=("arbitrary",),
                                             vmem_limit_bytes=MIXER_VMEM),
        name="mixer_sample",
    )(x, ct, st, *wts, ck, cv, s0)


def _sort_network(n):
    pairs = []

    def merge(lo, m, r):
        step = 2 * r
        if step < m:
            merge(lo, m, step)
            merge(lo + r, m, step)
            pairs.extend((i, i + r) for i in range(lo + r, lo + m - r, step))
        else:
            pairs.append((lo, lo + r))

    def sort(lo, m):
        if m > 1:
            sort(lo, m // 2)
            sort(lo + m // 2, m // 2)
            merge(lo, m, 1)

    sort(0, n)
    return pairs


def _top16(s):
    t = s.shape[1]
    nv = PEER_NKEYS // 8
    sub8 = lax.broadcasted_iota(jnp.int32, (8, t), 0).astype(F32)
    v = [s[8 * k:8 * k + 8] for k in range(nv)]
    ix = [sub8 + float(8 * k) for k in range(nv)]
    for p, q in _sort_network(nv):
        swap = (v[q] > v[p]) | ((v[q] == v[p]) & (ix[q] < ix[p]))
        v[p], v[q] = jnp.where(swap, v[q], v[p]), jnp.where(swap, v[p], v[q])
        ix[p], ix[q] = jnp.where(swap, ix[q], ix[p]), jnp.where(swap, ix[p], ix[q])
    vals, idxs = [], []
    for n in range(PEER_TOPK):
        m = jnp.max(v[0], axis=0, keepdims=True)
        imin = jnp.min(jnp.where(v[0] == m, ix[0], float(PEER_NKEYS)), axis=0, keepdims=True)
        hit = ix[0] == imin
        vals.append(m)
        idxs.append(imin)
        for k in range(PEER_TOPK - 1 - n):
            v[k] = jnp.where(hit, v[k + 1], v[k])
            ix[k] = jnp.where(hit, ix[k + 1], ix[k])
    return jnp.concatenate(vals, axis=0), jnp.concatenate(idxs, axis=0)


def _pair_top16(a, ia, b, ib):
    t = a.shape[1]
    sub8 = lax.broadcasted_iota(jnp.int32, (8, t), 0).astype(F32)
    ia_lo = ia[0:8] * float(PEER_NKEYS)
    lv, le = [], []
    for y in range(PEER_TOPK):
        nx = PEER_TOPK // (y + 1)
        val = a[0:8] + b[y:y + 1]
        lv.append(val if nx >= 8 else jnp.where(sub8 < float(nx), val, NEG_INF))
        le.append(ia_lo + ib[y:y + 1])
    hv = a[8:16] + b[0:1]
    he = ia[8:16] * float(PEER_NKEYS) + ib[0:1]
    code_lo = sub8 * float(PEER_TOPK)
    code_hi = (sub8 + 8.0) * float(PEER_TOPK)
    vals, sel = [], []
    for n in range(PEER_TOPK):
        m = jnp.max(jnp.maximum(lv[0], hv), axis=0, keepdims=True)
        cmin = jnp.min(jnp.minimum(jnp.where(lv[0] == m, code_lo, 1e9), jnp.where(hv == m, code_hi, 1e9)),
                       axis=0, keepdims=True)
        hit_lo = code_lo == cmin
        hit_hi = code_hi == cmin
        sel.append(jnp.max(jnp.maximum(jnp.where(hit_lo, le[0], -1.0), jnp.where(hit_hi, he, -1.0)),
                           axis=0, keepdims=True))
        vals.append(m)
        for y in range(PEER_TOPK - 1 - n):
            lv[y] = jnp.where(hit_lo, lv[y + 1], lv[y])
            le[y] = jnp.where(hit_lo, le[y + 1], le[y])
        code_lo = jnp.where(hit_lo, code_lo + 1.0, code_lo)
        hv = jnp.where(hit_hi, NEG_INF, hv)
    return jnp.concatenate(vals, axis=0), jnp.concatenate(sel, axis=0)


def _peer_route_gates_kernel(xa_ref, xb_ref, fnorm_ref, wqt_ref, keys_ref, h_ref, w_ref,
                             qt_s, e_s, p_s, i_s, j_s, g_s, t_s, *, tm, nta, nt):
    t = pl.program_id(0)
    cur = lax.rem(t, 2)
    prev = 1 - cur
    group = tm // PEER_HEADS

    @pl.when(t == 0)
    def _():
        i_s[...] = jnp.zeros_like(i_s)
        j_s[...] = jnp.zeros_like(j_s)
        g_s[...] = jnp.zeros_like(g_s)

    x = jnp.where(jnp.minimum(t, nt - 1) < nta, xa_ref[...], xb_ref[...])
    hb = _rms(x, fnorm_ref[...]).astype(BF16)
    h_ref[...] = hb
    qt_s[...] = _nt(wqt_ref[...], hb)
    sub = lax.broadcasted_iota(jnp.int32, (PEER_NKEYS, LANES), 0).astype(BF16)
    sub_odd = (lax.broadcasted_iota(jnp.int32, (W_ROWS_ODD, LANES), 0) - W_SHIFT).astype(BF16)
    one = jnp.ones((W_ROWS_ODD, LANES), BF16)
    zero = jnp.zeros((W_ROWS_ODD, LANES), BF16)

    def emit(n0, q):
        for i in range(PEER_NKEYS):
            a = t_s[pl.ds((16 * q) * W_PITCH + i, 8, stride=W_PITCH), :]
            b = t_s[pl.ds((16 * q + 8) * W_PITCH + i, 8, stride=W_PITCH), :]
            w_ref[pl.ds(n0 + 16 * q, 16), i * PEER_NKEYS:(i + 1) * PEER_NKEYS] = (
                jnp.concatenate([a, b], axis=0).astype(BF16))

    def head(hh, carry):
        r0 = pl.multiple_of(hh * 2 * PEER_NKEYS, 2 * PEER_NKEYS)
        s0 = _nn(keys_ref[2 * hh], qt_s[pl.ds(r0, PEER_NKEYS), :].astype(BF16))
        s1 = _nn(keys_ref[2 * hh + 1], qt_s[pl.ds(r0 + PEER_NKEYS, PEER_NKEYS), :].astype(BF16))
        n0 = pl.multiple_of(hh * group, group)
        for tok in range(group):
            ri = i_s[prev, pl.ds(n0 + tok, 1), :].astype(BF16)
            rj = j_s[prev, pl.ds(n0 + tok, 1), :].astype(BF16)
            rg = jnp.broadcast_to(g_s[prev, pl.ds(n0 + tok, 1), :].astype(BF16), sub.shape)
            bt = jnp.where(sub == rj, rg, zero[:PEER_NKEYS])
            if tok % 2 == 0:
                at = jnp.where(sub == ri, one[:PEER_NKEYS], zero[:PEER_NKEYS])
                t_s[tok * W_PITCH:tok * W_PITCH + PEER_NKEYS, :] = _nt(at, bt)
            else:
                at = jnp.where(sub_odd == ri, one, zero)
                t_s[tok * W_PITCH - W_SHIFT:tok * W_PITCH - W_SHIFT + W_ROWS_ODD, :] = _nt(at, bt)
        a, ia = _top16(s0)
        for q in range(group // 32):
            emit(n0, 2 * q)
        b, ib = _top16(s1)
        for q in range(group // 32):
            emit(n0, 2 * q + 1)
        c, e = _pair_top16(a, ia, b, ib)
        p = jnp.exp(c - c[0:1])
        rows = pl.ds(pl.multiple_of(hh * PEER_TOPK, PEER_TOPK), PEER_TOPK)
        e_s[rows, :] = e
        p_s[rows, :] = p / jnp.sum(p, axis=0, keepdims=True)
        return carry

    lax.fori_loop(0, PEER_HEADS, head, 0)
    et = jnp.transpose(e_s[...])
    it = jnp.floor(et * (1.0 / PEER_NKEYS))
    i_s[cur] = it
    j_s[cur] = et - it * float(PEER_NKEYS)
    g_s[cur] = jnp.transpose(p_s[...])


def _peer_route_gates(xa, xb, fnorm, wq, keys, *, tm=ROUTE_TM):
    assert xa.shape[0] % tm == 0 and xb.shape[0] % tm == 0
    nta, ntb = xa.shape[0] // tm, xb.shape[0] // tm
    nt = nta + ntb
    t = nt * tm
    nsel = PEER_HEADS * PEER_TOPK
    n_exp = PEER_NKEYS * PEER_NKEYS
    group = tm // PEER_HEADS
    assert group % 32 == 0
    cur_tile = lambda i: (jnp.minimum(i, nt - 1), 0)
    return pl.pallas_call(
        functools.partial(_peer_route_gates_kernel, tm=tm, nta=nta, nt=nt),
        grid=(nt + 1,),
        in_specs=[pl.BlockSpec((tm, D_MODEL), lambda i: (jnp.minimum(i, nta - 1), 0)),
                  pl.BlockSpec((tm, D_MODEL), lambda i: (jnp.clip(i - nta, 0, ntb - 1), 0)),
                  _const_spec(fnorm.shape), _const_spec(wq.shape), _const_spec(keys.shape)],
        out_specs=[pl.BlockSpec((tm, D_MODEL), cur_tile),
                   pl.BlockSpec((tm, n_exp), lambda i: (jnp.maximum(i - 1, 0), 0))],
        out_shape=[jax.ShapeDtypeStruct((t, D_MODEL), BF16), jax.ShapeDtypeStruct((t, n_exp), BF16)],
        scratch_shapes=[pltpu.VMEM((wq.shape[0], tm), F32), pltpu.VMEM((nsel, tm), F32),
                        pltpu.VMEM((nsel, tm), F32),
                        pltpu.VMEM((2, tm, nsel), F32), pltpu.VMEM((2, tm, nsel), F32),
                        pltpu.VMEM((2, tm, nsel), F32),
                        pltpu.VMEM(((group - 1) * W_PITCH - W_SHIFT + W_ROWS_ODD, LANES), F32)],
        compiler_params=pltpu.CompilerParams(dimension_semantics=("arbitrary",),
                                             vmem_limit_bytes=PEER_VMEM),
        name="peer_route_gates",
    )(xa, xb, fnorm, wq, keys)


def _peer_dense_kernel(h_ref, x_ref, w_ref, u_ref, v_ref, hs_ref, xs_ref, ws_ref, y_ref, ys_ref):
    e = pl.program_id(1)
    last_tile = pl.program_id(0) == pl.num_programs(0) - 1

    def experts(h, w):
        act = _nt(h, u_ref[...].astype(BF16))
        gelu = 0.5 * act * (1.0 + lax.erf(act * (2.0 ** -0.5)))
        return _nn(w * gelu.astype(BF16), v_ref[...].astype(BF16))

    @pl.when(e == 0)
    def _():
        y_ref[...] = x_ref[...]

    y_ref[...] += experts(h_ref[...], w_ref[...])

    @pl.when(last_tile & (e == 0))
    def _():
        ys_ref[...] = xs_ref[...]

    @pl.when(last_tile)
    def _():
        ys_ref[...] += experts(hs_ref[...], ws_ref[...])


def _peer_dense(h, w, x, xs, u, v, *, tm=DENSE_TM, te=DENSE_TE):
    t, ts = x.shape[0], xs.shape[0]
    tm = min(tm, t)
    assert t % tm == 0 and t % ts == 0
    nt = t // tm
    n_exp = u.shape[0]
    tok = pl.BlockSpec((tm, D_MODEL), lambda i, e: (i, 0))
    exp = pl.BlockSpec((te, D_MODEL), lambda i, e: (e, 0))
    once = dict(pipeline_mode=pl.Buffered(1))
    return pl.pallas_call(
        _peer_dense_kernel,
        grid=(nt, n_exp // te),
        in_specs=[tok, tok, pl.BlockSpec((tm, te), lambda i, e: (i, e)), exp, exp,
                  pl.BlockSpec((ts, D_MODEL), lambda i, e: (t // ts, 0), **once),
                  pl.BlockSpec((ts, D_MODEL), lambda i, e: (0, 0), **once),
                  pl.BlockSpec((ts, te), lambda i, e: (t // ts, jnp.where(i == nt - 1, e, 0)))],
        out_specs=[tok, pl.BlockSpec((ts, D_MODEL), lambda i, e: (0, 0))],
        out_shape=[jax.ShapeDtypeStruct((t, D_MODEL), F32), jax.ShapeDtypeStruct((ts, D_MODEL), F32)],
        compiler_params=pltpu.CompilerParams(dimension_semantics=("arbitrary", "arbitrary"),
                                             vmem_limit_bytes=DENSE_VMEM),
        name="peer_dense",
    )(h, x, w, u, v, h, xs, w)


def _rope_tables(pos):
    half = ROPE_DIM // 2
    inv_freq = ROPE_THETA ** (-jnp.arange(half, dtype=F32) / half)
    ang = pos.astype(F32)[:, None] * inv_freq[None, :]
    cos, sin = jnp.cos(ang), jnp.sin(ang)
    n = pos.shape[0]
    c64 = jnp.concatenate([cos, cos, jnp.ones((n, HEAD_DIM - ROPE_DIM), F32)], axis=1)
    s64 = jnp.concatenate([-sin, sin, jnp.zeros((n, HEAD_DIM - ROPE_DIM), F32)], axis=1)
    return jnp.tile(c64, (1, 2)), jnp.tile(s64, (1, 2))


def kernel(x_prompt, x_sample, cache_swa_k, cache_swa_v, state_gla, attn_norm, w_in, q_norm, k_norm, attn_sinks,
           w_gate, b_gate, gla_norm, w_out, ffn_norm, peer_wq, peer_keys, peer_u, peer_v):
    depth = w_in.shape[0]
    assert depth == 1 and x_prompt.shape[0] == 1
    seq = x_prompt.shape[1]
    nseq, dec = x_sample.shape[0], x_sample.shape[1]
    xp = x_prompt[0]
    xs = x_sample.reshape(nseq * dec, D_MODEL)
    l = 0

    win = jnp.pad(w_in[l], ((0, 0), (0, C_END - w_in.shape[2]))).astype(BF16)
    qkg = jnp.concatenate([jnp.tile(q_norm[l], SWA_Q_HEADS), jnp.tile(k_norm[l], 2)])[None, :]
    sinks = jnp.broadcast_to(attn_sinks[l][:, None], (SWA_Q_HEADS, LANES))
    wgate = jnp.pad(w_gate[l], ((0, LANES - GLA_LOWRANK), (0, 0))).astype(BF16)
    wts = (attn_norm[l][None, :], win, qkg, sinks, wgate, b_gate[l][None, :],
           jnp.tile(gla_norm[l], GLA_HEADS)[None, :], w_out[l].astype(BF16))
    fnorm = ffn_norm[l][None, :]
    wq = jnp.transpose(peer_wq[l]).astype(BF16)
    keys = peer_keys[l].reshape(PEER_HEADS * 2, PEER_NKEYS, peer_keys.shape[-1]).astype(BF16)

    ct_p, st_p = _rope_tables(jnp.arange(seq, dtype=jnp.int32))
    ct_s, st_s = _rope_tables(PAST_LEN + jnp.arange(nseq * dec, dtype=jnp.int32) % dec)

    xp2, kp, vp, sp = _mixer_prompt(xp, ct_p, st_p, wts)
    ck = cache_swa_k[l].reshape(nseq, WINDOW, LANES)
    cv = cache_swa_v[l].reshape(nseq, WINDOW, LANES)
    xs2, ck2, cv2, ss = _mixer_sample(xs, ct_s, st_s, wts, ck, cv, state_gla[l], dec=dec)

    h_all, w_all = _peer_route_gates(xp2, xs2, fnorm, wq, keys)
    yp, ys = _peer_dense(h_all, w_all, xp2, xs2, peer_u[l], peer_v[l])

    kv_shape = (1, 1, WINDOW, 2, HEAD_DIM)
    return (yp[None], ys.reshape(nseq, dec, D_MODEL),
            kp.reshape(kv_shape), vp.reshape(kv_shape), sp[None, None],
            ck2.reshape(1, nseq, WINDOW, 2, HEAD_DIM), cv2.reshape(1, nseq, WINDOW, 2, HEAD_DIM), ss[None])
```
